```python
import math
import jax
import jax.numpy as jnp
from jax import lax
import numpy as np

D_MODEL = 2048
BATCH = 2
SEQ = 4096
DEPTH = 1

GRID_W = 64
CTX_LEN = 256
EPS = 1e-6
N_MOD = 6

RW_HEAD_DIM = 64
RW_WIDTH = D_MODEL // 2
RW_HEADS = RW_WIDTH // RW_HEAD_DIM
DECAY_LORA = 64
AAA_LORA = 64
GATE_LORA = 160
LNX_EPS = 64e-5
RW_STATE_COLS = 2 * RW_WIDTH + 2 * DECAY_LORA + 2 * AAA_LORA
RW_COLS = RW_STATE_COLS + RW_WIDTH + GATE_LORA

RET_WIDTH = D_MODEL - RW_WIDTH
RET_HEADS = 8
RET_HEAD_DIM = RET_WIDTH // RET_HEADS
RET_CHUNK = 128
ROPE_BASE = 10000.0
RET_STATE_COLS = 2 * RET_WIDTH
RET_COLS = 4 * RET_WIDTH

P_IN = RW_COLS + RET_COLS
D_MIX = RW_WIDTH + RET_WIDTH
D_FF = 5632

kernel_name = 'hybrid_rwkv7_retention_convglu_dit'


def rmsnorm(x, g):
    xf = x.astype(jnp.float32)
    y = xf * lax.rsqrt(jnp.mean(xf * xf, axis=-1, keepdims=True) + EPS)
    return (y * g.astype(jnp.float32)).astype(x.dtype)


def dwconv_seq(x, w):
    return lax.conv_general_dilated(x, w[:, None, :].astype(x.dtype), (1,), ((1, 1),),
                                    dimension_numbers=('NWC', 'WIO', 'NWC'),
                                    feature_group_count=x.shape[-1])


def dwconv_grid(x, w, b, rows):
    B, L, C = x.shape
    y = lax.conv_general_dilated(x.reshape(B, rows, GRID_W, C), w[:, :, None, :].astype(x.dtype),
                                 (1, 1), ((1, 1), (1, 1)),
                                 dimension_numbers=('NHWC', 'HWIO', 'NHWC'),
                                 feature_group_count=C)
    return y.reshape(B, L, C) + b


def rope_grid(x):
    L, d = x.shape[2], x.shape[3]
    n = d // 4
    t = jnp.arange(L)
    inv = ROPE_BASE ** (-jnp.arange(n, dtype=jnp.float32) / n)
    ang_row = (t // GRID_W).astype(jnp.float32)[:, None] * inv
    ang_col = (t % GRID_W).astype(jnp.float32)[:, None] * inv

    def rot(u, ang):
        u1, u2 = jnp.split(u, 2, axis=-1)
        cos, sin = jnp.cos(ang).astype(u.dtype), jnp.sin(ang).astype(u.dtype)
        return jnp.concatenate([u1 * cos - u2 * sin, u1 * sin + u2 * cos], axis=-1)

    x_row, x_col = jnp.split(x, 2, axis=-1)
    return jnp.concatenate([rot(x_row, ang_row), rot(x_col, ang_col)], axis=-1)


def rw_heads(t):
    B, L, _ = t.shape
    return t.reshape(B, L, RW_HEADS, RW_HEAD_DIM)


def rwkv_state_inputs(rw, w0, w_up, a0, a_up, k_k, k_a):
    f32 = jnp.float32
    W = RW_WIDTH
    k = rw[..., :W].astype(f32)
    v = rw[..., W:2 * W].astype(f32)
    lo = rw[..., 2 * W:RW_STATE_COLS].astype(f32)
    wd = (lo[..., :DECAY_LORA], lo[..., DECAY_LORA:2 * DECAY_LORA])
    ad = (lo[..., 2 * DECAY_LORA:2 * DECAY_LORA + AAA_LORA], lo[..., 2 * DECAY_LORA + AAA_LORA:])
    kk = rw_heads(k * k_k)
    kk = kk * lax.rsqrt(jnp.sum(kk * kk, axis=-1, keepdims=True) + 1e-12)
    dirs = []
    for d in range(2):
        w_log = -jax.nn.softplus(-(w0[d] + jnp.tanh(wd[d]) @ w_up[d])) - 0.5
        decay = jnp.exp(-jnp.exp(w_log))
        a = jax.nn.sigmoid(a0[d] + ad[d] @ a_up[d])
        k_d = k * (1.0 + (a - 1.0) * k_a)
        dirs.append((rw_heads(decay), rw_heads(k_d), rw_heads(a)))
    return rw_heads(v), kk, dirs


def rwkv_dir(r, decay, k, v, kk, a, s0, reverse):
    readout = r is not None
    xs = [jnp.moveaxis(t, 1, 0) for t in (decay, k, v, -kk, kk * a)]
    if readout:
        xs.append(jnp.moveaxis(r, 1, 0))

    def step(s, inp):
        w_t, k_t, v_t, a_t, b_t = inp[:5]
        s = (s * w_t[:, :, None, :]
             + jnp.einsum('bhij,bhj->bhi', s, a_t)[..., None] * b_t[:, :, None, :]
             + v_t[..., None] * k_t[:, :, None, :])
        y = jnp.einsum('bhij,bhj->bhi', s, inp[5]) if readout else None
        return s, y

    s_fin, ys = lax.scan(step, s0, tuple(xs), reverse=reverse)
    return (jnp.moveaxis(ys, 0, 1) if readout else None), s_fin


def rwkv_group(rw, s0, w0, w_up, a0, a_up, g_up, k_k, k_a, r_k, lnx_w, lnx_b, readout):
    v, kk, dirs = rwkv_state_inputs(rw, w0, w_up, a0, a_up, k_k, k_a)
    r = rw_heads(rw[..., RW_STATE_COLS:RW_STATE_COLS + RW_WIDTH].astype(jnp.float32)) if readout else None
    ys, states = [], []
    for d in range(2):
        decay, k_d, a = dirs[d]
        y, s = rwkv_dir(r, decay, k_d, v, kk, a, s0[d], reverse=(d == 1))
        ys.append(y)
        states.append(s)
    if not readout:
        return None, states
    B, L = rw.shape[0], rw.shape[1]
    y = ys[0] + ys[1]
    mu = jnp.mean(y, axis=-1, keepdims=True)
    var = jnp.mean(jnp.square(y - mu), axis=-1, keepdims=True)
    y = ((y - mu) * lax.rsqrt(var + LNX_EPS)).reshape(B, L, RW_WIDTH) * lnx_w + lnx_b
    bonus = (jnp.sum(r * dirs[0][1] * r_k, axis=-1, keepdims=True)
             + jnp.sum(r * dirs[1][1] * r_k, axis=-1, keepdims=True)) * v
    g = jax.nn.sigmoid(rw[..., RW_STATE_COLS + RW_WIDTH:].astype(jnp.float32)) @ g_up
    return ((y + bonus.reshape(B, L, RW_WIDTH)) * g).astype(rw.dtype), states


def ret_heads(t):
    B, L, _ = t.shape
    return t.reshape(B, L, RET_HEADS, RET_HEAD_DIM).transpose(0, 2, 1, 3)


def retention_dir(q, k, v, log_g, s0, strict):
    B, H, L, d = k.shape
    n = L // RET_CHUNK
    pos = jnp.arange(RET_CHUNK, dtype=jnp.float32)
    kc = k.reshape(B, H, n, RET_CHUNK, d)
    vc = v.reshape(B, H, n, RET_CHUNK, v.shape[-1])
    to_end = jnp.exp((RET_CHUNK - 1.0 - pos)[None, :] * log_g[:, None])
    u = jnp.einsum('bhncd,bhnce->nbhde', kc * to_end[None, :, None, :, None], vc)
    g_chunk = jnp.exp(RET_CHUNK * log_g)[None, :, None, None]

    def step(s, u_n):
        return g_chunk * s + u_n, s

    s_fin, s_prev = lax.scan(step, s0, u)
    if q is None:
        return None, s_fin
    qc = q.reshape(B, H, n, RET_CHUNK, d)
    diff = pos[:, None] - pos[None, :]
    mask = (diff > 0) if strict else (diff >= 0)
    decay = jnp.where(mask, jnp.exp(jnp.where(mask, diff, 0.0)[None] * log_g[:, None, None]), 0.0)
    scores = jnp.einsum('bhnid,bhnjd->bhnij', qc, kc) * decay[None, :, None]
    intra = jnp.einsum('bhnij,bhnje->bhnie', scores, vc)
    from_start = jnp.exp((pos + 1.0)[None, :] * log_g[:, None])
    cross = jnp.einsum('bhnid,nbhde->bhnie', qc * from_start[None, :, None, :, None], s_prev)
    return (intra + cross).reshape(B, H, L, -1), s_fin


def retention_group(ret, s0, rope, readout):
    f32 = jnp.float32
    W = RET_WIDTH
    log_g = jnp.log(1.0 - 2.0 ** (-5.0 - jnp.arange(RET_HEADS, dtype=f32)))
    k = ret_heads(ret[..., :W].astype(f32))
    v = ret_heads(ret[..., W:2 * W].astype(f32))
    q = ret_heads(ret[..., 2 * W:3 * W].astype(f32)) if readout else None
    if rope:
        q, k = rope_grid(q), rope_grid(k)
    k = k * (RET_HEAD_DIM ** -0.5)
    flip = lambda t: jnp.flip(t, axis=2)
    y_f, s_f = retention_dir(q, k, v, log_g, s0[0], strict=False)
    y_b, s_b = retention_dir(None if q is None else flip(q), flip(k), flip(v), log_g, s0[1], strict=True)
    if not readout:
        return None, [s_f, s_b]
    B, L = ret.shape[0], ret.shape[1]
    y = (y_f + flip(y_b)).transpose(0, 2, 1, 3)
    y = y * lax.rsqrt(jnp.mean(y * y, axis=-1, keepdims=True) + EPS)
    g = jax.nn.silu(ret[..., 3 * W:].astype(f32))
    return (y.reshape(B, L, W) * g).astype(ret.dtype), [s_f, s_b]


def token_mixer(xm, xm_ctx, w_in, rw_conv, w0, w_up, a0, a_up, g_up, k_k, k_a, r_k, lnx_w, lnx_b,
                w_out, ctx_out):
    B = xm.shape[0]
    rw_params = (w0, w_up, a0, a_up, g_up, k_k, k_a, r_k, lnx_w, lnx_b)
    if ctx_out:
        pc = xm_ctx @ w_in
        rw_c = dwconv_seq(pc[..., :RW_COLS], rw_conv)
        ret_c = pc[..., RW_COLS:]
    else:
        rw_c = dwconv_seq(xm_ctx @ w_in[:, :RW_STATE_COLS], rw_conv[:, :RW_STATE_COLS])
        ret_c = xm_ctx @ w_in[:, RW_COLS:RW_COLS + RET_STATE_COLS]
    z_rw = jnp.zeros((B, RW_HEADS, RW_HEAD_DIM, RW_HEAD_DIM), jnp.float32)
    z_ret = jnp.zeros((B, RET_HEADS, RET_HEAD_DIM, RET_HEAD_DIM), jnp.float32)
    o_rw_c, s_rw_c = rwkv_group(rw_c, (z_rw, z_rw), *rw_params, readout=ctx_out)
    o_ret_c, s_ret_c = retention_group(ret_c, (z_ret, z_ret), rope=False, readout=ctx_out)
    p = xm @ w_in
    o_rw, _ = rwkv_group(dwconv_seq(p[..., :RW_COLS], rw_conv), s_rw_c, *rw_params, readout=True)
    o_ret, _ = retention_group(p[..., RW_COLS:], s_ret_c, rope=True, readout=True)
    out = jnp.concatenate([o_rw, o_ret], axis=-1) @ w_out
    out_c = (jnp.concatenate([o_rw_c, o_ret_c], axis=-1) @ w_out) if ctx_out else None
    return out, out_c


def conv_ffn(h, w_gate, w_up, conv_w, conv_b, w_down, rows):
    gt = h @ w_gate
    gt = dwconv_grid(gt, conv_w, conv_b, rows) if rows is not None else dwconv_seq(gt, conv_w[1]) + conv_b
    return (jax.nn.silu(gt) * (h @ w_up)) @ w_down


def setup_inputs(seed: int = 0) -> dict:
    key = jax.random.key(seed)
    ks = jax.random.split(key, 28)
    f32 = jnp.float32
    D = D_MODEL

    def nrm(k, shape, scale):
        return jax.random.normal(k, shape, f32) * scale

    ratio = jnp.arange(RW_WIDTH, dtype=f32) / (RW_WIDTH - 1)
    w0_base = -6.0 + 5.0 * ratio
    return {
        'x': nrm(ks[0], (BATCH, SEQ, D), 1.0),
        'c': nrm(ks[1], (BATCH, D), 1.0),
        'ctx': nrm(ks[2], (BATCH, CTX_LEN, D), 1.0),
        'c_ctx': nrm(ks[3], (D,), 1.0),
        'w_ada': nrm(ks[4], (DEPTH, D, N_MOD * D), 0.5 * D ** -0.5),
        'b_ada': nrm(ks[5], (DEPTH, N_MOD * D), 0.02),
        'norm_pre_mix': 1.0 + nrm(ks[6], (DEPTH, D), 0.02),
        'norm_post_mix': 1.0 + nrm(ks[7], (DEPTH, D), 0.02),
        'norm_pre_ffn': 1.0 + nrm(ks[8], (DEPTH, D), 0.02),
        'norm_post_ffn': 1.0 + nrm(ks[9], (DEPTH, D), 0.02),
        'w_in': nrm(ks[10], (DEPTH, D, P_IN), D ** -0.5),
        'rw_conv': nrm(ks[11], (DEPTH, 3, RW_COLS), 3 ** -0.5),
        'rw_w0': w0_base + nrm(ks[12], (DEPTH, 2, RW_WIDTH), 0.1),
        'rw_w_up': nrm(ks[13], (DEPTH, 2, DECAY_LORA, RW_WIDTH), 0.1 * DECAY_LORA ** -0.5),
        'rw_a0': nrm(ks[14], (DEPTH, 2, RW_WIDTH), 0.1),
        'rw_a_up': nrm(ks[15], (DEPTH, 2, AAA_LORA, RW_WIDTH), AAA_LORA ** -0.5),
        'rw_g_up': nrm(ks[16], (DEPTH, GATE_LORA, RW_WIDTH), GATE_LORA ** -0.5),
        'rw_k_k': 0.85 + nrm(ks[17], (DEPTH, RW_WIDTH), 0.02),
        'rw_k_a': 1.0 + nrm(ks[18], (DEPTH, RW_WIDTH), 0.02),
        'rw_r_k': nrm(ks[19], (DEPTH, RW_HEADS, RW_HEAD_DIM), 0.1),
        'rw_lnx_w': 1.0 + nrm(ks[20], (DEPTH, RW_WIDTH), 0.02),
        'rw_lnx_b': nrm(ks[21], (DEPTH, RW_WIDTH), 0.02),
        'w_out': nrm(ks[22], (DEPTH, D_MIX, D), D_MIX ** -0.5),
        'ffn_w_gate': nrm(ks[23], (DEPTH, D, D_FF), D ** -0.5),
        'ffn_w_up': nrm(ks[24], (DEPTH, D, D_FF), D ** -0.5),
        'ffn_conv': nrm(ks[25], (DEPTH, 3, 3, D_FF), 1.0 / 3.0),
        'ffn_conv_b': nrm(ks[26], (DEPTH, D_FF), 0.02),
        'ffn_w_down': nrm(ks[27], (DEPTH, D_FF, D), D_FF ** -0.5),
    }


def reference(x, c, ctx, c_ctx, w_ada, b_ada, norm_pre_mix, norm_post_mix, norm_pre_ffn, norm_post_ffn,
              w_in, rw_conv, rw_w0, rw_w_up, rw_a0, rw_a_up, rw_g_up, rw_k_k, rw_k_a, rw_r_k,
              rw_lnx_w, rw_lnx_b, w_out, ffn_w_gate, ffn_w_up, ffn_conv, ffn_conv_b, ffn_w_down):
    rows = x.shape[1] // GRID_W
    for l in range(DEPTH):
        ctx_out = l < DEPTH - 1
        mod = jax.nn.silu(c) @ w_ada[l] + b_ada[l]
        sh1, sc1, g1, sh2, sc2, g2 = jnp.split(mod[:, None, :], N_MOD, axis=-1)
        n_c = N_MOD if ctx_out else 2
        mod_c = jnp.split(jax.nn.silu(c_ctx) @ w_ada[l][:, :n_c * D_MODEL] + b_ada[l][:n_c * D_MODEL], n_c)

        xm = rmsnorm(x, norm_pre_mix[l]) * (1.0 + sc1) + sh1
        xm_c = rmsnorm(ctx, norm_pre_mix[l]) * (1.0 + mod_c[1]) + mod_c[0]
        out, out_c = token_mixer(xm, xm_c, w_in[l], rw_conv[l], rw_w0[l], rw_w_up[l], rw_a0[l], rw_a_up[l],
                                 rw_g_up[l], rw_k_k[l], rw_k_a[l], rw_r_k[l], rw_lnx_w[l], rw_lnx_b[l],
                                 w_out[l], ctx_out)
        x = x + g1 * rmsnorm(out, norm_post_mix[l])
        h = rmsnorm(x, norm_pre_ffn[l]) * (1.0 + sc2) + sh2
        y = conv_ffn(h, ffn_w_gate[l], ffn_w_up[l], ffn_conv[l], ffn_conv_b[l], ffn_w_down[l], rows)
        x = x + g2 * rmsnorm(y, norm_post_ffn[l])
        if ctx_out:
            ctx = ctx + mod_c[2] * rmsnorm(out_c, norm_post_mix[l])
            hc = rmsnorm(ctx, norm_pre_ffn[l]) * (1.0 + mod_c[4]) + mod_c[3]
            yc = conv_ffn(hc, ffn_w_gate[l], ffn_w_up[l], ffn_conv[l], ffn_conv_b[l], ffn_w_down[l], None)
            ctx = ctx + mod_c[5] * rmsnorm(yc, norm_post_ffn[l])
    return x
```

```python
import functools

import jax
import jax.numpy as jnp
import numpy as np
from jax import lax
from jax.experimental import pallas as pl
from jax.experimental.pallas import tpu as pltpu

F32 = jnp.float32
BF16 = jnp.bfloat16
HI = lax.Precision.HIGHEST

LANES = 128
EPS = 1e-6
GRID_W = 64
RW_HEAD_DIM = 64
DECAY_LORA = 64
AAA_LORA = 64
GATE_LORA = 160
LNX_EPS = 64e-5
RET_HEADS = 8
RET_CHUNK = 128
ROPE_BASE = 10000.0
RW_CHUNK = 64
RW_SUB = 16
VMEM_LIMIT = 56 * 1024 * 1024


def _cparams(*sem):
    return pltpu.CompilerParams(dimension_semantics=sem, vmem_limit_bytes=VMEM_LIMIT)


def _dot(a, b, prec=None):
    return jnp.dot(a, b, precision=prec, preferred_element_type=F32)


def _dot_nt(a, b, prec=None):
    return lax.dot_general(a, b, (((1,), (1,)), ((), ())), precision=prec, preferred_element_type=F32)


def _dot_tn(a, b, prec=None):
    return lax.dot_general(a, b, (((0,), (0,)), ((), ())), precision=prec, preferred_element_type=F32)


def _rms(x, g):
    return x * lax.rsqrt(jnp.mean(x * x, axis=-1, keepdims=True) + EPS) * g


def _ada_kernel(s_ref, w_ref, b_ref, o_ref):
    s = s_ref[...]
    s = s * jax.nn.sigmoid(s)
    o_ref[...] = _dot(s, w_ref[...], HI) + b_ref[...]


def ada_modulation(s, w, b, tn=1024):
    m, d = s.shape
    n = w.shape[1]
    return pl.pallas_call(
        _ada_kernel,
        grid=(n // tn,),
        in_specs=[pl.BlockSpec((m, d), lambda j: (0, 0)),
                  pl.BlockSpec((d, tn), lambda j: (0, j)),
                  pl.BlockSpec((1, tn), lambda j: (0, j))],
        out_specs=pl.BlockSpec((m, tn), lambda j: (0, j)),
        out_shape=jax.ShapeDtypeStruct((m, n), F32),
        compiler_params=_cparams("arbitrary"),
        name="ada_modulation",
    )(s, w, b)


def _norm_mm_kernel(x_ref, g_ref, sc_ref, sh_ref, w_ref, o_ref, xm_ref):
    @pl.when(pl.program_id(1) == 0)
    def _():
        y = _rms(x_ref[...], g_ref[...])
        xm_ref[...] = (y * (1.0 + sc_ref[0]) + sh_ref[0]).astype(BF16)

    o_ref[...] = _dot(xm_ref[...], w_ref[...]).astype(o_ref.dtype)


def norm_mod_matmul(x, g, sc, sh, w, rows_per_batch, tm, tn, out_dtype=F32):
    m, d = x.shape
    n = w.shape[1]
    bpb = rows_per_batch // tm
    return pl.pallas_call(
        _norm_mm_kernel,
        grid=(m // tm, n // tn),
        in_specs=[pl.BlockSpec((tm, d), lambda i, j: (i, 0)),
                  pl.BlockSpec((1, d), lambda i, j: (0, 0)),
                  pl.BlockSpec((1, 1, d), lambda i, j: (i // bpb, 0, 0)),
                  pl.BlockSpec((1, 1, d), lambda i, j: (i // bpb, 0, 0)),
                  pl.BlockSpec((d, tn), lambda i, j: (0, j))],
        out_specs=pl.BlockSpec((tm, tn), lambda i, j: (i, j)),
        out_shape=jax.ShapeDtypeStruct((m, n), out_dtype),
        scratch_shapes=[pltpu.VMEM((tm, d), BF16)],
        compiler_params=_cparams("parallel", "arbitrary"),
        name="norm_mod_matmul",
    )(x, g, sc, sh, w)


def _mm_kernel(a_ref, w_ref, o_ref):
    o_ref[...] = _dot(a_ref[...], w_ref[...]).astype(o_ref.dtype)


def matmul(a, w, tm, tn, out_dtype):
    m, k = a.shape
    n = w.shape[1]
    return pl.pallas_call(
        _mm_kernel,
        grid=(m // tm, n // tn),
        in_specs=[pl.BlockSpec((tm, k), lambda i, j: (i, 0)),
                  pl.BlockSpec((k, tn), lambda i, j: (0, j))],
        out_specs=pl.BlockSpec((tm, tn), lambda i, j: (i, j)),
        out_shape=jax.ShapeDtypeStruct((m, n), out_dtype),
        compiler_params=_cparams("parallel", "arbitrary"),
        name="matmul",
    )(a, w)


def _out_proj_kernel(oa_ref, ob_ref, wa_ref, wb_ref, x_ref, g1_ref, gpost_ref, gpre_ref, sc2_ref, sh2_ref,
                     x1_ref, h_ref):
    out = _dot(oa_ref[...], wa_ref[...]) + _dot(ob_ref[...], wb_ref[...])
    x1 = x_ref[...] + g1_ref[0] * _rms(out, gpost_ref[...])
    x1_ref[...] = x1
    h_ref[...] = (_rms(x1, gpre_ref[...]) * (1.0 + sc2_ref[0]) + sh2_ref[0]).astype(BF16)


def out_proj_residual(oa, ob, wa, wb, x, g1, gpost, gpre, sc2, sh2, rows_per_batch, tm):
    m, ka = oa.shape
    kb = ob.shape[1]
    d = x.shape[1]
    bpb = rows_per_batch // tm
    row = lambda i: (i, 0)
    fixed = lambda i: (0, 0)
    per_b = lambda i: (i // bpb, 0, 0)
    return pl.pallas_call(
        _out_proj_kernel,
        grid=(m // tm,),
        in_specs=[pl.BlockSpec((tm, ka), row), pl.BlockSpec((tm, kb), row),
                  pl.BlockSpec((ka, d), fixed), pl.BlockSpec((kb, d), fixed),
                  pl.BlockSpec((tm, d), row),
                  pl.BlockSpec((1, 1, d), per_b),
                  pl.BlockSpec((1, d), fixed), pl.BlockSpec((1, d), fixed),
                  pl.BlockSpec((1, 1, d), per_b), pl.BlockSpec((1, 1, d), per_b)],
        out_specs=[pl.BlockSpec((tm, d), row), pl.BlockSpec((tm, d), row)],
        out_shape=[jax.ShapeDtypeStruct((m, d), F32), jax.ShapeDtypeStruct((m, d), BF16)],
        compiler_params=_cparams("parallel"),
        name="out_proj_residual",
    )(oa, ob, wa, wb, x, g1, gpost, gpre, sc2, sh2)


def _down_proj_kernel(t_ref, w_ref, x1_ref, g2_ref, gpost_ref, o_ref, acc_ref):
    k = pl.program_id(1)

    @pl.when(k == 0)
    def _():
        acc_ref[...] = jnp.zeros_like(acc_ref)

    acc_ref[...] += _dot(t_ref[...], w_ref[...])

    @pl.when(k == pl.num_programs(1) - 1)
    def _():
        o_ref[...] = x1_ref[...] + g2_ref[0] * _rms(acc_ref[...], gpost_ref[...])


def down_proj_residual(t, w, x1, g2, gpost, rows_per_batch, tm, tk):
    m, kk = t.shape
    d = w.shape[1]
    bpb = rows_per_batch // tm
    return pl.pallas_call(
        _down_proj_kernel,
        grid=(m // tm, kk // tk),
        in_specs=[pl.BlockSpec((tm, tk), lambda i, k: (i, k)),
                  pl.BlockSpec((tk, d), lambda i, k: (k, 0)),
                  pl.BlockSpec((tm, d), lambda i, k: (i, 0)),
                  pl.BlockSpec((1, 1, d), lambda i, k: (i // bpb, 0, 0)),
                  pl.BlockSpec((1, d), lambda i, k: (0, 0))],
        out_specs=pl.BlockSpec((tm, d), lambda i, k: (i, 0)),
        out_shape=jax.ShapeDtypeStruct((m, d), F32),
        scratch_shapes=[pltpu.VMEM((tm, d), F32)],
        compiler_params=_cparams("parallel", "arbitrary"),
        name="down_proj_residual",
    )(t, w, x1, g2, gpost)


def _convglu_kernel(top_ref, mid_ref, bot_ref, up_ref, w_ref, b_ref, o_ref, *, n_blocks):
    i = pl.program_id(1)
    tb, tc = mid_ref.shape
    mid = mid_ref[...]
    top = jnp.where(i > 0, top_ref[...], 0.0)
    bot = jnp.where(i < n_blocks - 1, bot_ref[...], 0.0)
    above = jnp.concatenate([top, mid[:tb - GRID_W]], axis=0)
    below = jnp.concatenate([mid[GRID_W:], bot], axis=0)
    w = w_ref[...]
    col = lax.broadcasted_iota(jnp.int32, (tb, tc), 0) % GRID_W

    def column_sum(dx):
        return above * w[dx:dx + 1] + mid * w[3 + dx:4 + dx] + below * w[6 + dx:7 + dx]

    left = jnp.where(col > 0, pltpu.roll(column_sum(0), 1, axis=0), 0.0)
    right = jnp.where(col < GRID_W - 1, pltpu.roll(column_sum(2), tb - 1, axis=0), 0.0)
    gt = column_sum(1) + left + right + b_ref[...]
    o_ref[...] = (gt * jax.nn.sigmoid(gt) * up_ref[...].astype(F32)).astype(o_ref.dtype)


def convglu(gate, up, w9, b, seq_len, tb, tc):
    m, c = gate.shape
    nb = seq_len // tb
    bsz = m // seq_len
    hpb = tb // GRID_W
    n_halo = seq_len // GRID_W
    main = lambda b_, i, j: (b_ * nb + i, j)
    top = lambda b_, i, j: (b_ * n_halo + jnp.maximum(i * hpb - 1, 0), j)
    bot = lambda b_, i, j: (b_ * n_halo + jnp.minimum((i + 1) * hpb, n_halo - 1), j)
    return pl.pallas_call(
        functools.partial(_convglu_kernel, n_blocks=nb),
        grid=(bsz, nb, c // tc),
        in_specs=[pl.BlockSpec((GRID_W, tc), top), pl.BlockSpec((tb, tc), main), pl.BlockSpec((GRID_W, tc), bot),
                  pl.BlockSpec((tb, tc), main),
                  pl.BlockSpec((9, tc), lambda b_, i, j: (0, j)),
                  pl.BlockSpec((1, tc), lambda b_, i, j: (0, j))],
        out_specs=pl.BlockSpec((tb, tc), main),
        out_shape=jax.ShapeDtypeStruct((m, c), BF16),
        compiler_params=_cparams("parallel", "parallel", "parallel"),
        name="convglu",
    )(gate, gate, gate, up, w9, b)


def _head_sum(x, e, et):
    return _dot(_dot(x, e, HI), et, HI)


def _rw_prep_kernel(prev_ref, cur_ref, next_ref, cw_ref, wup_ref, aup_ref, gup_ref, vec_ref, e_ref, et_ref,
                    lwf_ref, lwb_ref, kdf_ref, kdb_ref, bdf_ref, bdb_ref, kk_ref, v_ref, r_ref, bon_ref, g_ref,
                    *, n_blocks, width):
    i = pl.program_id(1)
    tl = cur_ref.shape[0]
    cur = cur_ref[...]
    rows = lax.broadcasted_iota(jnp.int32, cur.shape, 0)
    prev_row = jnp.where(i > 0, prev_ref[7:8, :], 0.0)
    next_row = jnp.where(i < n_blocks - 1, next_ref[0:1, :], 0.0)
    before = jnp.where(rows == 0, prev_row, pltpu.roll(cur, 1, axis=0))
    after = jnp.where(rows == tl - 1, next_row, pltpu.roll(cur, tl - 1, axis=0))
    cw = cw_ref[...]
    c = before * cw[0:1] + cur * cw[1:2] + after * cw[2:3]

    w_ = width
    k = c[:, :w_]
    v = c[:, w_:2 * w_]
    lo = c[:, 2 * w_:2 * w_ + 256]
    r = c[:, 2 * w_ + 256:3 * w_ + 256]
    gl = c[:, 3 * w_ + 256:3 * w_ + 512]
    vec = vec_ref[...]
    e = e_ref[...]
    et = et_ref[...]
    k_k, k_a, r_k = vec[4:5], vec[5:6], vec[6:7]

    kk = k * k_k
    kk = kk * lax.rsqrt(_head_sum(kk * kk, e, et) + 1e-12)
    kk_ref[...] = kk
    v_ref[...] = v
    r_ref[...] = r
    tlo = jnp.tanh(lo)
    kd_sum = jnp.zeros_like(k)
    for d, (lw_ref, kd_ref, bd_ref) in enumerate(((lwf_ref, kdf_ref, bdf_ref), (lwb_ref, kdb_ref, bdb_ref))):
        z = vec[d:d + 1] + _dot(tlo, wup_ref[d], HI)
        nz = -z
        softplus = jnp.maximum(nz, 0.0) + jnp.log1p(jnp.exp(-jnp.abs(nz)))
        w_log = -softplus - 0.5
        lw_ref[...] = -jnp.exp(w_log)
        a = jax.nn.sigmoid(vec[2 + d:3 + d] + _dot(lo, aup_ref[d], HI))
        kd = k * (1.0 + (a - 1.0) * k_a)
        kd_ref[...] = kd
        bd_ref[...] = kk * a
        kd_sum = kd_sum + kd
    bon_ref[...] = _head_sum(r * kd_sum * r_k, e, et) * v
    g_ref[...] = _dot(jax.nn.sigmoid(gl), gup_ref[...], HI)


def rw_prep(p, cw, wup, aup, gup, vec, e, et, seq_len, tl, width, rw_cols):
    bsz = p.shape[0]
    nb = seq_len // tl
    hb = tl // 8
    n_halo = seq_len // 8
    fixed2 = lambda b_, i: (0, 0)
    fixed3 = lambda b_, i: (0, 0, 0)
    out_spec = pl.BlockSpec((None, tl, width), lambda b_, i: (b_, i, 0))
    out = jax.ShapeDtypeStruct((bsz, seq_len, width), F32)
    return pl.pallas_call(
        functools.partial(_rw_prep_kernel, n_blocks=nb, width=width),
        grid=(bsz, nb),
        in_specs=[pl.BlockSpec((None, 8, rw_cols), lambda b_, i: (b_, jnp.maximum(i * hb - 1, 0), 0)),
                  pl.BlockSpec((None, tl, rw_cols), lambda b_, i: (b_, i, 0)),
                  pl.BlockSpec((None, 8, rw_cols), lambda b_, i: (b_, jnp.minimum((i + 1) * hb, n_halo - 1), 0)),
                  pl.BlockSpec(cw.shape, fixed2),
                  pl.BlockSpec(wup.shape, fixed3), pl.BlockSpec(aup.shape, fixed3),
                  pl.BlockSpec(gup.shape, fixed2), pl.BlockSpec(vec.shape, fixed2),
                  pl.BlockSpec(e.shape, fixed2), pl.BlockSpec(et.shape, fixed2)],
        out_specs=[out_spec] * 11,
        out_shape=[out] * 11,
        compiler_params=_cparams("parallel", "parallel"),
        name="rw_prep",
    )(p, p, p, cw, wup, aup, gup, vec, e, et)


def _unit_tri_inverse(a, eye, blk, off1, off2):
    d = a * blk
    x = eye + d
    d2 = _dot(d, d, HI)
    x = x + _dot(x, d2, HI)
    d4 = _dot(d2, d2, HI)
    x = x + _dot(x, d4, HI)
    d8 = _dot(d4, d4, HI)
    x = x + _dot(x, d8, HI)
    x = x + _dot(_dot(x, a * off1, HI), x, HI)
    x = x + _dot(_dot(x, a * off2, HI), x, HI)
    return x


def _rw_chunk(lw, kd, bd, kk, v, r, s, consts, reverse):
    tri, strict, incl, eye, blk, off1, off2, head_masks, state_mask = consts
    cum = _dot(tri, lw, HI)
    total = cum[0:1] if reverse else cum[RW_CHUNK - 1:RW_CHUNK]
    w_ex = jnp.exp(cum - lw)
    w_inv = jnp.exp(-cum)
    a_t = -kk * w_ex
    b_t = bd * w_inv
    k_t = kd * w_inv
    r_t = r * jnp.exp(cum)
    w_end = jnp.exp(total - cum)
    ys = _dot_nt(r_t, s, HI)
    rhs = _dot_nt(a_t, s, HI)
    u = jnp.zeros_like(v)
    y = ys
    for m in head_masks:
        a_m = a_t * m
        r_m = r_t * m
        vm = v * m
        a_ab = jnp.where(strict, _dot_nt(a_m, b_t, HI), 0.0)
        a_ak = jnp.where(strict, _dot_nt(a_m, k_t, HI), 0.0)
        r_rb = jnp.where(incl, _dot_nt(r_m, b_t, HI), 0.0)
        r_rk = jnp.where(incl, _dot_nt(r_m, k_t, HI), 0.0)
        inv = _unit_tri_inverse(a_ab, eye, blk, off1, off2)
        u_m = _dot(inv, (rhs + _dot(a_ak, vm, HI)) * m, HI)
        u = u + u_m
        y = y + _dot(r_rb, u_m, HI) + _dot(r_rk, vm, HI)
    s_new = s * jnp.exp(total) + state_mask * (_dot_tn(u, bd * w_end, HI) + _dot_tn(v, kd * w_end, HI))
    return y, s_new


def _rw_scan_kernel(lwf_ref, kdf_ref, bdf_ref, kkf_ref, vf_ref, rf_ref,
                    lwb_ref, kdb_ref, bdb_ref, kkb_ref, vb_ref, rb_ref, s0_ref,
                    yf_ref, yb_ref, sout_ref, s_ref):
    c = pl.program_id(2)

    @pl.when(c == 0)
    def _():
        s_ref[...] = s0_ref[...]

    n = RW_CHUNK
    ri = lax.broadcasted_iota(jnp.int32, (n, n), 0)
    ci = lax.broadcasted_iota(jnp.int32, (n, n), 1)
    eye = (ri == ci).astype(F32)
    blk = (ri // RW_SUB == ci // RW_SUB).astype(F32)
    off1 = ((ri // (2 * RW_SUB) == ci // (2 * RW_SUB)) & (ri // RW_SUB != ci // RW_SUB)).astype(F32)
    off2 = (ri // (2 * RW_SUB) != ci // (2 * RW_SUB)).astype(F32)
    lane = lax.broadcasted_iota(jnp.int32, (1, LANES), 1)
    head_masks = ((lane < RW_HEAD_DIM).astype(F32), (lane >= RW_HEAD_DIM).astype(F32))
    si = lax.broadcasted_iota(jnp.int32, (LANES, LANES), 0) // RW_HEAD_DIM
    sj = lax.broadcasted_iota(jnp.int32, (LANES, LANES), 1) // RW_HEAD_DIM
    state_mask = (si == sj).astype(F32)
    shared = (eye, blk, off1, off2, head_masks, state_mask)
    fwd = ((ri >= ci).astype(F32), ri > ci, ri >= ci) + shared
    bwd = ((ri <= ci).astype(F32), ri < ci, ri <= ci) + shared

    y, s_f = _rw_chunk(lwf_ref[...], kdf_ref[...], bdf_ref[...], kkf_ref[...], vf_ref[...], rf_ref[...],
                       s_ref[0], fwd, False)
    yf_ref[...] = y
    s_ref[0] = s_f
    y, s_b = _rw_chunk(lwb_ref[...], kdb_ref[...], bdb_ref[...], kkb_ref[...], vb_ref[...], rb_ref[...],
                       s_ref[1], bwd, True)
    yb_ref[...] = y
    s_ref[1] = s_b

    @pl.when(c == pl.num_programs(2) - 1)
    def _():
        sout_ref[...] = s_ref[...]


def rw_scan(lwf, lwb, kdf, kdb, bdf, bdb, kk, v, r, s0):
    bsz, seq_len, width = kk.shape
    nc = seq_len // RW_CHUNK
    pairs = width // LANES
    f_spec = pl.BlockSpec((None, RW_CHUNK, LANES), lambda b_, h, c: (b_, c, h))
    b_spec = pl.BlockSpec((None, RW_CHUNK, LANES), lambda b_, h, c: (b_, nc - 1 - c, h))
    s_spec = pl.BlockSpec((None, None, 2, LANES, LANES), lambda b_, h, c: (b_, h, 0, 0, 0))
    y_shape = jax.ShapeDtypeStruct((bsz, seq_len, width), F32)
    return pl.pallas_call(
        _rw_scan_kernel,
        grid=(bsz, pairs, nc),
        in_specs=[f_spec] * 6 + [b_spec] * 6 + [s_spec],
        out_specs=[f_spec, b_spec, s_spec],
        out_shape=[y_shape, y_shape, jax.ShapeDtypeStruct(s0.shape, F32)],
        scratch_shapes=[pltpu.VMEM((2, LANES, LANES), F32)],
        compiler_params=_cparams("parallel", "parallel", "arbitrary"),
        name="rw_scan",
    )(lwf, kdf, bdf, kk, v, r, lwb, kdb, bdb, kk, v, r, s0)


def _rw_finish_kernel(yf_ref, yb_ref, bon_ref, g_ref, lnw_ref, lnb_ref, e_ref, et_ref, o_ref):
    y = yf_ref[...] + yb_ref[...]
    e = e_ref[...]
    et = et_ref[...]
    mu = _head_sum(y, e, et) * (1.0 / RW_HEAD_DIM)
    yc = y - mu
    var = _head_sum(yc * yc, e, et) * (1.0 / RW_HEAD_DIM)
    yn = yc * lax.rsqrt(var + LNX_EPS) * lnw_ref[...] + lnb_ref[...]
    o_ref[...] = ((yn + bon_ref[...]) * g_ref[...]).astype(o_ref.dtype)


def rw_finish(yf, yb, bon, g, lnw, lnb, e, et, tl):
    m, width = yf.shape
    row = pl.BlockSpec((tl, width), lambda i: (i, 0))
    fixed = lambda a: pl.BlockSpec(a.shape, lambda i: (0, 0))
    return pl.pallas_call(
        _rw_finish_kernel,
        grid=(m // tl,),
        in_specs=[row, row, row, row, fixed(lnw), fixed(lnb), fixed(e), fixed(et)],
        out_specs=row,
        out_shape=jax.ShapeDtypeStruct((m, width), BF16),
        compiler_params=_cparams("parallel"),
        name="rw_finish",
    )(yf, yb, bon, g, lnw, lnb, e, et)


def _rope(x, cos, sin):
    quarter = LANES // 4
    lane = lax.broadcasted_iota(jnp.int32, x.shape, 1)
    first = (lane // quarter) % 2 == 0
    partner = jnp.where(first, pltpu.roll(x, LANES - quarter, axis=1), pltpu.roll(x, quarter, axis=1))
    return x * cos + partner * sin


def _ret_chunk(q, k, v, s, lg, reverse):
    n = RET_CHUNK
    ri = lax.broadcasted_iota(jnp.int32, (n, n), 0)
    ci = lax.broadcasted_iota(jnp.int32, (n, n), 1)
    dist = (ci - ri) if reverse else (ri - ci)
    mask = dist > 0 if reverse else dist >= 0
    decay = jnp.where(mask, jnp.exp(jnp.where(mask, dist, 0).astype(F32) * lg[:, 0:1]), 0.0)
    pos = lax.broadcasted_iota(jnp.int32, (n, LANES), 0)
    done = ((n - 1 - pos) if reverse else pos).astype(F32)
    from_start = jnp.exp((done + 1.0) * lg)
    to_end = jnp.exp((n - 1.0 - done) * lg)
    scores = _dot_nt(q, k, HI) * decay
    y = _dot(scores, v, HI) + _dot(q * from_start, s, HI)
    s_new = jnp.exp(n * lg[:, 0:1]) * s + _dot_tn(k * to_end, v, HI)
    return y, s_new


def _ret_scan_kernel(kf_ref, vf_ref, qf_ref, cf_ref, sf_ref, kb_ref, vb_ref, qb_ref, cb_ref, sb_ref,
                     lg_ref, s0_ref, yf_ref, yb_ref, sout_ref, s_ref, *, rope, scale):
    c = pl.program_id(2)

    @pl.when(c == 0)
    def _():
        s_ref[...] = s0_ref[...]

    lg = lg_ref[...]
    for d, (k_ref, v_ref, q_ref, cos_ref, sin_ref, y_ref) in enumerate(
            ((kf_ref, vf_ref, qf_ref, cf_ref, sf_ref, yf_ref), (kb_ref, vb_ref, qb_ref, cb_ref, sb_ref, yb_ref))):
        q = q_ref[...]
        k = k_ref[...]
        if rope:
            q = _rope(q, cos_ref[...], sin_ref[...])
            k = _rope(k, cos_ref[...], sin_ref[...])
        y, s_new = _ret_chunk(q, k * scale, v_ref[...], s_ref[d], lg, d == 1)
        y_ref[...] = y
        s_ref[d] = s_new

    @pl.when(c == pl.num_programs(2) - 1)
    def _():
        sout_ref[...] = s_ref[...]


def ret_scan(p, col0, cos, sin, lg, s0, seq_len, width, rope):
    bsz = p.shape[0]
    nc = seq_len // RET_CHUNK
    hb = col0 // LANES
    wb = width // LANES

    def specs(chunk):
        cols = [lambda b_, h, c, o=o: (b_, chunk(c), hb + o * wb + h) for o in range(3)]
        tab = lambda b_, h, c: (chunk(c), 0)
        return ([pl.BlockSpec((None, RET_CHUNK, LANES), f) for f in cols]
                + [pl.BlockSpec((RET_CHUNK, LANES), tab)] * 2)

    fwd = lambda c: c
    bwd = lambda c: nc - 1 - c
    s_spec = pl.BlockSpec((None, None, 2, LANES, LANES), lambda b_, h, c: (b_, h, 0, 0, 0))
    y_shape = jax.ShapeDtypeStruct((bsz, seq_len, width), F32)
    return pl.pallas_call(
        functools.partial(_ret_scan_kernel, rope=rope, scale=float(LANES) ** -0.5),
        grid=(bsz, wb, nc),
        in_specs=specs(fwd) + specs(bwd) + [pl.BlockSpec((None, 1, LANES), lambda b_, h, c: (h, 0, 0)), s_spec],
        out_specs=[pl.BlockSpec((None, RET_CHUNK, LANES), lambda b_, h, c: (b_, c, h)),
                   pl.BlockSpec((None, RET_CHUNK, LANES), lambda b_, h, c: (b_, nc - 1 - c, h)),
                   s_spec],
        out_shape=[y_shape, y_shape, jax.ShapeDtypeStruct(s0.shape, F32)],
        scratch_shapes=[pltpu.VMEM((2, LANES, LANES), F32)],
        compiler_params=_cparams("parallel", "parallel", "arbitrary"),
        name="ret_scan",
    )(p, p, p, cos, sin, p, p, p, cos, sin, lg, s0)


def _ret_finish_kernel(yf_ref, yb_ref, g_ref, o_ref):
    y = yf_ref[...] + yb_ref[...]
    y = y * lax.rsqrt(jnp.mean(y * y, axis=-1, keepdims=True) + EPS)
    g = g_ref[...]
    o_ref[...] = (y * (g * jax.nn.sigmoid(g))).astype(o_ref.dtype)


def ret_finish(yf, yb, p2, gate_col0, tl):
    m, width = yf.shape
    gb = gate_col0 // LANES
    blk = lambda i, h: (i, h)
    return pl.pallas_call(
        _ret_finish_kernel,
        grid=(m // tl, width // LANES),
        in_specs=[pl.BlockSpec((tl, LANES), blk), pl.BlockSpec((tl, LANES), blk),
                  pl.BlockSpec((tl, LANES), lambda i, h: (i, gb + h))],
        out_specs=pl.BlockSpec((tl, LANES), blk),
        out_shape=jax.ShapeDtypeStruct((m, width), BF16),
        compiler_params=_cparams("parallel", "parallel"),
        name="ret_finish",
    )(yf, yb, p2)


def _rope_tables(seq_len):
    n = LANES // 4
    t = jnp.arange(seq_len)
    inv = ROPE_BASE ** (-jnp.arange(n, dtype=F32) / n)
    ang_row = (t // GRID_W).astype(F32)[:, None] * inv
    ang_col = (t % GRID_W).astype(F32)[:, None] * inv
    cr, sr, cc, sc = jnp.cos(ang_row), jnp.sin(ang_row), jnp.cos(ang_col), jnp.sin(ang_col)
    return jnp.concatenate([cr, cr, cc, cc], axis=-1), jnp.concatenate([-sr, sr, -sc, sc], axis=-1)


def _pad_rows(w, rows, at):
    return jnp.zeros((rows, w.shape[1]), w.dtype).at[at:at + w.shape[0]].set(w)


def kernel(x, c, ctx, c_ctx, w_ada, b_ada, norm_pre_mix, norm_post_mix, norm_pre_ffn, norm_post_ffn, w_in, rw_conv, rw_w0, rw_w_up, rw_a0, rw_a_up, rw_g_up, rw_k_k, rw_k_a, rw_r_k, rw_lnx_w, rw_lnx_b, w_out, ffn_w_gate, ffn_w_up, ffn_conv, ffn_conv_b, ffn_w_down):
    bsz, seq_len, d = x.shape
    ctx_len = ctx.shape[1]
    n_layers = w_ada.shape[0]
    assert n_layers == 1, "context-stream outputs are only needed between layers"
    rw_w = rw_k_k.shape[1]
    ret_w = w_out.shape[1] - rw_w
    rw_state_cols = 2 * rw_w + 2 * DECAY_LORA + 2 * AAA_LORA
    rw_cols = rw_state_cols + rw_w + GATE_LORA
    rw_pad = -rw_cols % LANES
    rw_cols_p = rw_cols + rw_pad
    d_ff = ffn_w_gate.shape[2]
    l = 0

    cond = jnp.zeros((8, d), F32).at[:bsz].set(c).at[bsz].set(c_ctx)
    mod = ada_modulation(cond, w_ada[l], b_ada[l][None])
    sh1, sc1, g1, sh2, sc2, g2 = [m[:bsz, None, :] for m in jnp.split(mod, 6, axis=-1)]
    sh_c = jnp.broadcast_to(mod[bsz, :d], (bsz, 1, d))
    sc_c = jnp.broadcast_to(mod[bsz, d:2 * d], (bsz, 1, d))

    w_in_p = jnp.concatenate([w_in[l][:, :rw_cols], jnp.zeros((d, rw_pad), F32), w_in[l][:, rw_cols:]],
                             axis=1).astype(BF16)
    p_cols = w_in_p.shape[1]
    g_pre = norm_pre_mix[l][None]
    tn_in = p_cols // 6
    p = norm_mod_matmul(x.reshape(bsz * seq_len, d), g_pre, sc1, sh1, w_in_p, seq_len, 512, tn_in)
    p = p.reshape(bsz, seq_len, p_cols)
    pc = norm_mod_matmul(ctx.reshape(bsz * ctx_len, d), g_pre, sc_c, sh_c, w_in_p, ctx_len, ctx_len, tn_in)
    pc = pc.reshape(bsz, ctx_len, p_cols)

    cw = jnp.pad(rw_conv[l], ((0, 0), (0, rw_pad)))
    lora_rows = 2 * DECAY_LORA + 2 * AAA_LORA
    wup = jnp.stack([_pad_rows(rw_w_up[l][dd], lora_rows, dd * DECAY_LORA) for dd in range(2)])
    aup = jnp.stack([_pad_rows(rw_a_up[l][dd], lora_rows, 2 * DECAY_LORA + dd * AAA_LORA) for dd in range(2)])
    gup = _pad_rows(rw_g_up[l], GATE_LORA + rw_pad, 0)
    vec = jnp.concatenate([rw_w0[l], rw_a0[l], rw_k_k[l][None], rw_k_a[l][None], rw_r_k[l].reshape(1, rw_w)], axis=0)
    vec = jnp.pad(vec, ((0, 1), (0, 0)))
    head_of_lane = jnp.arange(rw_w) // RW_HEAD_DIM
    e = (head_of_lane[:, None] == jnp.arange(LANES)[None, :]).astype(F32)
    et = e.T
    prep = functools.partial(rw_prep, cw=cw, wup=wup, aup=aup, gup=gup, vec=vec, e=e, et=et,
                             width=rw_w, rw_cols=rw_cols_p)
    s0 = jnp.zeros((bsz, rw_w // LANES, 2, LANES, LANES), F32)
    (lwf, lwb, kdf, kdb, bdf, bdb, kk, v, r, _, _) = prep(pc, seq_len=ctx_len, tl=min(256, ctx_len))
    _, _, s_rw = rw_scan(lwf, lwb, kdf, kdb, bdf, bdb, kk, v, r, s0)
    (lwf, lwb, kdf, kdb, bdf, bdb, kk, v, r, bon, gate) = prep(p, seq_len=seq_len, tl=256)
    yf, yb, _ = rw_scan(lwf, lwb, kdf, kdb, bdf, bdb, kk, v, r, s_rw)
    m = bsz * seq_len
    flat = lambda a: a.reshape(m, a.shape[-1])
    o_rw = rw_finish(flat(yf), flat(yb), flat(bon), flat(gate), rw_lnx_w[l][None], rw_lnx_b[l][None], e, et, 512)

    n_ret_heads = ret_w // LANES
    lg = jnp.log(1.0 - 2.0 ** (-5.0 - jnp.arange(n_ret_heads, dtype=F32)))
    lg = jnp.broadcast_to(lg[:, None, None], (n_ret_heads, 1, LANES))
    cos, sin = _rope_tables(seq_len)
    s0 = jnp.zeros((bsz, n_ret_heads, 2, LANES, LANES), F32)
    _, _, s_ret = ret_scan(pc, rw_cols_p, cos[:ctx_len], sin[:ctx_len], lg, s0, ctx_len, ret_w, rope=False)
    yf, yb, _ = ret_scan(p, rw_cols_p, cos, sin, lg, s_ret, seq_len, ret_w, rope=True)
    o_ret = ret_finish(flat(yf), flat(yb), flat(p), rw_cols_p + 3 * ret_w, 512)

    w_o = w_out[l].astype(BF16)
    x1, h = out_proj_residual(o_rw, o_ret, w_o[:rw_w], w_o[rw_w:], flat(x), g1, norm_post_mix[l][None],
                              norm_pre_ffn[l][None], sc2, sh2, seq_len, 512)
    gate_ffn = matmul(h, ffn_w_gate[l].astype(BF16), 1024, d_ff // 4, F32)
    up_ffn = matmul(h, ffn_w_up[l].astype(BF16), 1024, d_ff // 4, BF16)
    t = convglu(gate_ffn, up_ffn, ffn_conv[l].reshape(9, d_ff), ffn_conv_b[l][None], seq_len, 512, d_ff // 11)
    out = down_proj_residual(t, ffn_w_down[l].astype(BF16), x1, g2, norm_post_ffn[l][None], seq_len, 512, d_ff // 4)
    return out.reshape(bsz, seq_len, d)
```

```python
import functools

import jax
import jax.numpy as jnp
import numpy as np
from jax import lax
from jax.experimental import pallas as pl
from jax.experimental.pallas import tpu as pltpu

F32 = jnp.float32
BF16 = jnp.bfloat16
HI = lax.Precision.HIGHEST

LANES = 128
EPS = 1e-6
GRID_W = 64
RW_HEAD_DIM = 64
DECAY_LORA = 64
AAA_LORA = 64
GATE_LORA = 160
LNX_EPS = 64e-5
RET_HEADS = 8
RET_CHUNK = 128
ROPE_BASE = 10000.0
RW_CHUNK = 64
RW_SUB = 16
VMEM_LIMIT = 56 * 1024 * 1024


def _cparams(*sem):
    return pltpu.CompilerParams(dimension_semantics=sem, vmem_limit_bytes=VMEM_LIMIT)


def _dot(a, b, prec=None):
    return jnp.dot(a, b, precision=prec, preferred_element_type=F32)


def _dot_nt(a, b, prec=None):
    return lax.dot_general(a, b, (((1,), (1,)), ((), ())), precision=prec, preferred_element_type=F32)


def _dot_tn(a, b, prec=None):
    return lax.dot_general(a, b, (((0,), (0,)), ((), ())), precision=prec, preferred_element_type=F32)


def _split_bf16(a):
    hi = a.astype(BF16)
    return hi, (a - hi.astype(F32)).astype(BF16)


def _mm(dot, a, b, mode):
    if mode == "f32":
        return dot(a, b, HI)
    if mode == "x1":
        return dot(a.astype(BF16), b.astype(BF16))
    ah, al = _split_bf16(a)
    bh, bl = _split_bf16(b)
    return dot(ah, bh) + (dot(ah, bl) + dot(al, bh))


def _rms(x, g):
    return x * lax.rsqrt(jnp.mean(x * x, axis=-1, keepdims=True) + EPS) * g


def _ada_kernel(s_ref, w_ref, b_ref, o_ref):
    s = s_ref[...]
    s = s * jax.nn.sigmoid(s)
    o_ref[...] = _dot(s, w_ref[...], HI) + b_ref[...]


def ada_modulation(s, w, b, tn=1024):
    m, d = s.shape
    n = w.shape[1]
    return pl.pallas_call(
        _ada_kernel,
        grid=(n // tn,),
        in_specs=[pl.BlockSpec((m, d), lambda j: (0, 0)),
                  pl.BlockSpec((d, tn), lambda j: (0, j)),
                  pl.BlockSpec((1, tn), lambda j: (0, j))],
        out_specs=pl.BlockSpec((m, tn), lambda j: (0, j)),
        out_shape=jax.ShapeDtypeStruct((m, n), F32),
        compiler_params=_cparams("arbitrary"),
        name="ada_modulation",
    )(s, w, b)


def _norm_mm_kernel(x_ref, g_ref, sc_ref, sh_ref, w_ref, o_ref, xm_ref):
    @pl.when(pl.program_id(1) == 0)
    def _():
        y = _rms(x_ref[...], g_ref[...])
        xm_ref[...] = (y * (1.0 + sc_ref[0]) + sh_ref[0]).astype(BF16)

    o_ref[...] = _dot(xm_ref[...], w_ref[...]).astype(o_ref.dtype)


def norm_mod_matmul(x, g, sc, sh, w, rows_per_batch, tm, tn, out_dtype=F32):
    m, d = x.shape
    n = w.shape[1]
    bpb = rows_per_batch // tm
    return pl.pallas_call(
        _norm_mm_kernel,
        grid=(m // tm, n // tn),
        in_specs=[pl.BlockSpec((tm, d), lambda i, j: (i, 0)),
                  pl.BlockSpec((1, d), lambda i, j: (0, 0)),
                  pl.BlockSpec((1, 1, d), lambda i, j: (i // bpb, 0, 0)),
                  pl.BlockSpec((1, 1, d), lambda i, j: (i // bpb, 0, 0)),
                  pl.BlockSpec((d, tn), lambda i, j: (0, j))],
        out_specs=pl.BlockSpec((tm, tn), lambda i, j: (i, j)),
        out_shape=jax.ShapeDtypeStruct((m, n), out_dtype),
        scratch_shapes=[pltpu.VMEM((tm, d), BF16)],
        compiler_params=_cparams("parallel", "arbitrary"),
        name="norm_mod_matmul",
    )(x, g, sc, sh, w)


def _mm_kernel(a_ref, w_ref, o_ref):
    o_ref[...] = _dot(a_ref[...], w_ref[...]).astype(o_ref.dtype)


def matmul(a, w, tm, tn, out_dtype):
    m, k = a.shape
    n = w.shape[1]
    return pl.pallas_call(
        _mm_kernel,
        grid=(m // tm, n // tn),
        in_specs=[pl.BlockSpec((tm, k), lambda i, j: (i, 0)),
                  pl.BlockSpec((k, tn), lambda i, j: (0, j))],
        out_specs=pl.BlockSpec((tm, tn), lambda i, j: (i, j)),
        out_shape=jax.ShapeDtypeStruct((m, n), out_dtype),
        compiler_params=_cparams("parallel", "arbitrary"),
        name="matmul",
    )(a, w)


def _out_proj_kernel(oa_ref, ob_ref, wa_ref, wb_ref, x_ref, g1_ref, gpost_ref, gpre_ref, sc2_ref, sh2_ref,
                     x1_ref, h_ref):
    out = _dot(oa_ref[...], wa_ref[...]) + _dot(ob_ref[...], wb_ref[...])
    x1 = x_ref[...] + g1_ref[0] * _rms(out, gpost_ref[...])
    x1_ref[...] = x1
    h_ref[...] = (_rms(x1, gpre_ref[...]) * (1.0 + sc2_ref[0]) + sh2_ref[0]).astype(BF16)


def out_proj_residual(oa, ob, wa, wb, x, g1, gpost, gpre, sc2, sh2, rows_per_batch, tm):
    m, ka = oa.shape
    kb = ob.shape[1]
    d = x.shape[1]
    bpb = rows_per_batch // tm
    row = lambda i: (i, 0)
    fixed = lambda i: (0, 0)
    per_b = lambda i: (i // bpb, 0, 0)
    return pl.pallas_call(
        _out_proj_kernel,
        grid=(m // tm,),
        in_specs=[pl.BlockSpec((tm, ka), row), pl.BlockSpec((tm, kb), row),
                  pl.BlockSpec((ka, d), fixed), pl.BlockSpec((kb, d), fixed),
                  pl.BlockSpec((tm, d), row),
                  pl.BlockSpec((1, 1, d), per_b),
                  pl.BlockSpec((1, d), fixed), pl.BlockSpec((1, d), fixed),
                  pl.BlockSpec((1, 1, d), per_b), pl.BlockSpec((1, 1, d), per_b)],
        out_specs=[pl.BlockSpec((tm, d), row), pl.BlockSpec((tm, d), row)],
        out_shape=[jax.ShapeDtypeStruct((m, d), F32), jax.ShapeDtypeStruct((m, d), BF16)],
        compiler_params=_cparams("parallel"),
        name="out_proj_residual",
    )(oa, ob, wa, wb, x, g1, gpost, gpre, sc2, sh2)


def _down_proj_kernel(t_ref, w_ref, x1_ref, g2_ref, gpost_ref, o_ref, acc_ref):
    k = pl.program_id(1)

    @pl.when(k == 0)
    def _():
        acc_ref[...] = jnp.zeros_like(acc_ref)

    acc_ref[...] += _dot(t_ref[...], w_ref[...])

    @pl.when(k == pl.num_programs(1) - 1)
    def _():
        o_ref[...] = x1_ref[...] + g2_ref[0] * _rms(acc_ref[...], gpost_ref[...])


def down_proj_residual(t, w, x1, g2, gpost, rows_per_batch, tm, tk):
    m, kk = t.shape
    d = w.shape[1]
    bpb = rows_per_batch // tm
    return pl.pallas_call(
        _down_proj_kernel,
        grid=(m // tm, kk // tk),
        in_specs=[pl.BlockSpec((tm, tk), lambda i, k: (i, k)),
                  pl.BlockSpec((tk, d), lambda i, k: (k, 0)),
                  pl.BlockSpec((tm, d), lambda i, k: (i, 0)),
                  pl.BlockSpec((1, 1, d), lambda i, k: (i // bpb, 0, 0)),
                  pl.BlockSpec((1, d), lambda i, k: (0, 0))],
        out_specs=pl.BlockSpec((tm, d), lambda i, k: (i, 0)),
        out_shape=jax.ShapeDtypeStruct((m, d), F32),
        scratch_shapes=[pltpu.VMEM((tm, d), F32)],
        compiler_params=_cparams("parallel", "arbitrary"),
        name="down_proj_residual",
    )(t, w, x1, g2, gpost)


def _convglu_kernel(top_ref, mid_ref, bot_ref, up_ref, w_ref, b_ref, o_ref, *, n_blocks):
    i = pl.program_id(1)
    tb, tc = mid_ref.shape
    mid = mid_ref[...]
    top = jnp.where(i > 0, top_ref[...], 0.0)
    bot = jnp.where(i < n_blocks - 1, bot_ref[...], 0.0)
    above = jnp.concatenate([top, mid[:tb - GRID_W]], axis=0)
    below = jnp.concatenate([mid[GRID_W:], bot], axis=0)
    w = w_ref[...]
    col = lax.broadcasted_iota(jnp.int32, (tb, tc), 0) % GRID_W

    def column_sum(dx):
        return above * w[dx:dx + 1] + mid * w[3 + dx:4 + dx] + below * w[6 + dx:7 + dx]

    left = jnp.where(col > 0, pltpu.roll(column_sum(0), 1, axis=0), 0.0)
    right = jnp.where(col < GRID_W - 1, pltpu.roll(column_sum(2), tb - 1, axis=0), 0.0)
    gt = column_sum(1) + left + right + b_ref[...]
    o_ref[...] = (gt * jax.nn.sigmoid(gt) * up_ref[...].astype(F32)).astype(o_ref.dtype)


def convglu(gate, up, w9, b, seq_len, tb, tc):
    m, c = gate.shape
    nb = seq_len // tb
    bsz = m // seq_len
    hpb = tb // GRID_W
    n_halo = seq_len // GRID_W
    main = lambda b_, i, j: (b_ * nb + i, j)
    top = lambda b_, i, j: (b_ * n_halo + jnp.maximum(i * hpb - 1, 0), j)
    bot = lambda b_, i, j: (b_ * n_halo + jnp.minimum((i + 1) * hpb, n_halo - 1), j)
    return pl.pallas_call(
        functools.partial(_convglu_kernel, n_blocks=nb),
        grid=(bsz, nb, c // tc),
        in_specs=[pl.BlockSpec((GRID_W, tc), top), pl.BlockSpec((tb, tc), main), pl.BlockSpec((GRID_W, tc), bot),
                  pl.BlockSpec((tb, tc), main),
                  pl.BlockSpec((9, tc), lambda b_, i, j: (0, j)),
                  pl.BlockSpec((1, tc), lambda b_, i, j: (0, j))],
        out_specs=pl.BlockSpec((tb, tc), main),
        out_shape=jax.ShapeDtypeStruct((m, c), BF16),
        compiler_params=_cparams("parallel", "parallel", "parallel"),
        name="convglu",
    )(gate, gate, gate, up, w9, b)


def _head_sum(x, e, et):
    return _dot(_dot(x, e, HI), et, HI)


def _rw_prep_kernel(prev_ref, cur_ref, next_ref, cw_ref, wup_ref, aup_ref, gup_ref, vec_ref, e_ref, et_ref,
                    af_ref, btf_ref, ktf_ref, rf_ref, bhf_ref, khf_ref, wtf_ref,
                    ab_ref, btb_ref, ktb_ref, rb_ref, bhb_ref, khb_ref, wtb_ref,
                    v_ref, bon_ref, g_ref, *, n_blocks, width):
    i = pl.program_id(1)
    tl = cur_ref.shape[0]
    cur = cur_ref[...]
    rows = lax.broadcasted_iota(jnp.int32, cur.shape, 0)
    prev_row = jnp.where(i > 0, prev_ref[7:8, :], 0.0)
    next_row = jnp.where(i < n_blocks - 1, next_ref[0:1, :], 0.0)
    before = jnp.where(rows == 0, prev_row, pltpu.roll(cur, 1, axis=0))
    after = jnp.where(rows == tl - 1, next_row, pltpu.roll(cur, tl - 1, axis=0))
    cw = cw_ref[...]
    c = before * cw[0:1] + cur * cw[1:2] + after * cw[2:3]

    w_ = width
    k = c[:, :w_]
    v = c[:, w_:2 * w_]
    lo = c[:, 2 * w_:2 * w_ + 256]
    r = c[:, 2 * w_ + 256:3 * w_ + 256]
    gl = c[:, 3 * w_ + 256:3 * w_ + 512]
    vec = vec_ref[...]
    e = e_ref[...]
    et = et_ref[...]
    k_k, k_a, r_k = vec[4:5], vec[5:6], vec[6:7]

    kk = k * k_k
    kk = kk * lax.rsqrt(_head_sum(kk * kk, e, et) + 1e-12)
    v_ref[...] = v.astype(BF16)
    tlo = jnp.tanh(lo)
    kd_sum = jnp.zeros_like(k)
    ti = lax.broadcasted_iota(jnp.int32, (tl, tl), 0)
    tj = lax.broadcasted_iota(jnp.int32, (tl, tl), 1)
    same_chunk = ti // RW_CHUNK == tj // RW_CHUNK
    wi = lax.broadcasted_iota(jnp.int32, (tl // 8, tl), 0)
    wj = lax.broadcasted_iota(jnp.int32, (tl // 8, tl), 1)
    chunk_rows = ((wi * 8) // RW_CHUNK == wj // RW_CHUNK).astype(F32)
    all_steps = same_chunk.astype(F32)
    dirs = ((af_ref, btf_ref, ktf_ref, rf_ref, bhf_ref, khf_ref, wtf_ref, tj <= ti),
            (ab_ref, btb_ref, ktb_ref, rb_ref, bhb_ref, khb_ref, wtb_ref, tj >= ti))
    for d, (a_ref, bt_ref, kt_ref, r_ref, bh_ref, kh_ref, wt_ref, done) in enumerate(dirs):
        z = vec[d:d + 1] + _dot(tlo, wup_ref[d], HI)
        nz = -z
        softplus = jnp.maximum(nz, 0.0) + jnp.log1p(jnp.exp(-jnp.abs(nz)))
        w_log = -softplus - 0.5
        lw = -jnp.exp(w_log)
        a = jax.nn.sigmoid(vec[2 + d:3 + d] + _dot(lo, aup_ref[d], HI))
        kd = k * (1.0 + (a - 1.0) * k_a)
        bd = kk * a
        kd_sum = kd_sum + kd
        cum = _dot((same_chunk & done).astype(F32), lw, HI)
        total = _dot(all_steps, lw, HI)
        w_inv = jnp.exp(-cum)
        w_end = jnp.exp(total - cum)
        a_ref[...] = (-kk * jnp.exp(cum - lw)).astype(BF16)
        bt_ref[...] = (bd * w_inv).astype(BF16)
        kt_ref[...] = (kd * w_inv).astype(BF16)
        r_ref[...] = (r * jnp.exp(cum)).astype(BF16)
        bh_ref[...] = (bd * w_end).astype(BF16)
        kh_ref[...] = (kd * w_end).astype(BF16)
        wt_ref[...] = jnp.exp(_dot(chunk_rows, lw, HI))
    bon_ref[...] = _head_sum(r * kd_sum * r_k, e, et) * v
    g_ref[...] = _dot(jax.nn.sigmoid(gl), gup_ref[...], HI)


def rw_prep(p, cw, wup, aup, gup, vec, e, et, seq_len, tl, width, rw_cols):
    bsz = p.shape[0]
    nb = seq_len // tl
    hb = tl // 8
    n_halo = seq_len // 8
    fixed2 = lambda b_, i: (0, 0)
    fixed3 = lambda b_, i: (0, 0, 0)
    row_spec = pl.BlockSpec((None, tl, width), lambda b_, i: (b_, i, 0))
    wt_spec = pl.BlockSpec((None, tl // 8, width), lambda b_, i: (b_, i, 0))
    act = lambda dt: jax.ShapeDtypeStruct((bsz, seq_len, width), dt)
    wt = jax.ShapeDtypeStruct((bsz, seq_len // 8, width), F32)
    per_dir_specs = [row_spec] * 6 + [wt_spec]
    per_dir_shapes = [act(BF16)] * 6 + [wt]
    return pl.pallas_call(
        functools.partial(_rw_prep_kernel, n_blocks=nb, width=width),
        grid=(bsz, nb),
        in_specs=[pl.BlockSpec((None, 8, rw_cols), lambda b_, i: (b_, jnp.maximum(i * hb - 1, 0), 0)),
                  pl.BlockSpec((None, tl, rw_cols), lambda b_, i: (b_, i, 0)),
                  pl.BlockSpec((None, 8, rw_cols), lambda b_, i: (b_, jnp.minimum((i + 1) * hb, n_halo - 1), 0)),
                  pl.BlockSpec(cw.shape, fixed2),
                  pl.BlockSpec(wup.shape, fixed3), pl.BlockSpec(aup.shape, fixed3),
                  pl.BlockSpec(gup.shape, fixed2), pl.BlockSpec(vec.shape, fixed2),
                  pl.BlockSpec(e.shape, fixed2), pl.BlockSpec(et.shape, fixed2)],
        out_specs=per_dir_specs * 2 + [row_spec] * 3,
        out_shape=per_dir_shapes * 2 + [act(BF16), act(F32), act(F32)],
        compiler_params=_cparams("parallel", "parallel"),
        name="rw_prep",
    )(p, p, p, cw, wup, aup, gup, vec, e, et)


RW_PROBLEMS = 4
RW_CAT = RW_PROBLEMS * RW_CHUNK
RW_GROUP = 8


def _block_diag(y, diag):
    yb = y.astype(BF16)
    tiled = jnp.concatenate([yb] * RW_PROBLEMS, axis=0)
    return jnp.where(diag, tiled, jnp.zeros_like(tiled))


def _each(f, *lists):
    return [f(*xs) for xs in zip(*lists)]


def _unit_tri_inverse(a, eye, blk, off1, off2, diag):
    mm = lambda xs, ys: _each(lambda x, y: _dot(x.astype(BF16), _block_diag(y, diag)), xs, ys)
    add = lambda xs, ys: _each(jnp.add, xs, ys)
    d = [ai * blk for ai in a]
    x = [eye + di for di in d]
    d2 = mm(d, d)
    x = add(x, mm(x, d2))
    d4 = mm(d2, d2)
    x = add(x, mm(x, d4))
    d8 = mm(d4, d4)
    x = add(x, mm(x, d8))
    x = add(x, mm(mm(x, [ai * off1 for ai in a]), x))
    x = add(x, mm(mm(x, [ai * off2 for ai in a]), x))
    return x


def _rw_chunk(a, bt, kt, r, bh, kh, v, wtot, s, masks):
    strict, incl, eye, blk, off1, off2, diag = masks
    n = RW_CHUNK
    bf = lambda xs: [x.astype(BF16) for x in xs]
    bdiag = lambda xs: [_block_diag(x, diag) for x in xs]
    bd_v = bdiag(v)
    ar = _each(lambda x, y: jnp.concatenate([x, y], axis=0), a, r)
    sc_b = _each(_dot_nt, ar, bdiag(bt))
    sc_k = _each(_dot_nt, ar, bdiag(kt))
    sc_s = _each(_dot_nt, ar, bf(s))
    a_ab = [jnp.where(strict, x[:n], 0.0) for x in sc_b]
    a_ak = [jnp.where(strict, x[:n], 0.0) for x in sc_k]
    r_rb = [jnp.where(incl, x[n:], 0.0) for x in sc_b]
    r_rk = [jnp.where(incl, x[n:], 0.0) for x in sc_k]
    inv = _unit_tri_inverse(a_ab, eye, blk, off1, off2, diag)
    rhs = _each(lambda x, y: x[:n] + y, sc_s, _each(_dot, bf(a_ak), bd_v))
    u = _each(_dot, bf(inv), bdiag(rhs))
    y = _each(lambda x, p, q: x[n:] + p + q, sc_s, _each(_dot, bf(r_rb), bdiag(u)), _each(_dot, bf(r_rk), bd_v))
    uv = _each(lambda x, z: jnp.concatenate([x, z], axis=0), bf(u), v)
    bk = _each(lambda x, z: jnp.concatenate([x, z], axis=0), bh, kh)
    upd = _each(_dot_tn, uv, bk)
    return y, _each(lambda si, w, x: si * w + jnp.where(diag, x, 0.0), s, wtot, upd)


def _rw_scan_kernel(*refs):
    fwd, bwd = refs[0:8], refs[8:16]
    s0_ref, yf_ref, yb_ref, sout_ref, s_ref = refs[16:]
    c = pl.program_id(2)

    @pl.when(c == 0)
    def _():
        s_ref[...] = s0_ref[...]

    n = RW_CHUNK
    t = lax.broadcasted_iota(jnp.int32, (n, RW_CAT), 0)
    lane = lax.broadcasted_iota(jnp.int32, (n, RW_CAT), 1)
    step = lane % n
    ahead = jnp.where(lane >= RW_CAT // 2, step - t, t - step)
    strict = ahead > 0
    incl = ahead >= 0
    eye = (t == step).astype(F32)
    blk = (t // RW_SUB == step // RW_SUB).astype(F32)
    off1 = ((t // (2 * RW_SUB) == step // (2 * RW_SUB)) & (t // RW_SUB != step // RW_SUB)).astype(F32)
    off2 = (t // (2 * RW_SUB) != step // (2 * RW_SUB)).astype(F32)
    di = lax.broadcasted_iota(jnp.int32, (RW_CAT, RW_CAT), 0) // n
    dj = lax.broadcasted_iota(jnp.int32, (RW_CAT, RW_CAT), 1) // n
    masks = (strict, incl, eye, blk, off1, off2, di == dj)

    groups = range(s_ref.shape[0])
    lanes = [slice(g * LANES, (g + 1) * LANES) for g in groups]
    ops = [[jnp.concatenate([f[:, ln], b[:, ln]], axis=1) for ln in lanes] for f, b in zip(fwd[:7], bwd[:7])]
    wtot = [jnp.concatenate([fwd[7][0:1, ln], bwd[7][0:1, ln]], axis=1) for ln in lanes]
    ys, s_new = _rw_chunk(*ops, wtot, [s_ref[g] for g in groups], masks)
    for g in groups:
        yf_ref[:, lanes[g]] = ys[g][:, :LANES]
        yb_ref[:, lanes[g]] = ys[g][:, LANES:]
        s_ref[g] = s_new[g]

    @pl.when(c == pl.num_programs(2) - 1)
    def _():
        sout_ref[...] = s_ref[...]


def rw_scan(fwd, bwd, s0):
    bsz, seq_len, width = fwd[0].shape
    nc = seq_len // RW_CHUNK
    gl = RW_GROUP * LANES
    f_spec = pl.BlockSpec((None, RW_CHUNK, gl), lambda b_, h, c: (b_, c, h))
    b_spec = pl.BlockSpec((None, RW_CHUNK, gl), lambda b_, h, c: (b_, nc - 1 - c, h))
    fw_spec = pl.BlockSpec((None, 8, gl), lambda b_, h, c: (b_, c, h))
    bw_spec = pl.BlockSpec((None, 8, gl), lambda b_, h, c: (b_, nc - 1 - c, h))
    s_spec = pl.BlockSpec((None, RW_GROUP, RW_CAT, RW_CAT), lambda b_, h, c: (b_, h, 0, 0))
    y_shape = jax.ShapeDtypeStruct((bsz, seq_len, width), F32)
    return pl.pallas_call(
        _rw_scan_kernel,
        grid=(bsz, width // gl, nc),
        in_specs=[f_spec] * 7 + [fw_spec] + [b_spec] * 7 + [bw_spec] + [s_spec],
        out_specs=[f_spec, b_spec, s_spec],
        out_shape=[y_shape, y_shape, jax.ShapeDtypeStruct(s0.shape, F32)],
        scratch_shapes=[pltpu.VMEM((RW_GROUP, RW_CAT, RW_CAT), F32)],
        compiler_params=_cparams("parallel", "parallel", "arbitrary"),
        name="rw_scan",
    )(*fwd, *bwd, s0)


def _rw_finish_kernel(yf_ref, yb_ref, bon_ref, g_ref, lnw_ref, lnb_ref, e_ref, et_ref, o_ref):
    y = yf_ref[...] + yb_ref[...]
    e = e_ref[...]
    et = et_ref[...]
    mu = _head_sum(y, e, et) * (1.0 / RW_HEAD_DIM)
    yc = y - mu
    var = _head_sum(yc * yc, e, et) * (1.0 / RW_HEAD_DIM)
    yn = yc * lax.rsqrt(var + LNX_EPS) * lnw_ref[...] + lnb_ref[...]
    o_ref[...] = ((yn + bon_ref[...]) * g_ref[...]).astype(o_ref.dtype)


def rw_finish(yf, yb, bon, g, lnw, lnb, e, et, tl):
    m, width = yf.shape
    row = pl.BlockSpec((tl, width), lambda i: (i, 0))
    fixed = lambda a: pl.BlockSpec(a.shape, lambda i: (0, 0))
    return pl.pallas_call(
        _rw_finish_kernel,
        grid=(m // tl,),
        in_specs=[row, row, row, row, fixed(lnw), fixed(lnb), fixed(e), fixed(et)],
        out_specs=row,
        out_shape=jax.ShapeDtypeStruct((m, width), BF16),
        compiler_params=_cparams("parallel"),
        name="rw_finish",
    )(yf, yb, bon, g, lnw, lnb, e, et)


def _rope(x, cos, sin):
    quarter = LANES // 4
    lane = lax.broadcasted_iota(jnp.int32, x.shape, 1)
    first = (lane // quarter) % 2 == 0
    partner = jnp.where(first, pltpu.roll(x, LANES - quarter, axis=1), pltpu.roll(x, quarter, axis=1))
    return x * cos + partner * sin


P_RET = "x1"


def _ret_chunk(q, k, v, s, lg, reverse):
    n = RET_CHUNK
    ri = lax.broadcasted_iota(jnp.int32, (n, n), 0)
    ci = lax.broadcasted_iota(jnp.int32, (n, n), 1)
    dist = (ci - ri) if reverse else (ri - ci)
    mask = dist > 0 if reverse else dist >= 0
    decay = jnp.where(mask, jnp.exp(jnp.where(mask, dist, 0).astype(F32) * lg[:, 0:1]), 0.0)
    pos = lax.broadcasted_iota(jnp.int32, (n, LANES), 0)
    done = ((n - 1 - pos) if reverse else pos).astype(F32)
    from_start = jnp.exp((done + 1.0) * lg)
    to_end = jnp.exp((n - 1.0 - done) * lg)
    scores = _mm(_dot_nt, q, k, P_RET) * decay
    y = _mm(_dot, scores, v, P_RET) + _mm(_dot, q * from_start, s, P_RET)
    s_new = jnp.exp(n * lg[:, 0:1]) * s + _mm(_dot_tn, k * to_end, v, P_RET)
    return y, s_new


def _ret_scan_kernel(kf_ref, vf_ref, qf_ref, cf_ref, sf_ref, kb_ref, vb_ref, qb_ref, cb_ref, sb_ref,
                     lg_ref, s0_ref, yf_ref, yb_ref, sout_ref, s_ref, *, rope, scale):
    c = pl.program_id(2)

    @pl.when(c == 0)
    def _():
        s_ref[...] = s0_ref[...]

    lg = lg_ref[...]
    for d, (k_ref, v_ref, q_ref, cos_ref, sin_ref, y_ref) in enumerate(
            ((kf_ref, vf_ref, qf_ref, cf_ref, sf_ref, yf_ref), (kb_ref, vb_ref, qb_ref, cb_ref, sb_ref, yb_ref))):
        q = q_ref[...]
        k = k_ref[...]
        if rope:
            q = _rope(q, cos_ref[...], sin_ref[...])
            k = _rope(k, cos_ref[...], sin_ref[...])
        y, s_new = _ret_chunk(q, k * scale, v_ref[...], s_ref[d], lg, d == 1)
        y_ref[...] = y
        s_ref[d] = s_new

    @pl.when(c == pl.num_programs(2) - 1)
    def _():
        sout_ref[...] = s_ref[...]


def ret_scan(p, col0, cos, sin, lg, s0, seq_len, width, rope):
    bsz = p.shape[0]
    nc = seq_len // RET_CHUNK
    hb = col0 // LANES
    wb = width // LANES

    def specs(chunk):
        cols = [lambda b_, h, c, o=o: (b_, chunk(c), hb + o * wb + h) for o in range(3)]
        tab = lambda b_, h, c: (chunk(c), 0)
        return ([pl.BlockSpec((None, RET_CHUNK, LANES), f) for f in cols]
                + [pl.BlockSpec((RET_CHUNK, LANES), tab)] * 2)

    fwd = lambda c: c
    bwd = lambda c: nc - 1 - c
    s_spec = pl.BlockSpec((None, None, 2, LANES, LANES), lambda b_, h, c: (b_, h, 0, 0, 0))
    y_shape = jax.ShapeDtypeStruct((bsz, seq_len, width), F32)
    return pl.pallas_call(
        functools.partial(_ret_scan_kernel, rope=rope, scale=float(LANES) ** -0.5),
        grid=(bsz, wb, nc),
        in_specs=specs(fwd) + specs(bwd) + [pl.BlockSpec((None, 1, LANES), lambda b_, h, c: (h, 0, 0)), s_spec],
        out_specs=[pl.BlockSpec((None, RET_CHUNK, LANES), lambda b_, h, c: (b_, c, h)),
                   pl.BlockSpec((None, RET_CHUNK, LANES), lambda b_, h, c: (b_, nc - 1 - c, h)),
                   s_spec],
        out_shape=[y_shape, y_shape, jax.ShapeDtypeStruct(s0.shape, F32)],
        scratch_shapes=[pltpu.VMEM((2, LANES, LANES), F32)],
        compiler_params=_cparams("parallel", "parallel", "arbitrary"),
        name="ret_scan",
    )(p, p, p, cos, sin, p, p, p, cos, sin, lg, s0)


def _ret_finish_kernel(yf_ref, yb_ref, g_ref, o_ref):
    y = yf_ref[...] + yb_ref[...]
    y = y * lax.rsqrt(jnp.mean(y * y, axis=-1, keepdims=True) + EPS)
    g = g_ref[...]
    o_ref[...] = (y * (g * jax.nn.sigmoid(g))).astype(o_ref.dtype)


def ret_finish(yf, yb, p2, gate_col0, tl):
    m, width = yf.shape
    gb = gate_col0 // LANES
    blk = lambda i, h: (i, h)
    return pl.pallas_call(
        _ret_finish_kernel,
        grid=(m // tl, width // LANES),
        in_specs=[pl.BlockSpec((tl, LANES), blk), pl.BlockSpec((tl, LANES), blk),
                  pl.BlockSpec((tl, LANES), lambda i, h: (i, gb + h))],
        out_specs=pl.BlockSpec((tl, LANES), blk),
        out_shape=jax.ShapeDtypeStruct((m, width), BF16),
        compiler_params=_cparams("parallel", "parallel"),
        name="ret_finish",
    )(yf, yb, p2)


def _rope_tables(seq_len):
    n = LANES // 4
    t = jnp.arange(seq_len)
    inv = ROPE_BASE ** (-jnp.arange(n, dtype=F32) / n)
    ang_row = (t // GRID_W).astype(F32)[:, None] * inv
    ang_col = (t % GRID_W).astype(F32)[:, None] * inv
    cr, sr, cc, sc = jnp.cos(ang_row), jnp.sin(ang_row), jnp.cos(ang_col), jnp.sin(ang_col)
    return jnp.concatenate([cr, cr, cc, cc], axis=-1), jnp.concatenate([-sr, sr, -sc, sc], axis=-1)


def _pad_rows(w, rows, at):
    return jnp.zeros((rows, w.shape[1]), w.dtype).at[at:at + w.shape[0]].set(w)


def kernel(x, c, ctx, c_ctx, w_ada, b_ada, norm_pre_mix, norm_post_mix, norm_pre_ffn, norm_post_ffn, w_in, rw_conv, rw_w0, rw_w_up, rw_a0, rw_a_up, rw_g_up, rw_k_k, rw_k_a, rw_r_k, rw_lnx_w, rw_lnx_b, w_out, ffn_w_gate, ffn_w_up, ffn_conv, ffn_conv_b, ffn_w_down):
    bsz, seq_len, d = x.shape
    ctx_len = ctx.shape[1]
    n_layers = w_ada.shape[0]
    assert n_layers == 1, "context-stream outputs are only needed between layers"
    rw_w = rw_k_k.shape[1]
    ret_w = w_out.shape[1] - rw_w
    rw_state_cols = 2 * rw_w + 2 * DECAY_LORA + 2 * AAA_LORA
    rw_cols = rw_state_cols + rw_w + GATE_LORA
    rw_pad = -rw_cols % LANES
    rw_cols_p = rw_cols + rw_pad
    d_ff = ffn_w_gate.shape[2]
    l = 0

    cond = jnp.zeros((8, d), F32).at[:bsz].set(c).at[bsz].set(c_ctx)
    mod = ada_modulation(cond, w_ada[l], b_ada[l][None])
    sh1, sc1, g1, sh2, sc2, g2 = [m[:bsz, None, :] for m in jnp.split(mod, 6, axis=-1)]
    sh_c = jnp.broadcast_to(mod[bsz, :d], (bsz, 1, d))
    sc_c = jnp.broadcast_to(mod[bsz, d:2 * d], (bsz, 1, d))

    w_in_p = jnp.concatenate([w_in[l][:, :rw_cols], jnp.zeros((d, rw_pad), F32), w_in[l][:, rw_cols:]],
                             axis=1).astype(BF16)
    p_cols = w_in_p.shape[1]
    g_pre = norm_pre_mix[l][None]
    tn_in = p_cols // 6
    p = norm_mod_matmul(x.reshape(bsz * seq_len, d), g_pre, sc1, sh1, w_in_p, seq_len, 512, tn_in)
    p = p.reshape(bsz, seq_len, p_cols)
    pc = norm_mod_matmul(ctx.reshape(bsz * ctx_len, d), g_pre, sc_c, sh_c, w_in_p, ctx_len, ctx_len, tn_in)
    pc = pc.reshape(bsz, ctx_len, p_cols)

    cw = jnp.pad(rw_conv[l], ((0, 0), (0, rw_pad)))
    lora_rows = 2 * DECAY_LORA + 2 * AAA_LORA
    wup = jnp.stack([_pad_rows(rw_w_up[l][dd], lora_rows, dd * DECAY_LORA) for dd in range(2)])
    aup = jnp.stack([_pad_rows(rw_a_up[l][dd], lora_rows, 2 * DECAY_LORA + dd * AAA_LORA) for dd in range(2)])
    gup = _pad_rows(rw_g_up[l], GATE_LORA + rw_pad, 0)
    vec = jnp.concatenate([rw_w0[l], rw_a0[l], rw_k_k[l][None], rw_k_a[l][None], rw_r_k[l].reshape(1, rw_w)], axis=0)
    vec = jnp.pad(vec, ((0, 1), (0, 0)))
    head_of_lane = jnp.arange(rw_w) // RW_HEAD_DIM
    e = (head_of_lane[:, None] == jnp.arange(LANES)[None, :]).astype(F32)
    et = e.T
    prep = functools.partial(rw_prep, cw=cw, wup=wup, aup=aup, gup=gup, vec=vec, e=e, et=et,
                             width=rw_w, rw_cols=rw_cols_p)
    s0 = jnp.zeros((bsz, rw_w // LANES, RW_CAT, RW_CAT), F32)
    outs = list(prep(pc, seq_len=ctx_len, tl=min(256, ctx_len)))
    _, _, s_rw = rw_scan(outs[0:6] + [outs[14], outs[6]], outs[7:13] + [outs[14], outs[13]], s0)
    outs = list(prep(p, seq_len=seq_len, tl=256))
    yf, yb, _ = rw_scan(outs[0:6] + [outs[14], outs[6]], outs[7:13] + [outs[14], outs[13]], s_rw)
    bon, gate = outs[15], outs[16]
    m = bsz * seq_len
    flat = lambda a: a.reshape(m, a.shape[-1])
    o_rw = rw_finish(flat(yf), flat(yb), flat(bon), flat(gate), rw_lnx_w[l][None], rw_lnx_b[l][None], e, et, 512)

    n_ret_heads = ret_w // LANES
    lg = jnp.log(1.0 - 2.0 ** (-5.0 - jnp.arange(n_ret_heads, dtype=F32)))
    lg = jnp.broadcast_to(lg[:, None, None], (n_ret_heads, 1, LANES))
    cos, sin = _rope_tables(seq_len)
    s0 = jnp.zeros((bsz, n_ret_heads, 2, LANES, LANES), F32)
    _, _, s_ret = ret_scan(pc, rw_cols_p, cos[:ctx_len], sin[:ctx_len], lg, s0, ctx_len, ret_w, rope=False)
    yf, yb, _ = ret_scan(p, rw_cols_p, cos, sin, lg, s_ret, seq_len, ret_w, rope=True)
    o_ret = ret_finish(flat(yf), flat(yb), flat(p), rw_cols_p + 3 * ret_w, 512)

    w_o = w_out[l].astype(BF16)
    x1, h = out_proj_residual(o_rw, o_ret, w_o[:rw_w], w_o[rw_w:], flat(x), g1, norm_post_mix[l][None],
                              norm_pre_ffn[l][None], sc2, sh2, seq_len, 512)
    gate_ffn = matmul(h, ffn_w_gate[l].astype(BF16), 1024, d_ff // 4, F32)
    up_ffn = matmul(h, ffn_w_up[l].astype(BF16), 1024, d_ff // 4, BF16)
    t = convglu(gate_ffn, up_ffn, ffn_conv[l].reshape(9, d_ff), ffn_conv_b[l][None], seq_len, 512, d_ff // 11)
    out = down_proj_residual(t, ffn_w_down[l].astype(BF16), x1, g2, norm_post_ffn[l][None], seq_len, 512, d_ff // 4)
    return out.reshape(bsz, seq_len, d)
```

```python
import functools

import jax
import jax.numpy as jnp
from jax import lax
from jax.experimental import pallas as pl
from jax.experimental.pallas import tpu as pltpu

F32 = jnp.float32
BF16 = jnp.bfloat16
HI = lax.Precision.HIGHEST

LANES = 128
EPS = 1e-6
GRID_W = 64
RW_HEAD_DIM = 64
DECAY_LORA = 64
AAA_LORA = 64
GATE_LORA = 160
LNX_EPS = 64e-5
RET_CHUNK = 128
ROPE_BASE = 10000.0
RW_CHUNK = 64
RW_SUB = 16
RW_PROBLEMS = 4
RW_CAT = RW_PROBLEMS * RW_CHUNK
VMEM_LIMIT = 56 * 1024 * 1024

TM_IN, TM_OUT, TM_FFN, TN_FFN, TM_DOWN = 512, 512, 1024, 512, 512
TL_PREP, TL_FINISH = 256, 512


def _cparams(*sem):
    return pltpu.CompilerParams(dimension_semantics=sem, vmem_limit_bytes=VMEM_LIMIT)


def _dot(a, b, prec=None):
    return jnp.dot(a, b, precision=prec, preferred_element_type=F32)


def _dot_nt(a, b, prec=None):
    return lax.dot_general(a, b, (((1,), (1,)), ((), ())), precision=prec, preferred_element_type=F32)


def _dot_tn(a, b, prec=None):
    return lax.dot_general(a, b, (((0,), (0,)), ((), ())), precision=prec, preferred_element_type=F32)


def _bf16_terms(a, n):
    terms = []
    for _ in range(n):
        t = a.astype(BF16)
        terms.append(t)
        a = a - t.astype(F32)
    return terms


def _dot_sel(sel, x, left=True):
    hi, lo = _bf16_terms(x, 2)
    return (_dot(sel, hi) + _dot(sel, lo)) if left else (_dot(hi, sel) + _dot(lo, sel))


def _dot_x3(a, b_hi, b_lo):
    ah, al = _bf16_terms(a, 2)
    return _dot(ah, b_hi) + (_dot(ah, b_lo) + _dot(al, b_hi))


def _rms(x, g):
    return x * lax.rsqrt(jnp.mean(x * x, axis=-1, keepdims=True) + EPS) * g


def _each(f, *lists):
    return [f(*xs) for xs in zip(*lists)]


def _ada_kernel(s_ref, w_ref, b_ref, o_ref):
    s = s_ref[...]
    s = s * jax.nn.sigmoid(s)
    o_ref[...] = _dot(s, w_ref[...], HI) + b_ref[...]


def ada_modulation(s, w, b, tn=1024):
    m, d = s.shape
    n = w.shape[1]
    return pl.pallas_call(
        _ada_kernel,
        grid=(n // tn,),
        in_specs=[pl.BlockSpec((m, d), lambda j: (0, 0)),
                  pl.BlockSpec((d, tn), lambda j: (0, j)),
                  pl.BlockSpec((1, tn), lambda j: (0, j))],
        out_specs=pl.BlockSpec((m, tn), lambda j: (0, j)),
        out_shape=jax.ShapeDtypeStruct((m, n), F32),
        compiler_params=_cparams("arbitrary"),
        name="ada_modulation",
    )(s, w, b)


def _norm_mm_kernel(x_ref, g_ref, sc_ref, sh_ref, w_ref, o_ref, xm_ref):
    @pl.when(pl.program_id(1) == 0)
    def _():
        y = _rms(x_ref[...], g_ref[...])
        xm_ref[...] = (y * (1.0 + sc_ref[0]) + sh_ref[0]).astype(BF16)

    o_ref[...] = _dot(xm_ref[...], w_ref[...]).astype(o_ref.dtype)


def norm_mod_matmul(x, g, sc, sh, w, rows_per_batch, tm, tn, out_dtype=F32):
    m, d = x.shape
    n = w.shape[1]
    bpb = rows_per_batch // tm
    return pl.pallas_call(
        _norm_mm_kernel,
        grid=(m // tm, n // tn),
        in_specs=[pl.BlockSpec((tm, d), lambda i, j: (i, 0)),
                  pl.BlockSpec((1, d), lambda i, j: (0, 0)),
                  pl.BlockSpec((1, 1, d), lambda i, j: (i // bpb, 0, 0)),
                  pl.BlockSpec((1, 1, d), lambda i, j: (i // bpb, 0, 0)),
                  pl.BlockSpec((d, tn), lambda i, j: (0, j))],
        out_specs=pl.BlockSpec((tm, tn), lambda i, j: (i, j)),
        out_shape=jax.ShapeDtypeStruct((m, n), out_dtype),
        scratch_shapes=[pltpu.VMEM((tm, d), BF16)],
        compiler_params=_cparams("parallel", "arbitrary"),
        name="norm_mod_matmul",
    )(x, g, sc, sh, w)


def _out_proj_kernel(oa_ref, ob_ref, wa_ref, wb_ref, x_ref, g1_ref, gpost_ref, gpre_ref, sc2_ref, sh2_ref,
                     x1_ref, h_ref):
    out = _dot(oa_ref[...], wa_ref[...]) + _dot(ob_ref[...], wb_ref[...])
    x1 = x_ref[...] + g1_ref[0] * _rms(out, gpost_ref[...])
    x1_ref[...] = x1
    h_ref[...] = (_rms(x1, gpre_ref[...]) * (1.0 + sc2_ref[0]) + sh2_ref[0]).astype(BF16)


def out_proj_residual(oa, ob, wa, wb, x, g1, gpost, gpre, sc2, sh2, rows_per_batch, tm):
    m, ka = oa.shape
    kb = ob.shape[1]
    d = x.shape[1]
    bpb = rows_per_batch // tm
    row = lambda i: (i, 0)
    fixed = lambda i: (0, 0)
    per_b = lambda i: (i // bpb, 0, 0)
    return pl.pallas_call(
        _out_proj_kernel,
        grid=(m // tm,),
        in_specs=[pl.BlockSpec((tm, ka), row), pl.BlockSpec((tm, kb), row),
                  pl.BlockSpec((ka, d), fixed), pl.BlockSpec((kb, d), fixed),
                  pl.BlockSpec((tm, d), row),
                  pl.BlockSpec((1, 1, d), per_b),
                  pl.BlockSpec((1, d), fixed), pl.BlockSpec((1, d), fixed),
                  pl.BlockSpec((1, 1, d), per_b), pl.BlockSpec((1, 1, d), per_b)],
        out_specs=[pl.BlockSpec((tm, d), row), pl.BlockSpec((tm, d), row)],
        out_shape=[jax.ShapeDtypeStruct((m, d), F32), jax.ShapeDtypeStruct((m, d), BF16)],
        compiler_params=_cparams("parallel"),
        name="out_proj_residual",
    )(oa, ob, wa, wb, x, g1, gpost, gpre, sc2, sh2)


def _ffn_up_kernel(top_ref, mid_ref, bot_ref, wg_ref, wu_ref, cw_ref, cb_ref, o_ref, hext_ref, *, n_blocks):
    i = pl.program_id(1)
    tm = mid_ref.shape[0]

    @pl.when(pl.program_id(2) == 0)
    def _():
        hext_ref[0:GRID_W] = top_ref[...]
        hext_ref[GRID_W:GRID_W + tm] = mid_ref[...]
        hext_ref[GRID_W + tm:] = bot_ref[...]

    gate = _dot(hext_ref[...], wg_ref[...])
    up = _dot(mid_ref[...], wu_ref[...])
    tn = gate.shape[1]
    mid = gate[GRID_W:GRID_W + tm]
    above = jnp.where(i > 0, gate[:GRID_W], 0.0)
    below = jnp.where(i < n_blocks - 1, gate[GRID_W + tm:], 0.0)
    above = jnp.concatenate([above, mid[:tm - GRID_W]], axis=0)
    below = jnp.concatenate([mid[GRID_W:], below], axis=0)
    w = cw_ref[...]
    col = lax.broadcasted_iota(jnp.int32, (tm, tn), 0) % GRID_W

    def column_sum(dx):
        return above * w[dx:dx + 1] + mid * w[3 + dx:4 + dx] + below * w[6 + dx:7 + dx]

    left = jnp.where(col > 0, pltpu.roll(column_sum(0), 1, axis=0), 0.0)
    right = jnp.where(col < GRID_W - 1, pltpu.roll(column_sum(2), tm - 1, axis=0), 0.0)
    gt = column_sum(1) + left + right + cb_ref[...]
    o_ref[...] = (gt * jax.nn.sigmoid(gt) * up).astype(o_ref.dtype)


def ffn_up(h, wg, wu, w9, b, seq_len, tm, tn):
    m, d = h.shape
    f = wg.shape[1]
    nb = seq_len // tm
    bsz = m // seq_len
    hpb = tm // GRID_W
    n_halo = seq_len // GRID_W
    main = lambda b_, i, j: (b_ * nb + i, 0)
    top = lambda b_, i, j: (b_ * n_halo + jnp.maximum(i * hpb - 1, 0), 0)
    bot = lambda b_, i, j: (b_ * n_halo + jnp.minimum((i + 1) * hpb, n_halo - 1), 0)
    col = lambda b_, i, j: (0, j)
    return pl.pallas_call(
        functools.partial(_ffn_up_kernel, n_blocks=nb),
        grid=(bsz, nb, f // tn),
        in_specs=[pl.BlockSpec((GRID_W, d), top), pl.BlockSpec((tm, d), main), pl.BlockSpec((GRID_W, d), bot),
                  pl.BlockSpec((d, tn), col), pl.BlockSpec((d, tn), col),
                  pl.BlockSpec((9, tn), col), pl.BlockSpec((1, tn), col)],
        out_specs=pl.BlockSpec((tm, tn), lambda b_, i, j: (b_ * nb + i, j)),
        out_shape=jax.ShapeDtypeStruct((m, f), BF16),
        scratch_shapes=[pltpu.VMEM((tm + 2 * GRID_W, d), BF16)],
        compiler_params=_cparams("parallel", "parallel", "arbitrary"),
        name="ffn_up",
    )(h, h, h, wg, wu, w9, b)


def _down_proj_kernel(t_ref, w_ref, x1_ref, g2_ref, gpost_ref, o_ref, acc_ref):
    k = pl.program_id(1)

    @pl.when(k == 0)
    def _():
        acc_ref[...] = jnp.zeros_like(acc_ref)

    acc_ref[...] += _dot(t_ref[...], w_ref[...])

    @pl.when(k == pl.num_programs(1) - 1)
    def _():
        o_ref[...] = x1_ref[...] + g2_ref[0] * _rms(acc_ref[...], gpost_ref[...])


def down_proj_residual(t, w, x1, g2, gpost, rows_per_batch, tm, tk):
    m, kk = t.shape
    d = w.shape[1]
    bpb = rows_per_batch // tm
    return pl.pallas_call(
        _down_proj_kernel,
        grid=(m // tm, kk // tk),
        in_specs=[pl.BlockSpec((tm, tk), lambda i, k: (i, k)),
                  pl.BlockSpec((tk, d), lambda i, k: (k, 0)),
                  pl.BlockSpec((tm, d), lambda i, k: (i, 0)),
                  pl.BlockSpec((1, 1, d), lambda i, k: (i // bpb, 0, 0)),
                  pl.BlockSpec((1, d), lambda i, k: (0, 0))],
        out_specs=pl.BlockSpec((tm, d), lambda i, k: (i, 0)),
        out_shape=jax.ShapeDtypeStruct((m, d), F32),
        scratch_shapes=[pltpu.VMEM((tm, d), F32)],
        compiler_params=_cparams("parallel", "arbitrary"),
        name="down_proj_residual",
    )(t, w, x1, g2, gpost)


def _head_sum(x, e, et):
    return _dot_sel(et, _dot_sel(e, x, left=False), left=False)


def _rw_prep_kernel(prev_ref, cur_ref, next_ref, cw_ref, wup_ref, aup_ref, gup_ref, vec_ref, e_ref, et_ref,
                    af_ref, btf_ref, ktf_ref, rf_ref, wtf_ref, ab_ref, btb_ref, ktb_ref, rb_ref, wtb_ref,
                    v_ref, bon_ref, g_ref, *, n_blocks, width):
    i = pl.program_id(1)
    tl = cur_ref.shape[0]
    cur = cur_ref[...]
    rows = lax.broadcasted_iota(jnp.int32, cur.shape, 0)
    prev_row = jnp.where(i > 0, prev_ref[7:8, :], 0.0)
    next_row = jnp.where(i < n_blocks - 1, next_ref[0:1, :], 0.0)
    before = jnp.where(rows == 0, prev_row, pltpu.roll(cur, 1, axis=0))
    after = jnp.where(rows == tl - 1, next_row, pltpu.roll(cur, tl - 1, axis=0))
    cw = cw_ref[...]
    c = before * cw[0:1] + cur * cw[1:2] + after * cw[2:3]

    w_ = width
    k = c[:, :w_]
    v = c[:, w_:2 * w_]
    lo = c[:, 2 * w_:2 * w_ + 256]
    r = c[:, 2 * w_ + 256:3 * w_ + 256]
    gl = c[:, 3 * w_ + 256:3 * w_ + 512]
    vec = vec_ref[...]
    e = e_ref[...]
    et = et_ref[...]
    k_k, k_a, r_k = vec[4:5], vec[5:6], vec[6:7]

    kk = k * k_k
    kk = kk * lax.rsqrt(_head_sum(kk * kk, e, et) + 1e-12)
    v_ref[...] = v.astype(BF16)
    tlo = jnp.tanh(lo)
    kd_sum = jnp.zeros_like(k)
    ti = lax.broadcasted_iota(jnp.int32, (tl, tl), 0)
    tj = lax.broadcasted_iota(jnp.int32, (tl, tl), 1)
    same_chunk = ti // RW_CHUNK == tj // RW_CHUNK
    wi = lax.broadcasted_iota(jnp.int32, (tl // 8, tl), 0)
    wj = lax.broadcasted_iota(jnp.int32, (tl // 8, tl), 1)
    chunk_rows = (wi * 8) // RW_CHUNK == wj // RW_CHUNK
    dirs = ((af_ref, btf_ref, ktf_ref, rf_ref, wtf_ref, tj <= ti), (ab_ref, btb_ref, ktb_ref, rb_ref, wtb_ref, tj >= ti))
    for d, (a_ref, bt_ref, kt_ref, r_ref, wt_ref, done) in enumerate(dirs):
        z = vec[d:d + 1] + _dot_x3(tlo, wup_ref[0, d], wup_ref[1, d])
        nz = -z
        softplus = jnp.maximum(nz, 0.0) + jnp.log1p(jnp.exp(-jnp.abs(nz)))
        w_log = -softplus - 0.5
        lw = -jnp.exp(w_log)
        a = jax.nn.sigmoid(vec[2 + d:3 + d] + _dot_x3(lo, aup_ref[0, d], aup_ref[1, d]))
        kd = k * (1.0 + (a - 1.0) * k_a)
        kd_sum = kd_sum + kd
        sel = jnp.concatenate([(same_chunk & done).astype(F32), chunk_rows.astype(F32)], axis=0).astype(BF16)
        sums = _dot_sel(sel, lw)
        cum = sums[:tl]
        w_inv = jnp.exp(-cum)
        a_ref[...] = (-kk * jnp.exp(cum - lw)).astype(BF16)
        bt_ref[...] = (kk * a * w_inv).astype(BF16)
        kt_ref[...] = (kd * w_inv).astype(BF16)
        r_ref[...] = (r * jnp.exp(cum)).astype(BF16)
        wt_ref[...] = jnp.exp(sums[tl:])
    bon_ref[...] = _head_sum(r * kd_sum * r_k, e, et) * v
    g_ref[...] = _dot_x3(jax.nn.sigmoid(gl), gup_ref[0], gup_ref[1])


def rw_prep(p, cw, wup, aup, gup, vec, e, et, seq_len, tl, width):
    bsz, _, rw_cols = p.shape
    nb = seq_len // tl
    hb = tl // 8
    n_halo = seq_len // 8
    fixed = lambda a: pl.BlockSpec(a.shape, lambda b_, i: (0,) * a.ndim)
    row_spec = pl.BlockSpec((None, tl, width), lambda b_, i: (b_, i, 0))
    wt_spec = pl.BlockSpec((None, tl // 8, width), lambda b_, i: (b_, i, 0))
    act = lambda dt: jax.ShapeDtypeStruct((bsz, seq_len, width), dt)
    wt = jax.ShapeDtypeStruct((bsz, seq_len // 8, width), F32)
    per_dir_specs = [row_spec] * 4 + [wt_spec]
    per_dir_shapes = [act(BF16)] * 4 + [wt]
    return pl.pallas_call(
        functools.partial(_rw_prep_kernel, n_blocks=nb, width=width),
        grid=(bsz, nb),
        in_specs=[pl.BlockSpec((None, 8, rw_cols), lambda b_, i: (b_, jnp.maximum(i * hb - 1, 0), 0)),
                  pl.BlockSpec((None, tl, rw_cols), lambda b_, i: (b_, i, 0)),
                  pl.BlockSpec((None, 8, rw_cols), lambda b_, i: (b_, jnp.minimum((i + 1) * hb, n_halo - 1), 0)),
                  fixed(cw), fixed(wup), fixed(aup), fixed(gup), fixed(vec), fixed(e), fixed(et)],
        out_specs=per_dir_specs * 2 + [row_spec] * 3,
        out_shape=per_dir_shapes * 2 + [act(BF16), act(F32), act(F32)],
        compiler_params=_cparams("parallel", "parallel"),
        name="rw_prep",
    )(p, p, p, cw, wup, aup, gup, vec, e, et)


def _block_diag(y, diag):
    yb = y.astype(BF16)
    tiled = jnp.concatenate([yb] * RW_PROBLEMS, axis=0)
    return jnp.where(diag, tiled, jnp.zeros_like(tiled))


def _unit_tri_inverse(a, eye, blk, off1, off2, diag):
    mm = lambda xs, ys: _each(lambda x, y: _dot(x.astype(BF16), _block_diag(y, diag)), xs, ys)
    add = lambda xs, ys: _each(jnp.add, xs, ys)
    d = [ai * blk for ai in a]
    x = [eye + di for di in d]
    d2 = mm(d, d)
    x = add(x, mm(x, d2))
    d4 = mm(d2, d2)
    x = add(x, mm(x, d4))
    d8 = mm(d4, d4)
    x = add(x, mm(x, d8))
    x = add(x, mm(mm(x, [ai * off1 for ai in a]), x))
    x = add(x, mm(mm(x, [ai * off2 for ai in a]), x))
    return x


def _rw_chunk(a, bt, kt, r, v, wtot, s, masks):
    strict, incl, eye, blk, off1, off2, diag = masks
    n = RW_CHUNK
    bf = lambda xs: [x.astype(BF16) for x in xs]
    bdiag = lambda xs: [_block_diag(x, diag) for x in xs]
    bd_v = bdiag(v)
    ar = _each(lambda x, y: jnp.concatenate([x, y], axis=0), a, r)
    sc_b = _each(_dot_nt, ar, bdiag(bt))
    sc_k = _each(_dot_nt, ar, bdiag(kt))
    sc_s = _each(_dot_nt, ar, bf(s))
    a_ab = [jnp.where(strict, x[:n], 0.0) for x in sc_b]
    a_ak = [jnp.where(strict, x[:n], 0.0) for x in sc_k]
    r_rb = [jnp.where(incl, x[n:], 0.0) for x in sc_b]
    r_rk = [jnp.where(incl, x[n:], 0.0) for x in sc_k]
    inv = _unit_tri_inverse(a_ab, eye, blk, off1, off2, diag)
    rhs = _each(lambda x, y: x[:n] + y, sc_s, _each(_dot, bf(a_ak), bd_v))
    u = _each(_dot, bf(inv), bdiag(rhs))
    y = _each(lambda x, p, q: x[n:] + p + q, sc_s, _each(_dot, bf(r_rb), bdiag(u)), _each(_dot, bf(r_rk), bd_v))
    uv = _each(lambda x, z: jnp.concatenate([x, z], axis=0), bf(u), v)
    to_end = lambda x, w: (x.astype(F32) * w).astype(BF16)
    bk = _each(lambda x, z, w: jnp.concatenate([to_end(x, w), to_end(z, w)], axis=0), bt, kt, wtot)
    upd = _each(_dot_tn, uv, bk)
    return y, _each(lambda si, w, x: si * w + jnp.where(diag, x, 0.0), s, wtot, upd)


def _rw_scan_kernel(*refs):
    fwd, bwd = refs[0:6], refs[6:12]
    s0_ref, yf_ref, yb_ref, sout_ref, s_ref = refs[12:]
    c = pl.program_id(1)

    @pl.when(c == 0)
    def _():
        s_ref[...] = s0_ref[...]

    n = RW_CHUNK
    t = lax.broadcasted_iota(jnp.int32, (n, RW_CAT), 0)
    lane = lax.broadcasted_iota(jnp.int32, (n, RW_CAT), 1)
    step = lane % n
    ahead = jnp.where(lane >= RW_CAT // 2, step - t, t - step)
    strict = ahead > 0
    incl = ahead >= 0
    eye = (t == step).astype(F32)
    blk = (t // RW_SUB == step // RW_SUB).astype(F32)
    off1 = ((t // (2 * RW_SUB) == step // (2 * RW_SUB)) & (t // RW_SUB != step // RW_SUB)).astype(F32)
    off2 = (t // (2 * RW_SUB) != step // (2 * RW_SUB)).astype(F32)
    di = lax.broadcasted_iota(jnp.int32, (RW_CAT, RW_CAT), 0) // n
    dj = lax.broadcasted_iota(jnp.int32, (RW_CAT, RW_CAT), 1) // n
    masks = (strict, incl, eye, blk, off1, off2, di == dj)

    groups = range(s_ref.shape[0])
    lanes = [slice(g * LANES, (g + 1) * LANES) for g in groups]
    ops = [[jnp.concatenate([f[:, ln], b[:, ln]], axis=1) for ln in lanes] for f, b in zip(fwd[:5], bwd[:5])]
    wtot = [jnp.concatenate([fwd[5][0:1, ln], bwd[5][0:1, ln]], axis=1) for ln in lanes]
    ys, s_new = _rw_chunk(*ops, wtot, [s_ref[g] for g in groups], masks)
    for g in groups:
        yf_ref[:, lanes[g]] = ys[g][:, :LANES]
        yb_ref[:, lanes[g]] = ys[g][:, LANES:]
        s_ref[g] = s_new[g]

    @pl.when(c == pl.num_programs(1) - 1)
    def _():
        sout_ref[...] = s_ref[...]


def rw_scan(fwd, bwd, s0):
    bsz, seq_len, width = fwd[0].shape
    nc = seq_len // RW_CHUNK
    pairs = width // LANES
    f_spec = pl.BlockSpec((None, RW_CHUNK, width), lambda b_, c: (b_, c, 0))
    b_spec = pl.BlockSpec((None, RW_CHUNK, width), lambda b_, c: (b_, nc - 1 - c, 0))
    fw_spec = pl.BlockSpec((None, 8, width), lambda b_, c: (b_, c, 0))
    bw_spec = pl.BlockSpec((None, 8, width), lambda b_, c: (b_, nc - 1 - c, 0))
    s_spec = pl.BlockSpec((None, pairs, RW_CAT, RW_CAT), lambda b_, c: (b_, 0, 0, 0))
    y_shape = jax.ShapeDtypeStruct((bsz, seq_len, width), F32)
    return pl.pallas_call(
        _rw_scan_kernel,
        grid=(bsz, nc),
        in_specs=[f_spec] * 5 + [fw_spec] + [b_spec] * 5 + [bw_spec] + [s_spec],
        out_specs=[f_spec, b_spec, s_spec],
        out_shape=[y_shape, y_shape, jax.ShapeDtypeStruct(s0.shape, F32)],
        scratch_shapes=[pltpu.VMEM((pairs, RW_CAT, RW_CAT), F32)],
        compiler_params=_cparams("parallel", "arbitrary"),
        name="rw_scan",
    )(*fwd, *bwd, s0)


def _rw_finish_kernel(yf_ref, yb_ref, bon_ref, g_ref, lnw_ref, lnb_ref, e_ref, et_ref, o_ref):
    y = yf_ref[...] + yb_ref[...]
    e = e_ref[...]
    et = et_ref[...]
    mu = _head_sum(y, e, et) * (1.0 / RW_HEAD_DIM)
    yc = y - mu
    var = _head_sum(yc * yc, e, et) * (1.0 / RW_HEAD_DIM)
    yn = yc * lax.rsqrt(var + LNX_EPS) * lnw_ref[...] + lnb_ref[...]
    o_ref[...] = ((yn + bon_ref[...]) * g_ref[...]).astype(o_ref.dtype)


def rw_finish(yf, yb, bon, g, lnw, lnb, e, et, tl):
    m, width = yf.shape
    row = pl.BlockSpec((tl, width), lambda i: (i, 0))
    fixed = lambda a: pl.BlockSpec(a.shape, lambda i: (0, 0))
    return pl.pallas_call(
        _rw_finish_kernel,
        grid=(m // tl,),
        in_specs=[row, row, row, row, fixed(lnw), fixed(lnb), fixed(e), fixed(et)],
        out_specs=row,
        out_shape=jax.ShapeDtypeStruct((m, width), BF16),
        compiler_params=_cparams("parallel"),
        name="rw_finish",
    )(yf, yb, bon, g, lnw, lnb, e, et)


def _rope(x, cos, sin):
    quarter = LANES // 4
    width = x.shape[1]
    lane = lax.broadcasted_iota(jnp.int32, x.shape, 1)
    first = (lane // quarter) % 2 == 0
    partner = jnp.where(first, pltpu.roll(x, width - quarter, axis=1), pltpu.roll(x, quarter, axis=1))
    return x * cos + partner * sin


def _ret_scan_kernel(kf_ref, vf_ref, qf_ref, cf_ref, sf_ref, kb_ref, vb_ref, qb_ref, cb_ref, sb_ref,
                     dec_ref, fs_ref, te_ref, gc_ref, s0_ref, yf_ref, yb_ref, sout_ref, s_ref, *, rope, scale):
    c = pl.program_id(1)

    @pl.when(c == 0)
    def _():
        s_ref[...] = s0_ref[...]

    n_heads = s_ref.shape[1]
    heads = [slice(h * LANES, (h + 1) * LANES) for h in range(n_heads)]
    qs, ks, vs, decay, ss = [], [], [], [], []
    for d, (k_ref, v_ref, q_ref, cos_ref, sin_ref) in enumerate(
            ((kf_ref, vf_ref, qf_ref, cf_ref, sf_ref), (kb_ref, vb_ref, qb_ref, cb_ref, sb_ref))):
        q = q_ref[...]
        k = k_ref[...]
        if rope:
            cos = jnp.concatenate([cos_ref[...]] * n_heads, axis=1)
            sin = jnp.concatenate([sin_ref[...]] * n_heads, axis=1)
            q = _rope(q, cos, sin)
            k = _rope(k, cos, sin)
        k = k * scale
        v = v_ref[...].astype(BF16)
        q_cross = (q * fs_ref[d]).astype(BF16)
        k_end = (k * te_ref[d]).astype(BF16)
        q = q.astype(BF16)
        k = k.astype(BF16)
        for h, hs in enumerate(heads):
            qs.append((q[:, hs], q_cross[:, hs]))
            ks.append((k[:, hs], k_end[:, hs]))
            vs.append(v[:, hs])
            decay.append(dec_ref[d, h])
            ss.append(s_ref[d, h])
    scores = _each(lambda q, k, dm: (_dot_nt(q[0], k[0]) * dm).astype(BF16), qs, ks, decay)
    cross = _each(lambda q, s: _dot(q[1], s.astype(BF16)), qs, ss)
    intra = _each(_dot, scores, vs)
    upd = _each(lambda k, v: _dot_tn(k[1], v), ks, vs)
    for d, y_ref in enumerate((yf_ref, yb_ref)):
        for h, hs in enumerate(heads):
            i = d * n_heads + h
            y_ref[:, hs] = intra[i] + cross[i]
            s_ref[d, h] = gc_ref[h] * ss[i] + upd[i]

    @pl.when(c == pl.num_programs(1) - 1)
    def _():
        sout_ref[...] = s_ref[...]


def ret_scan(p, cos, sin, tables, s0, rope):
    bsz, seq_len, cols = p.shape
    width = cols // 4
    nc = seq_len // RET_CHUNK
    dec, fs, te, gc = tables

    def specs(chunk):
        cols_ = [pl.BlockSpec((None, RET_CHUNK, width), lambda b_, c, o=o: (b_, chunk(c), o)) for o in range(3)]
        return cols_ + [pl.BlockSpec((RET_CHUNK, LANES), lambda b_, c: (chunk(c), 0))] * 2

    fwd = lambda c: c
    bwd = lambda c: nc - 1 - c
    fixed = lambda a: pl.BlockSpec(a.shape, lambda b_, c: (0,) * a.ndim)
    s_spec = pl.BlockSpec((None,) + s0.shape[1:], lambda b_, c: (b_, 0, 0, 0, 0))
    y_shape = jax.ShapeDtypeStruct((bsz, seq_len, width), F32)
    return pl.pallas_call(
        functools.partial(_ret_scan_kernel, rope=rope, scale=float(LANES) ** -0.5),
        grid=(bsz, nc),
        in_specs=specs(fwd) + specs(bwd) + [fixed(dec), fixed(fs), fixed(te), fixed(gc), s_spec],
        out_specs=[pl.BlockSpec((None, RET_CHUNK, width), lambda b_, c: (b_, c, 0)),
                   pl.BlockSpec((None, RET_CHUNK, width), lambda b_, c: (b_, nc - 1 - c, 0)),
                   s_spec],
        out_shape=[y_shape, y_shape, jax.ShapeDtypeStruct(s0.shape, F32)],
        scratch_shapes=[pltpu.VMEM(s0.shape[1:], F32)],
        compiler_params=_cparams("parallel", "arbitrary"),
        name="ret_scan",
    )(p, p, p, cos, sin, p, p, p, cos, sin, dec, fs, te, gc, s0)


def _ret_tables(n_heads):
    n = RET_CHUNK
    lg = jnp.log(1.0 - 2.0 ** (-5.0 - jnp.arange(n_heads, dtype=F32)))
    pos = jnp.arange(n, dtype=F32)
    diff = pos[:, None] - pos[None, :]
    masks = (diff >= 0, diff < 0)
    dists = (diff, -diff)
    dec = jnp.stack([jnp.where(m, jnp.exp(jnp.where(m, dd, 0.0)[None] * lg[:, None, None]), 0.0)
                     for m, dd in zip(masks, dists)])
    done = jnp.stack([pos, n - 1.0 - pos])
    lanes = jnp.repeat(lg, LANES)[None, None, :]
    fs = jnp.exp((done[:, :, None] + 1.0) * lanes)
    te = jnp.exp((n - 1.0 - done[:, :, None]) * lanes)
    gc = jnp.broadcast_to(jnp.exp(n * lg)[:, None, None], (n_heads, 1, LANES))
    return dec, fs, te, gc


def _ret_finish_kernel(yf_ref, yb_ref, g_ref, o_ref):
    y = yf_ref[...] + yb_ref[...]
    y = y * lax.rsqrt(jnp.mean(y * y, axis=-1, keepdims=True) + EPS)
    g = g_ref[...]
    o_ref[...] = (y * (g * jax.nn.sigmoid(g))).astype(o_ref.dtype)


def ret_finish(yf, yb, p2, tl):
    m, width = yf.shape
    gb = 3 * width // LANES
    blk = lambda i, h: (i, h)
    return pl.pallas_call(
        _ret_finish_kernel,
        grid=(m // tl, width // LANES),
        in_specs=[pl.BlockSpec((tl, LANES), blk), pl.BlockSpec((tl, LANES), blk),
                  pl.BlockSpec((tl, LANES), lambda i, h: (i, gb + h))],
        out_specs=pl.BlockSpec((tl, LANES), blk),
        out_shape=jax.ShapeDtypeStruct((m, width), BF16),
        compiler_params=_cparams("parallel", "parallel"),
        name="ret_finish",
    )(yf, yb, p2)


def _rope_tables(seq_len):
    n = LANES // 4
    t = jnp.arange(seq_len)
    inv = ROPE_BASE ** (-jnp.arange(n, dtype=F32) / n)
    ang_row = (t // GRID_W).astype(F32)[:, None] * inv
    ang_col = (t % GRID_W).astype(F32)[:, None] * inv
    cr, sr, cc, sc = jnp.cos(ang_row), jnp.sin(ang_row), jnp.cos(ang_col), jnp.sin(ang_col)
    return jnp.concatenate([cr, cr, cc, cc], axis=-1), jnp.concatenate([-sr, sr, -sc, sc], axis=-1)


def _pad_rows(w, rows, at):
    return jnp.zeros((rows, w.shape[1]), w.dtype).at[at:at + w.shape[0]].set(w)


def kernel(x, c, ctx, c_ctx, w_ada, b_ada, norm_pre_mix, norm_post_mix, norm_pre_ffn, norm_post_ffn, w_in, rw_conv, rw_w0, rw_w_up, rw_a0, rw_a_up, rw_g_up, rw_k_k, rw_k_a, rw_r_k, rw_lnx_w, rw_lnx_b, w_out, ffn_w_gate, ffn_w_up, ffn_conv, ffn_conv_b, ffn_w_down):
    bsz, seq_len, d = x.shape
    ctx_len = ctx.shape[1]
    n_layers = w_ada.shape[0]
    assert n_layers == 1, "context-stream outputs are only needed between layers"
    rw_w = rw_k_k.shape[1]
    ret_w = w_out.shape[1] - rw_w
    rw_cols = 3 * rw_w + 2 * DECAY_LORA + 2 * AAA_LORA + GATE_LORA
    rw_pad = -rw_cols % (2 * LANES)
    d_ff = ffn_w_gate.shape[2]
    m = bsz * seq_len
    flat = lambda a: a.reshape(-1, a.shape[-1])
    l = 0

    cond = jnp.zeros((8, d), F32).at[:bsz].set(c).at[bsz].set(c_ctx)
    mod = ada_modulation(cond, w_ada[l], b_ada[l][None])
    sh1, sc1, g1, sh2, sc2, g2 = [mm[:bsz, None, :] for mm in jnp.split(mod, 6, axis=-1)]
    sh_c = jnp.broadcast_to(mod[bsz, :d], (bsz, 1, d))
    sc_c = jnp.broadcast_to(mod[bsz, d:2 * d], (bsz, 1, d))

    w_rw = jnp.pad(w_in[l][:, :rw_cols].astype(BF16), ((0, 0), (0, rw_pad)))
    w_ret = w_in[l][:, rw_cols:].astype(BF16)
    g_pre = norm_pre_mix[l][None]
    in_proj = lambda a, sc, sh, w, rows, tm: norm_mod_matmul(flat(a), g_pre, sc, sh, w, rows, tm, w.shape[1] // 2
                                                             ).reshape(bsz, rows, w.shape[1])
    tm_in = min(TM_IN, seq_len)
    p_rw = in_proj(x, sc1, sh1, w_rw, seq_len, tm_in)
    p_ret = in_proj(x, sc1, sh1, w_ret, seq_len, tm_in)
    pc_rw = in_proj(ctx, sc_c, sh_c, w_rw, ctx_len, ctx_len)
    pc_ret = in_proj(ctx, sc_c, sh_c, w_ret, ctx_len, ctx_len)

    cw = jnp.pad(rw_conv[l], ((0, 0), (0, rw_pad)))
    lora_rows = 2 * DECAY_LORA + 2 * AAA_LORA
    split = lambda w: jnp.stack(_bf16_terms(w, 2))
    wup = split(jnp.stack([_pad_rows(rw_w_up[l][dd], lora_rows, dd * DECAY_LORA) for dd in range(2)]))
    aup = split(jnp.stack([_pad_rows(rw_a_up[l][dd], lora_rows, 2 * DECAY_LORA + dd * AAA_LORA) for dd in range(2)]))
    gup = split(_pad_rows(rw_g_up[l], 256, 0))
    vec = jnp.concatenate([rw_w0[l], rw_a0[l], rw_k_k[l][None], rw_k_a[l][None], rw_r_k[l].reshape(1, rw_w)], axis=0)
    vec = jnp.pad(vec, ((0, 1), (0, 0)))
    head_of_lane = jnp.arange(rw_w) // RW_HEAD_DIM
    e = (head_of_lane[:, None] == jnp.arange(LANES)[None, :]).astype(BF16)
    et = e.T
    prep = functools.partial(rw_prep, cw=cw, wup=wup, aup=aup, gup=gup, vec=vec, e=e, et=et, width=rw_w)
    scan_args = lambda o: (o[0:4] + [o[10], o[4]], o[5:9] + [o[10], o[9]])
    s0 = jnp.zeros((bsz, rw_w // LANES, RW_CAT, RW_CAT), F32)
    outs = list(prep(pc_rw, seq_len=ctx_len, tl=min(TL_PREP, ctx_len)))
    _, _, s_rw = rw_scan(*scan_args(outs), s0)
    outs = list(prep(p_rw, seq_len=seq_len, tl=TL_PREP))
    yf, yb, _ = rw_scan(*scan_args(outs), s_rw)
    o_rw = rw_finish(flat(yf), flat(yb), flat(outs[11]), flat(outs[12]), rw_lnx_w[l][None], rw_lnx_b[l][None],
                     e, et, TL_FINISH)

    n_ret_heads = ret_w // LANES
    tables = _ret_tables(n_ret_heads)
    cos, sin = _rope_tables(seq_len)
    s0 = jnp.zeros((bsz, 2, n_ret_heads, LANES, LANES), F32)
    _, _, s_ret = ret_scan(pc_ret, cos[:ctx_len], sin[:ctx_len], tables, s0, rope=False)
    yf, yb, _ = ret_scan(p_ret, cos, sin, tables, s_ret, rope=True)
    o_ret = ret_finish(flat(yf), flat(yb), flat(p_ret), TL_FINISH)

    w_o = w_out[l].astype(BF16)
    x1, h = out_proj_residual(o_rw, o_ret, w_o[:rw_w], w_o[rw_w:], flat(x), g1, norm_post_mix[l][None],
                              norm_pre_ffn[l][None], sc2, sh2, seq_len, min(TM_OUT, seq_len))
    t = ffn_up(h, ffn_w_gate[l].astype(BF16), ffn_w_up[l].astype(BF16), ffn_conv[l].reshape(9, d_ff),
               ffn_conv_b[l][None], seq_len, min(TM_FFN, seq_len), TN_FFN)
    out = down_proj_residual(t, ffn_w_down[l].astype(BF16), x1, g2, norm_post_ffn[l][None], seq_len,
                             min(TM_DOWN, seq_len), d_ff // 4)
    return out.reshape(bsz, seq_len, d)
```

```python
import functools

import jax
import jax.numpy as jnp
from jax import lax
from jax.experimental import pallas as pl
from jax.experimental.pallas import tpu as pltpu

F32 = jnp.float32
BF16 = jnp.bfloat16
HI = lax.Precision.HIGHEST

LANES = 128
EPS = 1e-6
GRID_W = 64
RW_HEAD_DIM = 64
DECAY_LORA = 64
AAA_LORA = 64
GATE_LORA = 160
LNX_EPS = 64e-5
RET_CHUNK = 128
ROPE_BASE = 10000.0
RW_CHUNK = 64
RW_SUB = 16
RW_PROBLEMS = 2
RW_CAT = RW_PROBLEMS * RW_CHUNK
VMEM_LIMIT = 56 * 1024 * 1024

TM_IN, TN_IN_RW, TN_IN_RET, TM_OUT, TM_FFN, TN_FFN, TM_DOWN, TK_DOWN = 1024, 512, 1024, 512, 1024, 512, 1024, 512
TL_PREP, TL_FINISH = 256, 512


def _cparams(*sem):
    return pltpu.CompilerParams(dimension_semantics=sem, vmem_limit_bytes=VMEM_LIMIT)


def _dot(a, b, prec=None):
    return jnp.dot(a, b, precision=prec, preferred_element_type=F32)


def _dot_nt(a, b, prec=None):
    return lax.dot_general(a, b, (((1,), (1,)), ((), ())), precision=prec, preferred_element_type=F32)


def _dot_tn(a, b, prec=None):
    return lax.dot_general(a, b, (((0,), (0,)), ((), ())), precision=prec, preferred_element_type=F32)


def _bf16_terms(a, n):
    terms = []
    for _ in range(n):
        t = a.astype(BF16)
        terms.append(t)
        a = a - t.astype(F32)
    return terms


def _dot_sel(sel, x, left=True):
    hi, lo = _bf16_terms(x, 2)
    return (_dot(sel, hi) + _dot(sel, lo)) if left else (_dot(hi, sel) + _dot(lo, sel))


def _dot_x3(a, b_hi, b_lo):
    ah, al = _bf16_terms(a, 2)
    return _dot(ah, b_hi) + (_dot(ah, b_lo) + _dot(al, b_hi))


def _rms(x, g):
    return x * lax.rsqrt(jnp.mean(x * x, axis=-1, keepdims=True) + EPS) * g


def _each(f, *lists):
    return [f(*xs) for xs in zip(*lists)]


def _ada_kernel(s_ref, w_ref, b_ref, o_ref):
    s = s_ref[...]
    s = s * jax.nn.sigmoid(s)
    o_ref[...] = _dot(s, w_ref[...], HI) + b_ref[...]


def ada_modulation(s, w, b, tn=1024):
    m, d = s.shape
    n = w.shape[1]
    return pl.pallas_call(
        _ada_kernel,
        grid=(n // tn,),
        in_specs=[pl.BlockSpec((m, d), lambda j: (0, 0)),
                  pl.BlockSpec((d, tn), lambda j: (0, j)),
                  pl.BlockSpec((1, tn), lambda j: (0, j))],
        out_specs=pl.BlockSpec((m, tn), lambda j: (0, j)),
        out_shape=jax.ShapeDtypeStruct((m, n), F32),
        compiler_params=_cparams("arbitrary"),
        name="ada_modulation",
    )(s, w, b)


def _norm_mm_kernel(x_ref, g_ref, sc_ref, sh_ref, w_ref, o_ref, xm_ref):
    @pl.when(pl.program_id(1) == 0)
    def _():
        y = _rms(x_ref[...], g_ref[...])
        xm_ref[...] = (y * (1.0 + sc_ref[0]) + sh_ref[0]).astype(BF16)

    o_ref[...] = _dot(xm_ref[...], w_ref[...]).astype(o_ref.dtype)


def norm_mod_matmul(x, g, sc, sh, w, rows_per_batch, tm, tn):
    m, d = x.shape
    n = w.shape[1]
    bpb = rows_per_batch // tm
    return pl.pallas_call(
        _norm_mm_kernel,
        grid=(m // tm, n // tn),
        in_specs=[pl.BlockSpec((tm, d), lambda i, j: (i, 0)),
                  pl.BlockSpec((1, d), lambda i, j: (0, 0)),
                  pl.BlockSpec((1, 1, d), lambda i, j: (i // bpb, 0, 0)),
                  pl.BlockSpec((1, 1, d), lambda i, j: (i // bpb, 0, 0)),
                  pl.BlockSpec((d, tn), lambda i, j: (0, j))],
        out_specs=[pl.BlockSpec((tm, tn), lambda i, j: (i, j)), pl.BlockSpec((tm, d), lambda i, j: (i, 0))],
        out_shape=[jax.ShapeDtypeStruct((m, n), F32), jax.ShapeDtypeStruct((m, d), BF16)],
        compiler_params=_cparams("parallel", "arbitrary"),
        name="norm_mod_matmul",
    )(x, g, sc, sh, w)


def _mm_kernel(a_ref, w_ref, o_ref):
    o_ref[...] = _dot(a_ref[...], w_ref[...]).astype(o_ref.dtype)


def matmul(a, w, tm, tn, out_dtype):
    m, k = a.shape
    n = w.shape[1]
    return pl.pallas_call(
        _mm_kernel,
        grid=(m // tm, n // tn),
        in_specs=[pl.BlockSpec((tm, k), lambda i, j: (i, 0)),
                  pl.BlockSpec((k, tn), lambda i, j: (0, j))],
        out_specs=pl.BlockSpec((tm, tn), lambda i, j: (i, j)),
        out_shape=jax.ShapeDtypeStruct((m, n), out_dtype),
        compiler_params=_cparams("parallel", "arbitrary"),
        name="matmul",
    )(a, w)


def _out_proj_kernel(oa_ref, ob_ref, wa_ref, wb_ref, x_ref, g1_ref, gpost_ref, gpre_ref, sc2_ref, sh2_ref,
                     x1_ref, h_ref):
    out = _dot(oa_ref[...], wa_ref[...]) + _dot(ob_ref[...], wb_ref[...])
    x1 = x_ref[...] + g1_ref[0] * _rms(out, gpost_ref[...])
    x1_ref[...] = x1
    h_ref[...] = (_rms(x1, gpre_ref[...]) * (1.0 + sc2_ref[0]) + sh2_ref[0]).astype(BF16)


def out_proj_residual(oa, ob, wa, wb, x, g1, gpost, gpre, sc2, sh2, rows_per_batch, tm):
    m, ka = oa.shape
    kb = ob.shape[1]
    d = x.shape[1]
    bpb = rows_per_batch // tm
    row = lambda i: (i, 0)
    fixed = lambda i: (0, 0)
    per_b = lambda i: (i // bpb, 0, 0)
    return pl.pallas_call(
        _out_proj_kernel,
        grid=(m // tm,),
        in_specs=[pl.BlockSpec((tm, ka), row), pl.BlockSpec((tm, kb), row),
                  pl.BlockSpec((ka, d), fixed), pl.BlockSpec((kb, d), fixed),
                  pl.BlockSpec((tm, d), row),
                  pl.BlockSpec((1, 1, d), per_b),
                  pl.BlockSpec((1, d), fixed), pl.BlockSpec((1, d), fixed),
                  pl.BlockSpec((1, 1, d), per_b), pl.BlockSpec((1, 1, d), per_b)],
        out_specs=[pl.BlockSpec((tm, d), row), pl.BlockSpec((tm, d), row)],
        out_shape=[jax.ShapeDtypeStruct((m, d), F32), jax.ShapeDtypeStruct((m, d), BF16)],
        compiler_params=_cparams("parallel"),
        name="out_proj_residual",
    )(oa, ob, wa, wb, x, g1, gpost, gpre, sc2, sh2)


def _ffn_up_kernel(top_ref, mid_ref, bot_ref, wg_ref, wu_ref, cw_ref, cb_ref, o_ref, hext_ref, *, n_blocks):
    i = pl.program_id(1)
    tm = mid_ref.shape[0]

    @pl.when(pl.program_id(2) == 0)
    def _():
        hext_ref[0:GRID_W] = top_ref[...]
        hext_ref[GRID_W:GRID_W + tm] = mid_ref[...]
        hext_ref[GRID_W + tm:] = bot_ref[...]

    gate = _dot(hext_ref[...], wg_ref[...].astype(BF16))
    up = _dot(mid_ref[...], wu_ref[...].astype(BF16))
    tn = gate.shape[1]
    mid = gate[GRID_W:GRID_W + tm]
    above = jnp.where(i > 0, gate[:GRID_W], 0.0)
    below = jnp.where(i < n_blocks - 1, gate[GRID_W + tm:], 0.0)
    above = jnp.concatenate([above, mid[:tm - GRID_W]], axis=0)
    below = jnp.concatenate([mid[GRID_W:], below], axis=0)
    w = cw_ref[...]
    col = lax.broadcasted_iota(jnp.int32, (tm, tn), 0) % GRID_W

    def column_sum(dx):
        return above * w[dx:dx + 1] + mid * w[3 + dx:4 + dx] + below * w[6 + dx:7 + dx]

    left = jnp.where(col > 0, pltpu.roll(column_sum(0), 1, axis=0), 0.0)
    right = jnp.where(col < GRID_W - 1, pltpu.roll(column_sum(2), tm - 1, axis=0), 0.0)
    gt = column_sum(1) + left + right + cb_ref[...]
    o_ref[...] = (gt * jax.nn.sigmoid(gt) * up).astype(o_ref.dtype)


def ffn_up(h, wg, wu, w9, b, seq_len, tm, tn):
    m, d = h.shape
    f = wg.shape[1]
    nb = seq_len // tm
    bsz = m // seq_len
    hpb = tm // GRID_W
    n_halo = seq_len // GRID_W
    main = lambda b_, i, j: (b_ * nb + i, 0)
    top = lambda b_, i, j: (b_ * n_halo + jnp.maximum(i * hpb - 1, 0), 0)
    bot = lambda b_, i, j: (b_ * n_halo + jnp.minimum((i + 1) * hpb, n_halo - 1), 0)
    col = lambda b_, i, j: (0, j)
    return pl.pallas_call(
        functools.partial(_ffn_up_kernel, n_blocks=nb),
        grid=(bsz, nb, f // tn),
        in_specs=[pl.BlockSpec((GRID_W, d), top), pl.BlockSpec((tm, d), main), pl.BlockSpec((GRID_W, d), bot),
                  pl.BlockSpec((d, tn), col), pl.BlockSpec((d, tn), col),
                  pl.BlockSpec((9, tn), col), pl.BlockSpec((1, tn), col)],
        out_specs=pl.BlockSpec((tm, tn), lambda b_, i, j: (b_ * nb + i, j)),
        out_shape=jax.ShapeDtypeStruct((m, f), BF16),
        scratch_shapes=[pltpu.VMEM((tm + 2 * GRID_W, d), BF16)],
        compiler_params=_cparams("parallel", "parallel", "arbitrary"),
        name="ffn_up",
    )(h, h, h, wg, wu, w9, b)


def _down_proj_kernel(t_ref, w_ref, x1_ref, g2_ref, gpost_ref, o_ref, acc_ref):
    k = pl.program_id(1)

    @pl.when(k == 0)
    def _():
        acc_ref[...] = jnp.zeros_like(acc_ref)

    acc_ref[...] += _dot(t_ref[...], w_ref[...])

    @pl.when(k == pl.num_programs(1) - 1)
    def _():
        o_ref[...] = x1_ref[...] + g2_ref[0] * _rms(acc_ref[...], gpost_ref[...])


def down_proj_residual(t, w, x1, g2, gpost, rows_per_batch, tm, tk):
    m, kk = t.shape
    d = w.shape[1]
    bpb = rows_per_batch // tm
    return pl.pallas_call(
        _down_proj_kernel,
        grid=(m // tm, kk // tk),
        in_specs=[pl.BlockSpec((tm, tk), lambda i, k: (i, k)),
                  pl.BlockSpec((tk, d), lambda i, k: (k, 0)),
                  pl.BlockSpec((tm, d), lambda i, k: (i, 0)),
                  pl.BlockSpec((1, 1, d), lambda i, k: (i // bpb, 0, 0)),
                  pl.BlockSpec((1, d), lambda i, k: (0, 0))],
        out_specs=pl.BlockSpec((tm, d), lambda i, k: (i, 0)),
        out_shape=jax.ShapeDtypeStruct((m, d), F32),
        scratch_shapes=[pltpu.VMEM((tm, d), F32)],
        compiler_params=_cparams("parallel", "arbitrary"),
        name="down_proj_residual",
    )(t, w, x1, g2, gpost)


def _head_sum(x, e, et):
    return _dot_sel(et, _dot_sel(e, x, left=False), left=False)


def _rw_prep_kernel(prev_ref, cur_ref, next_ref, cw_ref, wup_ref, aup_ref, gup_ref, vec_ref, e_ref, et_ref,
                    af_ref, btf_ref, ktf_ref, rf_ref, wtf_ref, ab_ref, btb_ref, ktb_ref, rb_ref, wtb_ref,
                    v_ref, bon_ref, g_ref, *, n_blocks, width):
    i = pl.program_id(1)
    tl = cur_ref.shape[0]
    cur = cur_ref[...]
    rows = lax.broadcasted_iota(jnp.int32, cur.shape, 0)
    prev_row = jnp.where(i > 0, prev_ref[7:8, :], 0.0)
    next_row = jnp.where(i < n_blocks - 1, next_ref[0:1, :], 0.0)
    before = jnp.where(rows == 0, prev_row, pltpu.roll(cur, 1, axis=0))
    after = jnp.where(rows == tl - 1, next_row, pltpu.roll(cur, tl - 1, axis=0))
    cw = cw_ref[...]
    c = before * cw[0:1] + cur * cw[1:2] + after * cw[2:3]

    w_ = width
    k = c[:, :w_]
    v = c[:, w_:2 * w_]
    lo = c[:, 2 * w_:2 * w_ + 256]
    r = c[:, 2 * w_ + 256:3 * w_ + 256]
    gl = c[:, 3 * w_ + 256:3 * w_ + 512]
    vec = vec_ref[...]
    e = e_ref[...]
    et = et_ref[...]
    k_k, k_a, r_k = vec[4:5], vec[5:6], vec[6:7]

    kk = k * k_k
    kk = kk * lax.rsqrt(_head_sum(kk * kk, e, et) + 1e-12)
    v_ref[...] = v.astype(BF16)
    tlo = jnp.tanh(lo)
    kd_sum = jnp.zeros_like(k)
    ti = lax.broadcasted_iota(jnp.int32, (tl, tl), 0)
    tj = lax.broadcasted_iota(jnp.int32, (tl, tl), 1)
    same_chunk = ti // RW_CHUNK == tj // RW_CHUNK
    wi = lax.broadcasted_iota(jnp.int32, (tl // 8, tl), 0)
    wj = lax.broadcasted_iota(jnp.int32, (tl // 8, tl), 1)
    chunk_rows = (wi * 8) // RW_CHUNK == wj // RW_CHUNK
    dirs = ((af_ref, btf_ref, ktf_ref, rf_ref, wtf_ref, tj <= ti), (ab_ref, btb_ref, ktb_ref, rb_ref, wtb_ref, tj >= ti))
    for d, (a_ref, bt_ref, kt_ref, r_ref, wt_ref, done) in enumerate(dirs):
        z = vec[d:d + 1] + _dot_x3(tlo, wup_ref[0, d], wup_ref[1, d])
        nz = -z
        softplus = jnp.maximum(nz, 0.0) + jnp.log1p(jnp.exp(-jnp.abs(nz)))
        w_log = -softplus - 0.5
        lw = -jnp.exp(w_log)
        a = jax.nn.sigmoid(vec[2 + d:3 + d] + _dot_x3(lo, aup_ref[0, d], aup_ref[1, d]))
        kd = k * (1.0 + (a - 1.0) * k_a)
        kd_sum = kd_sum + kd
        sel = jnp.concatenate([(same_chunk & done).astype(F32), chunk_rows.astype(F32)], axis=0).astype(BF16)
        sums = _dot_sel(sel, lw)
        cum = sums[:tl]
        w_inv = jnp.exp(-cum)
        a_ref[...] = (-kk * jnp.exp(cum - lw)).astype(BF16)
        bt_ref[...] = (kk * a * w_inv).astype(BF16)
        kt_ref[...] = (kd * w_inv).astype(BF16)
        r_ref[...] = (r * jnp.exp(cum)).astype(BF16)
        wt_ref[...] = jnp.exp(sums[tl:])
    bon_ref[...] = _head_sum(r * kd_sum * r_k, e, et) * v
    g_ref[...] = _dot_x3(jax.nn.sigmoid(gl), gup_ref[0], gup_ref[1])


def rw_prep(p, cw, wup, aup, gup, vec, e, et, seq_len, tl, width):
    bsz, _, rw_cols = p.shape
    nb = seq_len // tl
    hb = tl // 8
    n_halo = seq_len // 8
    fixed = lambda a: pl.BlockSpec(a.shape, lambda b_, i: (0,) * a.ndim)
    row_spec = pl.BlockSpec((None, tl, width), lambda b_, i: (b_, i, 0))
    wt_spec = pl.BlockSpec((None, tl // 8, width), lambda b_, i: (b_, i, 0))
    act = lambda dt: jax.ShapeDtypeStruct((bsz, seq_len, width), dt)
    wt = jax.ShapeDtypeStruct((bsz, seq_len // 8, width), F32)
    per_dir_specs = [row_spec] * 4 + [wt_spec]
    per_dir_shapes = [act(BF16)] * 4 + [wt]
    return pl.pallas_call(
        functools.partial(_rw_prep_kernel, n_blocks=nb, width=width),
        grid=(bsz, nb),
        in_specs=[pl.BlockSpec((None, 8, rw_cols), lambda b_, i: (b_, jnp.maximum(i * hb - 1, 0), 0)),
                  pl.BlockSpec((None, tl, rw_cols), lambda b_, i: (b_, i, 0)),
                  pl.BlockSpec((None, 8, rw_cols), lambda b_, i: (b_, jnp.minimum((i + 1) * hb, n_halo - 1), 0)),
                  fixed(cw), fixed(wup), fixed(aup), fixed(gup), fixed(vec), fixed(e), fixed(et)],
        out_specs=per_dir_specs * 2 + [row_spec] * 3,
        out_shape=per_dir_shapes * 2 + [act(BF16), act(F32), act(F32)],
        compiler_params=_cparams("parallel", "parallel"),
        name="rw_prep",
    )(p, p, p, cw, wup, aup, gup, vec, e, et)


def _block_diag(y, diag):
    yb = y.astype(BF16)
    tiled = jnp.concatenate([yb] * RW_PROBLEMS, axis=0)
    return jnp.where(diag, tiled, jnp.zeros_like(tiled))


def _unit_tri_inverse(a, eye, blk, off1, off2, diag):
    n = RW_CHUNK
    mm = lambda xs, ys: _each(lambda x, y: _dot(x.astype(BF16), _block_diag(y, diag)), xs, ys)
    add = lambda xs, ys: _each(jnp.add, xs, ys)
    stack = lambda xs, ys: _each(lambda x, y: jnp.concatenate([x, y], axis=0), xs, ys)
    d = [ai * blk for ai in a]
    x = [eye + di for di in d]
    d2 = mm(d, d)
    both = mm(stack(x, d2), d2)
    x = _each(lambda xi, p: xi + p[:n], x, both)
    d4 = [p[n:] for p in both]
    both = mm(stack(x, d4), d4)
    x = _each(lambda xi, p: xi + p[:n], x, both)
    d8 = [p[n:] for p in both]
    x = add(x, mm(x, d8))
    x = add(x, mm(mm(x, [ai * off1 for ai in a]), x))
    x = add(x, mm(mm(x, [ai * off2 for ai in a]), x))
    return x


def _rw_chunk(a, bt, kt, r, v, wtot, s, masks):
    strict, incl, eye, blk, off1, off2, diag = masks
    n = RW_CHUNK
    bf = lambda xs: [x.astype(BF16) for x in xs]
    bdiag = lambda xs: [_block_diag(x, diag) for x in xs]
    stack = lambda xs, ys: _each(lambda x, y: jnp.concatenate([x, y], axis=0), xs, ys)
    ar = stack(a, r)
    sc = _each(_dot_nt, ar, stack(bdiag(bt), bdiag(kt)))
    sc_s = _each(_dot_nt, ar, bf(s))
    a_ab = _each(lambda x, m: jnp.where(m, x[:n, :RW_CAT], 0.0), sc, strict)
    a_ak = _each(lambda x, m: jnp.where(m, x[:n, RW_CAT:], 0.0), sc, strict)
    r_rb = _each(lambda x, m: jnp.where(m, x[n:, :RW_CAT], 0.0), sc, incl)
    r_rk = _each(lambda x, m: jnp.where(m, x[n:, RW_CAT:], 0.0), sc, incl)
    inv = _unit_tri_inverse(a_ab, eye, blk, off1, off2, diag)
    kv = _each(_dot, bf(stack(a_ak, r_rk)), bdiag(v))
    rhs = _each(lambda x, y: x[:n] + y[:n], sc_s, kv)
    u = _each(_dot, bf(inv), bdiag(rhs))
    y = _each(lambda x, p, q: x[n:] + p + q[n:], sc_s, _each(_dot, bf(r_rb), bdiag(u)), kv)
    to_end = lambda x, w: (x.astype(F32) * w).astype(BF16)
    bk = _each(lambda x, z, w: jnp.concatenate([to_end(x, w), to_end(z, w)], axis=0), bt, kt, wtot)
    upd = _each(_dot_tn, stack(bf(u), v), bk)
    return y, _each(lambda si, w, x: si * w + jnp.where(diag, x, 0.0), s, wtot, upd)


def _rw_scan_kernel(*refs):
    fwd, bwd = refs[0:6], refs[6:12]
    s0_ref, yf_ref, yb_ref, sout_ref, s_ref = refs[12:]
    c = pl.program_id(1)

    @pl.when(c == 0)
    def _():
        s_ref[...] = s0_ref[...]

    n = RW_CHUNK
    t = lax.broadcasted_iota(jnp.int32, (n, RW_CAT), 0)
    lane = lax.broadcasted_iota(jnp.int32, (n, RW_CAT), 1)
    step = lane % n
    eye = (t == step).astype(F32)
    blk = (t // RW_SUB == step // RW_SUB).astype(F32)
    off1 = ((t // (2 * RW_SUB) == step // (2 * RW_SUB)) & (t // RW_SUB != step // RW_SUB)).astype(F32)
    off2 = (t // (2 * RW_SUB) != step // (2 * RW_SUB)).astype(F32)
    di = lax.broadcasted_iota(jnp.int32, (RW_CAT, RW_CAT), 0) // n
    dj = lax.broadcasted_iota(jnp.int32, (RW_CAT, RW_CAT), 1) // n
    pairs = s_ref.shape[1]
    lanes = [slice(g * LANES, (g + 1) * LANES) for g in range(pairs)]
    strict = [t > step] * pairs + [t < step] * pairs
    incl = [t >= step] * pairs + [t <= step] * pairs
    masks = (strict, incl, eye, blk, off1, off2, di == dj)
    ops = [[ref[:, ln] for ref in (f, b) for ln in lanes] for f, b in zip(fwd[:5], bwd[:5])]
    wtot = [ref[0:1, ln] for ref in (fwd[5], bwd[5]) for ln in lanes]
    states = [s_ref[d, g] for d in range(2) for g in range(pairs)]
    ys, s_new = _rw_chunk(*ops, wtot, states, masks)
    for d, y_ref in enumerate((yf_ref, yb_ref)):
        for g in range(pairs):
            y_ref[:, lanes[g]] = ys[d * pairs + g]
            s_ref[d, g] = s_new[d * pairs + g]

    @pl.when(c == pl.num_programs(1) - 1)
    def _():
        sout_ref[...] = s_ref[...]


def rw_scan(fwd, bwd, s0):
    bsz, seq_len, width = fwd[0].shape
    nc = seq_len // RW_CHUNK
    pairs = width // LANES
    f_spec = pl.BlockSpec((None, RW_CHUNK, width), lambda b_, c: (b_, c, 0))
    b_spec = pl.BlockSpec((None, RW_CHUNK, width), lambda b_, c: (b_, nc - 1 - c, 0))
    fw_spec = pl.BlockSpec((None, 8, width), lambda b_, c: (b_, c, 0))
    bw_spec = pl.BlockSpec((None, 8, width), lambda b_, c: (b_, nc - 1 - c, 0))
    s_spec = pl.BlockSpec((None, 2, pairs, RW_CAT, RW_CAT), lambda b_, c: (b_, 0, 0, 0, 0))
    y_shape = jax.ShapeDtypeStruct((bsz, seq_len, width), F32)
    return pl.pallas_call(
        _rw_scan_kernel,
        grid=(bsz, nc),
        in_specs=[f_spec] * 5 + [fw_spec] + [b_spec] * 5 + [bw_spec] + [s_spec],
        out_specs=[f_spec, b_spec, s_spec],
        out_shape=[y_shape, y_shape, jax.ShapeDtypeStruct(s0.shape, F32)],
        scratch_shapes=[pltpu.VMEM((2, pairs, RW_CAT, RW_CAT), F32)],
        compiler_params=_cparams("parallel", "arbitrary"),
        name="rw_scan",
    )(*fwd, *bwd, s0)


def _rw_finish_kernel(yf_ref, yb_ref, bon_ref, g_ref, lnw_ref, lnb_ref, e_ref, et_ref, o_ref):
    y = yf_ref[...] + yb_ref[...]
    e = e_ref[...]
    et = et_ref[...]
    mu = _head_sum(y, e, et) * (1.0 / RW_HEAD_DIM)
    yc = y - mu
    var = _head_sum(yc * yc, e, et) * (1.0 / RW_HEAD_DIM)
    yn = yc * lax.rsqrt(var + LNX_EPS) * lnw_ref[...] + lnb_ref[...]
    o_ref[...] = ((yn + bon_ref[...]) * g_ref[...]).astype(o_ref.dtype)


def rw_finish(yf, yb, bon, g, lnw, lnb, e, et, tl):
    m, width = yf.shape
    row = pl.BlockSpec((tl, width), lambda i: (i, 0))
    fixed = lambda a: pl.BlockSpec(a.shape, lambda i: (0, 0))
    return pl.pallas_call(
        _rw_finish_kernel,
        grid=(m // tl,),
        in_specs=[row, row, row, row, fixed(lnw), fixed(lnb), fixed(e), fixed(et)],
        out_specs=row,
        out_shape=jax.ShapeDtypeStruct((m, width), BF16),
        compiler_params=_cparams("parallel"),
        name="rw_finish",
    )(yf, yb, bon, g, lnw, lnb, e, et)


def _rope(x, cos, sin):
    quarter = LANES // 4
    width = x.shape[1]
    lane = lax.broadcasted_iota(jnp.int32, x.shape, 1)
    first = (lane // quarter) % 2 == 0
    partner = jnp.where(first, pltpu.roll(x, width - quarter, axis=1), pltpu.roll(x, quarter, axis=1))
    return x * cos + partner * sin


def _ret_scan_kernel(kf_ref, vf_ref, qf_ref, cf_ref, sf_ref, kb_ref, vb_ref, qb_ref, cb_ref, sb_ref,
                     dec_ref, fs_ref, te_ref, gc_ref, s0_ref, yf_ref, yb_ref, sout_ref, s_ref, *, rope, scale):
    c = pl.program_id(1)

    @pl.when(c == 0)
    def _():
        s_ref[...] = s0_ref[...]

    n_heads = s_ref.shape[1]
    heads = [slice(h * LANES, (h + 1) * LANES) for h in range(n_heads)]
    qs, ks, vs, decay, ss = [], [], [], [], []
    for d, (k_ref, v_ref, q_ref, cos_ref, sin_ref) in enumerate(
            ((kf_ref, vf_ref, qf_ref, cf_ref, sf_ref), (kb_ref, vb_ref, qb_ref, cb_ref, sb_ref))):
        q = q_ref[...]
        k = k_ref[...]
        if rope:
            cos = jnp.concatenate([cos_ref[...]] * n_heads, axis=1)
            sin = jnp.concatenate([sin_ref[...]] * n_heads, axis=1)
            q = _rope(q, cos, sin)
            k = _rope(k, cos, sin)
        k = k * scale
        v = v_ref[...].astype(BF16)
        q_cross = (q * fs_ref[d]).astype(BF16)
        k_end = (k * te_ref[d]).astype(BF16)
        q = q.astype(BF16)
        k = k.astype(BF16)
        for h, hs in enumerate(heads):
            qs.append((q[:, hs], q_cross[:, hs]))
            ks.append((k[:, hs], k_end[:, hs]))
            vs.append(v[:, hs])
            decay.append(dec_ref[d, h])
            ss.append(s_ref[d, h])
    scores = _each(lambda q, k, dm: (_dot_nt(q[0], k[0]) * dm).astype(BF16), qs, ks, decay)
    cross = _each(lambda q, s: _dot(q[1], s.astype(BF16)), qs, ss)
    intra = _each(_dot, scores, vs)
    upd = _each(lambda k, v: _dot_tn(k[1], v), ks, vs)
    for d, y_ref in enumerate((yf_ref, yb_ref)):
        for h, hs in enumerate(heads):
            i = d * n_heads + h
            y_ref[:, hs] = intra[i] + cross[i]
            s_ref[d, h] = gc_ref[h] * ss[i] + upd[i]

    @pl.when(c == pl.num_programs(1) - 1)
    def _():
        sout_ref[...] = s_ref[...]


def ret_scan(p, cos, sin, tables, s0, rope):
    bsz, seq_len, cols = p.shape
    width = cols // 4
    nc = seq_len // RET_CHUNK
    dec, fs, te, gc = tables

    def specs(chunk):
        cols_ = [pl.BlockSpec((None, RET_CHUNK, width), lambda b_, c, o=o: (b_, chunk(c), o)) for o in range(3)]
        return cols_ + [pl.BlockSpec((RET_CHUNK, LANES), lambda b_, c: (chunk(c), 0))] * 2

    fwd = lambda c: c
    bwd = lambda c: nc - 1 - c
    fixed = lambda a: pl.BlockSpec(a.shape, lambda b_, c: (0,) * a.ndim)
    s_spec = pl.BlockSpec((None,) + s0.shape[1:], lambda b_, c: (b_, 0, 0, 0, 0))
    y_shape = jax.ShapeDtypeStruct((bsz, seq_len, width), F32)
    return pl.pallas_call(
        functools.partial(_ret_scan_kernel, rope=rope, scale=float(LANES) ** -0.5),
        grid=(bsz, nc),
        in_specs=specs(fwd) + specs(bwd) + [fixed(dec), fixed(fs), fixed(te), fixed(gc), s_spec],
        out_specs=[pl.BlockSpec((None, RET_CHUNK, width), lambda b_, c: (b_, c, 0)),
                   pl.BlockSpec((None, RET_CHUNK, width), lambda b_, c: (b_, nc - 1 - c, 0)),
                   s_spec],
        out_shape=[y_shape, y_shape, jax.ShapeDtypeStruct(s0.shape, F32)],
        scratch_shapes=[pltpu.VMEM(s0.shape[1:], F32)],
        compiler_params=_cparams("parallel", "arbitrary"),
        name="ret_scan",
    )(p, p, p, cos, sin, p, p, p, cos, sin, dec, fs, te, gc, s0)


def _ret_tables(n_heads):
    n = RET_CHUNK
    lg = jnp.log(1.0 - 2.0 ** (-5.0 - jnp.arange(n_heads, dtype=F32)))
    pos = jnp.arange(n, dtype=F32)
    diff = pos[:, None] - pos[None, :]
    masks = (diff >= 0, diff < 0)
    dists = (diff, -diff)
    dec = jnp.stack([jnp.where(m, jnp.exp(jnp.where(m, dd, 0.0)[None] * lg[:, None, None]), 0.0)
                     for m, dd in zip(masks, dists)])
    done = jnp.stack([pos, n - 1.0 - pos])
    lanes = jnp.repeat(lg, LANES)[None, None, :]
    fs = jnp.exp((done[:, :, None] + 1.0) * lanes)
    te = jnp.exp((n - 1.0 - done[:, :, None]) * lanes)
    gc = jnp.broadcast_to(jnp.exp(n * lg)[:, None, None], (n_heads, 1, LANES))
    return dec, fs, te, gc


def _ret_finish_kernel(yf_ref, yb_ref, g_ref, o_ref):
    g = g_ref[...]
    gate = g * jax.nn.sigmoid(g)
    for h in range(o_ref.shape[1] // LANES):
        hs = slice(h * LANES, (h + 1) * LANES)
        y = yf_ref[:, hs] + yb_ref[:, hs]
        y = y * lax.rsqrt(jnp.mean(y * y, axis=-1, keepdims=True) + EPS)
        o_ref[:, hs] = (y * gate[:, hs]).astype(o_ref.dtype)


def ret_finish(yf, yb, p2, tl):
    m, width = yf.shape
    row = pl.BlockSpec((tl, width), lambda i: (i, 0))
    return pl.pallas_call(
        _ret_finish_kernel,
        grid=(m // tl,),
        in_specs=[row, row, pl.BlockSpec((tl, width), lambda i: (i, 3))],
        out_specs=row,
        out_shape=jax.ShapeDtypeStruct((m, width), BF16),
        compiler_params=_cparams("parallel"),
        name="ret_finish",
    )(yf, yb, p2)


def _rope_tables(seq_len):
    n = LANES // 4
    t = jnp.arange(seq_len)
    inv = ROPE_BASE ** (-jnp.arange(n, dtype=F32) / n)
    ang_row = (t // GRID_W).astype(F32)[:, None] * inv
    ang_col = (t % GRID_W).astype(F32)[:, None] * inv
    cr, sr, cc, sc = jnp.cos(ang_row), jnp.sin(ang_row), jnp.cos(ang_col), jnp.sin(ang_col)
    return jnp.concatenate([cr, cr, cc, cc], axis=-1), jnp.concatenate([-sr, sr, -sc, sc], axis=-1)


def _pad_rows(w, rows, at):
    return jnp.zeros((rows, w.shape[1]), w.dtype).at[at:at + w.shape[0]].set(w)


def kernel(x, c, ctx, c_ctx, w_ada, b_ada, norm_pre_mix, norm_post_mix, norm_pre_ffn, norm_post_ffn, w_in, rw_conv, rw_w0, rw_w_up, rw_a0, rw_a_up, rw_g_up, rw_k_k, rw_k_a, rw_r_k, rw_lnx_w, rw_lnx_b, w_out, ffn_w_gate, ffn_w_up, ffn_conv, ffn_conv_b, ffn_w_down):
    bsz, seq_len, d = x.shape
    ctx_len = ctx.shape[1]
    n_layers = w_ada.shape[0]
    assert n_layers == 1, "context-stream outputs are only needed between layers"
    rw_w = rw_k_k.shape[1]
    ret_w = w_out.shape[1] - rw_w
    rw_cols = 3 * rw_w + 2 * DECAY_LORA + 2 * AAA_LORA + GATE_LORA
    rw_pad = -rw_cols % (2 * LANES)
    d_ff = ffn_w_gate.shape[2]
    m = bsz * seq_len
    flat = lambda a: a.reshape(-1, a.shape[-1])
    l = 0

    cond = jnp.zeros((8, d), F32).at[:bsz].set(c).at[bsz].set(c_ctx)
    mod = ada_modulation(cond, w_ada[l], b_ada[l][None])
    sh1, sc1, g1, sh2, sc2, g2 = [mm[:bsz, None, :] for mm in jnp.split(mod, 6, axis=-1)]
    sh_c = jnp.broadcast_to(mod[bsz, :d], (bsz, 1, d))
    sc_c = jnp.broadcast_to(mod[bsz, d:2 * d], (bsz, 1, d))

    w_in_b = w_in[l].astype(BF16)
    w_rw = jnp.pad(w_in_b[:, :rw_cols], ((0, 0), (0, rw_pad)))
    w_ret = w_in_b[:, rw_cols:]
    g_pre = norm_pre_mix[l][None]

    def in_proj(a, sc, sh, rows, tm):
        p_a, xm = norm_mod_matmul(flat(a), g_pre, sc, sh, w_rw, rows, tm, TN_IN_RW)
        p_b = matmul(xm, w_ret, tm, TN_IN_RET, F32)
        return p_a.reshape(bsz, rows, -1), p_b.reshape(bsz, rows, -1)

    p_rw, p_ret = in_proj(x, sc1, sh1, seq_len, min(TM_IN, seq_len))
    pc_rw, pc_ret = in_proj(ctx, sc_c, sh_c, ctx_len, ctx_len)

    cw = jnp.pad(rw_conv[l], ((0, 0), (0, rw_pad)))
    lora_rows = 2 * DECAY_LORA + 2 * AAA_LORA
    split = lambda w: jnp.stack(_bf16_terms(w, 2))
    wup = split(jnp.stack([_pad_rows(rw_w_up[l][dd], lora_rows, dd * DECAY_LORA) for dd in range(2)]))
    aup = split(jnp.stack([_pad_rows(rw_a_up[l][dd], lora_rows, 2 * DECAY_LORA + dd * AAA_LORA) for dd in range(2)]))
    gup = split(_pad_rows(rw_g_up[l], 256, 0))
    vec = jnp.concatenate([rw_w0[l], rw_a0[l], rw_k_k[l][None], rw_k_a[l][None], rw_r_k[l].reshape(1, rw_w)], axis=0)
    vec = jnp.pad(vec, ((0, 1), (0, 0)))
    head_of_lane = jnp.arange(rw_w) // RW_HEAD_DIM
    e = (head_of_lane[:, None] == jnp.arange(LANES)[None, :]).astype(BF16)
    et = e.T
    prep = functools.partial(rw_prep, cw=cw, wup=wup, aup=aup, gup=gup, vec=vec, e=e, et=et, width=rw_w)
    scan_args = lambda o: (o[0:4] + [o[10], o[4]], o[5:9] + [o[10], o[9]])
    s0 = jnp.zeros((bsz, 2, rw_w // LANES, RW_CAT, RW_CAT), F32)
    outs = list(prep(pc_rw, seq_len=ctx_len, tl=min(TL_PREP, ctx_len)))
    _, _, s_rw = rw_scan(*scan_args(outs), s0)
    outs = list(prep(p_rw, seq_len=seq_len, tl=TL_PREP))
    yf, yb, _ = rw_scan(*scan_args(outs), s_rw)
    o_rw = rw_finish(flat(yf), flat(yb), flat(outs[11]), flat(outs[12]), rw_lnx_w[l][None], rw_lnx_b[l][None],
                     e, et, TL_FINISH)

    n_ret_heads = ret_w // LANES
    tables = _ret_tables(n_ret_heads)
    cos, sin = _rope_tables(seq_len)
    s0 = jnp.zeros((bsz, 2, n_ret_heads, LANES, LANES), F32)
    _, _, s_ret = ret_scan(pc_ret, cos[:ctx_len], sin[:ctx_len], tables, s0, rope=False)
    yf, yb, _ = ret_scan(p_ret, cos, sin, tables, s_ret, rope=True)
    o_ret = ret_finish(flat(yf), flat(yb), flat(p_ret), TL_FINISH)

    w_o = w_out[l].astype(BF16)
    x1, h = out_proj_residual(o_rw, o_ret, w_o[:rw_w], w_o[rw_w:], flat(x), g1, norm_post_mix[l][None],
                              norm_pre_ffn[l][None], sc2, sh2, seq_len, min(TM_OUT, seq_len))
    t = ffn_up(h, ffn_w_gate[l], ffn_w_up[l], ffn_conv[l].reshape(9, d_ff),
               ffn_conv_b[l][None], seq_len, min(TM_FFN, seq_len), TN_FFN)
    out = down_proj_residual(t, ffn_w_down[l].astype(BF16), x1, g2, norm_post_ffn[l][None], seq_len,
                             min(TM_DOWN, seq_len), TK_DOWN)
    return out.reshape(bsz, seq_len, d)
```

```python
import functools

import jax
import jax.numpy as jnp
import numpy as np
from jax import lax
from jax.experimental import pallas as pl
from jax.experimental.pallas import tpu as pltpu

F32 = jnp.float32
BF16 = jnp.bfloat16
HI = lax.Precision.HIGHEST

LANES = 128
EPS = 1e-6
GRID_W = 64
RW_HEAD_DIM = 64
DECAY_LORA = 64
AAA_LORA = 64
GATE_LORA = 160
LNX_EPS = 64e-5
RET_CHUNK = 128
ROPE_BASE = 10000.0
RW_CHUNK = 64
RW_SUB = 16
RW_PROBLEMS = 2
RW_CAT = RW_PROBLEMS * RW_CHUNK
VMEM_LIMIT = 56 * 1024 * 1024

TM_IN, TN_IN_RW, TN_IN_RET, TM_OUT, TM_FFN, TN_FFN, TM_DOWN, TK_DOWN = 1024, 512, 1024, 512, 1024, 512, 1024, 1408
TL_PREP, TL_FINISH = 256, 512


def _cparams(*sem):
    return pltpu.CompilerParams(dimension_semantics=sem, vmem_limit_bytes=VMEM_LIMIT)


def _dot(a, b, prec=None):
    return jnp.dot(a, b, precision=prec, preferred_element_type=F32)


def _dot_nt(a, b, prec=None):
    return lax.dot_general(a, b, (((1,), (1,)), ((), ())), precision=prec, preferred_element_type=F32)


def _dot_tn(a, b, prec=None):
    return lax.dot_general(a, b, (((0,), (0,)), ((), ())), precision=prec, preferred_element_type=F32)


def _bf16_terms(a, n):
    terms = []
    for _ in range(n):
        t = a.astype(BF16)
        terms.append(t)
        a = a - t.astype(F32)
    return terms


def _dot_sel(sel, x, left=True):
    hi, lo = _bf16_terms(x, 2)
    return (_dot(sel, hi) + _dot(sel, lo)) if left else (_dot(hi, sel) + _dot(lo, sel))


def _dot_x3(a, b_hi, b_lo):
    ah, al = _bf16_terms(a, 2)
    return _dot(ah, b_hi) + (_dot(ah, b_lo) + _dot(al, b_hi))


def _rms(x, g):
    return x * lax.rsqrt(jnp.mean(x * x, axis=-1, keepdims=True) + EPS) * g


def _each(f, *lists):
    return [f(*xs) for xs in zip(*lists)]


def _ada_kernel(s_ref, w_ref, b_ref, o_ref):
    s = s_ref[...]
    s = s * jax.nn.sigmoid(s)
    w_hi, w_lo = _bf16_terms(w_ref[...], 2)
    o_ref[...] = _dot_x3(s, w_hi, w_lo) + b_ref[...]


def ada_modulation(s, w, b, tn=1024):
    m, d = s.shape
    n = w.shape[1]
    return pl.pallas_call(
        _ada_kernel,
        grid=(n // tn,),
        in_specs=[pl.BlockSpec((m, d), lambda j: (0, 0)),
                  pl.BlockSpec((d, tn), lambda j: (0, j)),
                  pl.BlockSpec((1, tn), lambda j: (0, j))],
        out_specs=pl.BlockSpec((m, tn), lambda j: (0, j)),
        out_shape=jax.ShapeDtypeStruct((m, n), F32),
        compiler_params=_cparams("arbitrary"),
        name="ada_modulation",
    )(s, w, b)


def _norm_mm_kernel(x_ref, g_ref, sc_ref, sh_ref, w_ref, o_ref, xm_ref):
    j = pl.program_id(1)

    @pl.when(j == 0)
    def _():
        half = x_ref.shape[0] // 2
        for rows in (slice(0, half), slice(half, 2 * half)):
            y = _rms(x_ref[rows], g_ref[...])
            xm = (y * (1.0 + sc_ref[0]) + sh_ref[0]).astype(BF16)
            xm_ref[rows] = xm
            o_ref[rows] = _dot(xm, w_ref[...])

    @pl.when(j > 0)
    def _():
        o_ref[...] = _dot(xm_ref[...], w_ref[...])


def norm_mod_matmul(x, g, sc, sh, w, rows_per_batch, tm, tn):
    m, d = x.shape
    n = w.shape[1]
    bpb = rows_per_batch // tm
    return pl.pallas_call(
        _norm_mm_kernel,
        grid=(m // tm, n // tn),
        in_specs=[pl.BlockSpec((tm, d), lambda i, j: (i, 0)),
                  pl.BlockSpec((1, d), lambda i, j: (0, 0)),
                  pl.BlockSpec((1, 1, d), lambda i, j: (i // bpb, 0, 0)),
                  pl.BlockSpec((1, 1, d), lambda i, j: (i // bpb, 0, 0)),
                  pl.BlockSpec((d, tn), lambda i, j: (0, j))],
        out_specs=[pl.BlockSpec((tm, tn), lambda i, j: (i, j)), pl.BlockSpec((tm, d), lambda i, j: (i, 0))],
        out_shape=[jax.ShapeDtypeStruct((m, n), F32), jax.ShapeDtypeStruct((m, d), BF16)],
        compiler_params=_cparams("parallel", "arbitrary"),
        name="norm_mod_matmul",
    )(x, g, sc, sh, w)


def _mm_kernel(a_ref, w_ref, o_ref):
    o_ref[...] = _dot(a_ref[...], w_ref[...]).astype(o_ref.dtype)


def matmul(a, w, tm, tn, out_dtype):
    m, k = a.shape
    n = w.shape[1]
    return pl.pallas_call(
        _mm_kernel,
        grid=(m // tm, n // tn),
        in_specs=[pl.BlockSpec((tm, k), lambda i, j: (i, 0)),
                  pl.BlockSpec((k, tn), lambda i, j: (0, j))],
        out_specs=pl.BlockSpec((tm, tn), lambda i, j: (i, j)),
        out_shape=jax.ShapeDtypeStruct((m, n), out_dtype),
        compiler_params=_cparams("parallel", "arbitrary"),
        name="matmul",
    )(a, w)


def _out_proj_kernel(oa_ref, ob_ref, wa_ref, wb_ref, x_ref, g1_ref, gpost_ref, gpre_ref, sc2_ref, sh2_ref,
                     x1_ref, h_ref):
    out = _dot(oa_ref[...], wa_ref[...]) + _dot(ob_ref[...], wb_ref[...])
    x1 = x_ref[...] + g1_ref[0] * _rms(out, gpost_ref[...])
    x1_ref[...] = x1
    h_ref[...] = (_rms(x1, gpre_ref[...]) * (1.0 + sc2_ref[0]) + sh2_ref[0]).astype(BF16)


def out_proj_residual(oa, ob, wa, wb, x, g1, gpost, gpre, sc2, sh2, rows_per_batch, tm):
    m, ka = oa.shape
    kb = ob.shape[1]
    d = x.shape[1]
    bpb = rows_per_batch // tm
    row = lambda i: (i, 0)
    fixed = lambda i: (0, 0)
    per_b = lambda i: (i // bpb, 0, 0)
    return pl.pallas_call(
        _out_proj_kernel,
        grid=(m // tm,),
        in_specs=[pl.BlockSpec((tm, ka), row), pl.BlockSpec((tm, kb), row),
                  pl.BlockSpec((ka, d), fixed), pl.BlockSpec((kb, d), fixed),
                  pl.BlockSpec((tm, d), row),
                  pl.BlockSpec((1, 1, d), per_b),
                  pl.BlockSpec((1, d), fixed), pl.BlockSpec((1, d), fixed),
                  pl.BlockSpec((1, 1, d), per_b), pl.BlockSpec((1, 1, d), per_b)],
        out_specs=[pl.BlockSpec((tm, d), row), pl.BlockSpec((tm, d), row)],
        out_shape=[jax.ShapeDtypeStruct((m, d), F32), jax.ShapeDtypeStruct((m, d), BF16)],
        compiler_params=_cparams("parallel"),
        name="out_proj_residual",
    )(oa, ob, wa, wb, x, g1, gpost, gpre, sc2, sh2)


def _ffn_up_kernel(top_ref, mid_ref, bot_ref, wg_ref, wu_ref, cw_ref, cb_ref, o_ref, hext_ref, *, n_blocks):
    i = pl.program_id(1)
    tm = mid_ref.shape[0]

    @pl.when(pl.program_id(2) == 0)
    def _():
        hext_ref[0:GRID_W] = top_ref[...]
        hext_ref[GRID_W:GRID_W + tm] = mid_ref[...]
        hext_ref[GRID_W + tm:] = bot_ref[...]

    gate = _dot(hext_ref[...], wg_ref[...].astype(BF16))
    up = _dot(mid_ref[...], wu_ref[...].astype(BF16))
    tn = gate.shape[1]
    mid = gate[GRID_W:GRID_W + tm]
    above = jnp.where(i > 0, gate[:GRID_W], 0.0)
    below = jnp.where(i < n_blocks - 1, gate[GRID_W + tm:], 0.0)
    above = jnp.concatenate([above, mid[:tm - GRID_W]], axis=0)
    below = jnp.concatenate([mid[GRID_W:], below], axis=0)
    w = cw_ref[...]
    col = lax.broadcasted_iota(jnp.int32, (tm, tn), 0) % GRID_W

    def column_sum(dx):
        return above * w[dx:dx + 1] + mid * w[3 + dx:4 + dx] + below * w[6 + dx:7 + dx]

    left = jnp.where(col > 0, pltpu.roll(column_sum(0), 1, axis=0), 0.0)
    right = jnp.where(col < GRID_W - 1, pltpu.roll(column_sum(2), tm - 1, axis=0), 0.0)
    gt = column_sum(1) + left + right + cb_ref[...]
    o_ref[...] = (gt * jax.nn.sigmoid(gt) * up).astype(o_ref.dtype)


def ffn_up(h, wg, wu, w9, b, seq_len, tm, tn):
    m, d = h.shape
    f = wg.shape[1]
    nb = seq_len // tm
    bsz = m // seq_len
    hpb = tm // GRID_W
    n_halo = seq_len // GRID_W
    main = lambda b_, i, j: (b_ * nb + i, 0)
    top = lambda b_, i, j: (b_ * n_halo + jnp.maximum(i * hpb - 1, 0), 0)
    bot = lambda b_, i, j: (b_ * n_halo + jnp.minimum((i + 1) * hpb, n_halo - 1), 0)
    col = lambda b_, i, j: (0, j)
    return pl.pallas_call(
        functools.partial(_ffn_up_kernel, n_blocks=nb),
        grid=(bsz, nb, f // tn),
        in_specs=[pl.BlockSpec((GRID_W, d), top), pl.BlockSpec((tm, d), main), pl.BlockSpec((GRID_W, d), bot),
                  pl.BlockSpec((d, tn), col), pl.BlockSpec((d, tn), col),
                  pl.BlockSpec((9, tn), col), pl.BlockSpec((1, tn), col)],
        out_specs=pl.BlockSpec((tm, tn), lambda b_, i, j: (b_ * nb + i, j)),
        out_shape=jax.ShapeDtypeStruct((m, f), BF16),
        scratch_shapes=[pltpu.VMEM((tm + 2 * GRID_W, d), BF16)],
        compiler_params=_cparams("parallel", "parallel", "arbitrary"),
        name="ffn_up",
    )(h, h, h, wg, wu, w9, b)


def _down_proj_kernel(t_ref, w_ref, x1_ref, g2_ref, gpost_ref, o_ref):
    k = pl.program_id(1)
    last = pl.num_programs(1) - 1

    @pl.when(k == 0)
    def _():
        o_ref[...] = _dot(t_ref[...], w_ref[...])

    @pl.when((k > 0) & (k < last))
    def _():
        o_ref[...] += _dot(t_ref[...], w_ref[...])

    @pl.when(k == last)
    def _():
        half = o_ref.shape[0] // 2
        for rows in (slice(0, half), slice(half, 2 * half)):
            acc = o_ref[rows] + _dot(t_ref[rows], w_ref[...])
            o_ref[rows] = x1_ref[rows] + g2_ref[0] * _rms(acc, gpost_ref[...])


def down_proj_residual(t, w, x1, g2, gpost, rows_per_batch, tm, tk):
    m, kk = t.shape
    d = w.shape[1]
    bpb = rows_per_batch // tm
    return pl.pallas_call(
        _down_proj_kernel,
        grid=(m // tm, kk // tk),
        in_specs=[pl.BlockSpec((tm, tk), lambda i, k: (i, k)),
                  pl.BlockSpec((tk, d), lambda i, k: (k, 0)),
                  pl.BlockSpec((tm, d), lambda i, k: (i, 0)),
                  pl.BlockSpec((1, 1, d), lambda i, k: (i // bpb, 0, 0)),
                  pl.BlockSpec((1, d), lambda i, k: (0, 0))],
        out_specs=pl.BlockSpec((tm, d), lambda i, k: (i, 0)),
        out_shape=jax.ShapeDtypeStruct((m, d), F32),
        compiler_params=_cparams("parallel", "arbitrary"),
        name="down_proj_residual",
    )(t, w, x1, g2, gpost)


def _head_sum(x, e, et):
    return _dot_sel(et, _dot_sel(e, x, left=False), left=False)


def _rw_prep_kernel(prev_ref, cur_ref, next_ref, cw_ref, wup_ref, aup_ref, gup_ref, vec_ref, e_ref, et_ref,
                    af_ref, btf_ref, ktf_ref, rf_ref, wtf_ref, ab_ref, btb_ref, ktb_ref, rb_ref, wtb_ref,
                    v_ref, bon_ref, g_ref, *, n_blocks, width):
    i = pl.program_id(1)
    tl = cur_ref.shape[0]
    cur = cur_ref[...]
    rows = lax.broadcasted_iota(jnp.int32, cur.shape, 0)
    prev_row = jnp.where(i > 0, prev_ref[7:8, :], 0.0)
    next_row = jnp.where(i < n_blocks - 1, next_ref[0:1, :], 0.0)
    before = jnp.where(rows == 0, prev_row, pltpu.roll(cur, 1, axis=0))
    after = jnp.where(rows == tl - 1, next_row, pltpu.roll(cur, tl - 1, axis=0))
    cw = cw_ref[...]
    c = before * cw[0:1] + cur * cw[1:2] + after * cw[2:3]

    w_ = width
    k = c[:, :w_]
    v = c[:, w_:2 * w_]
    lo = c[:, 2 * w_:2 * w_ + 256]
    r = c[:, 2 * w_ + 256:3 * w_ + 256]
    gl = c[:, 3 * w_ + 256:3 * w_ + 512]
    vec = vec_ref[...]
    e = e_ref[...]
    et = et_ref[...]
    k_k, k_a, r_k = vec[4:5], vec[5:6], vec[6:7]

    kk = k * k_k
    kk = kk * lax.rsqrt(_head_sum(kk * kk, e, et) + 1e-12)
    v_ref[...] = v.astype(BF16)
    tlo = jnp.tanh(lo)
    kd_sum = jnp.zeros_like(k)
    ti = lax.broadcasted_iota(jnp.int32, (tl, tl), 0)
    tj = lax.broadcasted_iota(jnp.int32, (tl, tl), 1)
    same_chunk = ti // RW_CHUNK == tj // RW_CHUNK
    wi = lax.broadcasted_iota(jnp.int32, (tl // 8, tl), 0)
    wj = lax.broadcasted_iota(jnp.int32, (tl // 8, tl), 1)
    chunk_rows = (wi * 8) // RW_CHUNK == wj // RW_CHUNK
    dirs = ((af_ref, btf_ref, ktf_ref, rf_ref, wtf_ref, tj <= ti), (ab_ref, btb_ref, ktb_ref, rb_ref, wtb_ref, tj >= ti))
    for d, (a_ref, bt_ref, kt_ref, r_ref, wt_ref, done) in enumerate(dirs):
        z = vec[d:d + 1] + _dot_x3(tlo, wup_ref[0, d], wup_ref[1, d])
        nz = -z
        softplus = jnp.maximum(nz, 0.0) + jnp.log1p(jnp.exp(-jnp.abs(nz)))
        w_log = -softplus - 0.5
        lw = -jnp.exp(w_log)
        a = jax.nn.sigmoid(vec[2 + d:3 + d] + _dot_x3(lo, aup_ref[0, d], aup_ref[1, d]))
        kd = k * (1.0 + (a - 1.0) * k_a)
        kd_sum = kd_sum + kd
        sel = jnp.concatenate([(same_chunk & done).astype(F32), chunk_rows.astype(F32)], axis=0).astype(BF16)
        sums = _dot_sel(sel, lw)
        cum = sums[:tl]
        w_inv = jnp.exp(-cum)
        a_ref[...] = (-kk * jnp.exp(cum - lw)).astype(BF16)
        bt_ref[...] = (kk * a * w_inv).astype(BF16)
        kt_ref[...] = (kd * w_inv).astype(BF16)
        r_ref[...] = (r * jnp.exp(cum)).astype(BF16)
        wt_ref[...] = jnp.exp(sums[tl:])
    bon_ref[...] = _head_sum(r * kd_sum * r_k, e, et) * v
    g_ref[...] = _dot_x3(jax.nn.sigmoid(gl), gup_ref[0], gup_ref[1])


def rw_prep(p, cw, wup, aup, gup, vec, e, et, seq_len, tl, width):
    bsz, _, rw_cols = p.shape
    nb = seq_len // tl
    hb = tl // 8
    n_halo = seq_len // 8
    fixed = lambda a: pl.BlockSpec(a.shape, lambda b_, i: (0,) * a.ndim)
    row_spec = pl.BlockSpec((None, tl, width), lambda b_, i: (b_, i, 0))
    wt_spec = pl.BlockSpec((None, tl // 8, width), lambda b_, i: (b_, i, 0))
    act = lambda dt: jax.ShapeDtypeStruct((bsz, seq_len, width), dt)
    wt = jax.ShapeDtypeStruct((bsz, seq_len // 8, width), F32)
    per_dir_specs = [row_spec] * 4 + [wt_spec]
    per_dir_shapes = [act(BF16)] * 4 + [wt]
    return pl.pallas_call(
        functools.partial(_rw_prep_kernel, n_blocks=nb, width=width),
        grid=(bsz, nb),
        in_specs=[pl.BlockSpec((None, 8, rw_cols), lambda b_, i: (b_, jnp.maximum(i * hb - 1, 0), 0)),
                  pl.BlockSpec((None, tl, rw_cols), lambda b_, i: (b_, i, 0)),
                  pl.BlockSpec((None, 8, rw_cols), lambda b_, i: (b_, jnp.minimum((i + 1) * hb, n_halo - 1), 0)),
                  fixed(cw), fixed(wup), fixed(aup), fixed(gup), fixed(vec), fixed(e), fixed(et)],
        out_specs=per_dir_specs * 2 + [row_spec] * 3,
        out_shape=per_dir_shapes * 2 + [act(BF16), act(F32), act(F32)],
        compiler_params=_cparams("parallel", "parallel"),
        name="rw_prep",
    )(p, p, p, cw, wup, aup, gup, vec, e, et)


def _block_diag(y, diag):
    yb = y.astype(BF16)
    tiled = jnp.concatenate([yb] * RW_PROBLEMS, axis=0)
    return jnp.where(diag, tiled, jnp.zeros_like(tiled))


def _unit_tri_inverse(a, eye, blk, off1, off2, diag):
    n = RW_CHUNK
    mm = lambda xs, ys: _each(lambda x, y: _dot(x.astype(BF16), _block_diag(y, diag)), xs, ys)
    add = lambda xs, ys: _each(jnp.add, xs, ys)
    stack = lambda xs, ys: _each(lambda x, y: jnp.concatenate([x, y], axis=0), xs, ys)
    d = [ai * blk for ai in a]
    x = [eye + di for di in d]
    d2 = mm(d, d)
    both = mm(stack(x, d2), d2)
    x = _each(lambda xi, p: xi + p[:n], x, both)
    d4 = [p[n:] for p in both]
    both = mm(stack(x, d4), d4)
    x = _each(lambda xi, p: xi + p[:n], x, both)
    d8 = [p[n:] for p in both]
    x = add(x, mm(x, d8))
    x = add(x, mm(mm(x, [ai * off1 for ai in a]), x))
    x = add(x, mm(mm(x, [ai * off2 for ai in a]), x))
    return x


def _rw_chunk(a, bt, kt, r, v, wtot, s, masks):
    strict, incl, eye, blk, off1, off2, diag = masks
    n = RW_CHUNK
    bf = lambda xs: [x.astype(BF16) for x in xs]
    bdiag = lambda xs: [_block_diag(x, diag) for x in xs]
    stack = lambda xs, ys: _each(lambda x, y: jnp.concatenate([x, y], axis=0), xs, ys)
    ar = stack(a, r)
    sc = _each(_dot_nt, ar, stack(bdiag(bt), bdiag(kt)))
    sc_s = _each(_dot_nt, ar, bf(s))
    a_ab = _each(lambda x, m: jnp.where(m, x[:n, :RW_CAT], 0.0), sc, strict)
    a_ak = _each(lambda x, m: jnp.where(m, x[:n, RW_CAT:], 0.0), sc, strict)
    r_rb = _each(lambda x, m: jnp.where(m, x[n:, :RW_CAT], 0.0), sc, incl)
    r_rk = _each(lambda x, m: jnp.where(m, x[n:, RW_CAT:], 0.0), sc, incl)
    inv = _unit_tri_inverse(a_ab, eye, blk, off1, off2, diag)
    kv = _each(_dot, bf(stack(a_ak, r_rk)), bdiag(v))
    rhs = _each(lambda x, y: x[:n] + y[:n], sc_s, kv)
    u = _each(_dot, bf(inv), bdiag(rhs))
    y = _each(lambda x, p, q: x[n:] + p + q[n:], sc_s, _each(_dot, bf(r_rb), bdiag(u)), kv)
    to_end = lambda x, w: (x.astype(F32) * w).astype(BF16)
    bk = _each(lambda x, z, w: jnp.concatenate([to_end(x, w), to_end(z, w)], axis=0), bt, kt, wtot)
    upd = _each(_dot_tn, stack(bf(u), v), bk)
    return y, _each(lambda si, w, x: si * w + jnp.where(diag, x, 0.0), s, wtot, upd)


def _rw_scan_kernel(*refs):
    fwd, bwd = refs[0:6], refs[6:12]
    s0_ref, yf_ref, yb_ref, sout_ref, s_ref = refs[12:]
    c = pl.program_id(1)

    @pl.when(c == 0)
    def _():
        s_ref[...] = s0_ref[...]

    n = RW_CHUNK
    t = lax.broadcasted_iota(jnp.int32, (n, RW_CAT), 0)
    lane = lax.broadcasted_iota(jnp.int32, (n, RW_CAT), 1)
    step = lane % n
    eye = (t == step).astype(F32)
    blk = (t // RW_SUB == step // RW_SUB).astype(F32)
    off1 = ((t // (2 * RW_SUB) == step // (2 * RW_SUB)) & (t // RW_SUB != step // RW_SUB)).astype(F32)
    off2 = (t // (2 * RW_SUB) != step // (2 * RW_SUB)).astype(F32)
    di = lax.broadcasted_iota(jnp.int32, (RW_CAT, RW_CAT), 0) // n
    dj = lax.broadcasted_iota(jnp.int32, (RW_CAT, RW_CAT), 1) // n
    pairs = s_ref.shape[1]
    lanes = [slice(g * LANES, (g + 1) * LANES) for g in range(pairs)]
    strict = [t > step] * pairs + [t < step] * pairs
    incl = [t >= step] * pairs + [t <= step] * pairs
    masks = (strict, incl, eye, blk, off1, off2, di == dj)
    ops = [[ref[:, ln] for ref in (f, b) for ln in lanes] for f, b in zip(fwd[:5], bwd[:5])]
    wtot = [ref[0:1, ln] for ref in (fwd[5], bwd[5]) for ln in lanes]
    states = [s_ref[d, g] for d in range(2) for g in range(pairs)]
    ys, s_new = _rw_chunk(*ops, wtot, states, masks)
    for d, y_ref in enumerate((yf_ref, yb_ref)):
        for g in range(pairs):
            y_ref[:, lanes[g]] = ys[d * pairs + g]
            s_ref[d, g] = s_new[d * pairs + g]

    @pl.when(c == pl.num_programs(1) - 1)
    def _():
        sout_ref[...] = s_ref[...]


def rw_scan(fwd, bwd, s0):
    bsz, seq_len, width = fwd[0].shape
    nc = seq_len // RW_CHUNK
    pairs = width // LANES
    f_spec = pl.BlockSpec((None, RW_CHUNK, width), lambda b_, c: (b_, c, 0))
    b_spec = pl.BlockSpec((None, RW_CHUNK, width), lambda b_, c: (b_, nc - 1 - c, 0))
    fw_spec = pl.BlockSpec((None, 8, width), lambda b_, c: (b_, c, 0))
    bw_spec = pl.BlockSpec((None, 8, width), lambda b_, c: (b_, nc - 1 - c, 0))
    s_spec = pl.BlockSpec((None, 2, pairs, RW_CAT, RW_CAT), lambda b_, c: (b_, 0, 0, 0, 0))
    y_shape = jax.ShapeDtypeStruct((bsz, seq_len, width), F32)
    return pl.pallas_call(
        _rw_scan_kernel,
        grid=(bsz, nc),
        in_specs=[f_spec] * 5 + [fw_spec] + [b_spec] * 5 + [bw_spec] + [s_spec],
        out_specs=[f_spec, b_spec, s_spec],
        out_shape=[y_shape, y_shape, jax.ShapeDtypeStruct(s0.shape, F32)],
        scratch_shapes=[pltpu.VMEM((2, pairs, RW_CAT, RW_CAT), F32)],
        compiler_params=_cparams("parallel", "arbitrary"),
        name="rw_scan",
    )(*fwd, *bwd, s0)


def _rw_finish_kernel(yf_ref, yb_ref, bon_ref, g_ref, lnw_ref, lnb_ref, e_ref, et_ref, o_ref):
    y = yf_ref[...] + yb_ref[...]
    e = e_ref[...]
    et = et_ref[...]
    mu = _head_sum(y, e, et) * (1.0 / RW_HEAD_DIM)
    yc = y - mu
    var = _head_sum(yc * yc, e, et) * (1.0 / RW_HEAD_DIM)
    yn = yc * lax.rsqrt(var + LNX_EPS) * lnw_ref[...] + lnb_ref[...]
    o_ref[...] = ((yn + bon_ref[...]) * g_ref[...]).astype(o_ref.dtype)


def rw_finish(yf, yb, bon, g, lnw, lnb, e, et, tl):
    m, width = yf.shape
    row = pl.BlockSpec((tl, width), lambda i: (i, 0))
    fixed = lambda a: pl.BlockSpec(a.shape, lambda i: (0, 0))
    return pl.pallas_call(
        _rw_finish_kernel,
        grid=(m // tl,),
        in_specs=[row, row, row, row, fixed(lnw), fixed(lnb), fixed(e), fixed(et)],
        out_specs=row,
        out_shape=jax.ShapeDtypeStruct((m, width), BF16),
        compiler_params=_cparams("parallel"),
        name="rw_finish",
    )(yf, yb, bon, g, lnw, lnb, e, et)


def _rope(x, cos, sin):
    quarter = LANES // 4
    width = x.shape[1]
    lane = lax.broadcasted_iota(jnp.int32, x.shape, 1)
    first = (lane // quarter) % 2 == 0
    partner = jnp.where(first, pltpu.roll(x, width - quarter, axis=1), pltpu.roll(x, quarter, axis=1))
    return x * cos + partner * sin


def _ret_scan_kernel(kf_ref, vf_ref, qf_ref, cf_ref, sf_ref, kb_ref, vb_ref, qb_ref, cb_ref, sb_ref,
                     dec_ref, fs_ref, te_ref, gc_ref, s0_ref, yf_ref, yb_ref, sout_ref, s_ref, *, rope, scale):
    c = pl.program_id(1)

    @pl.when(c == 0)
    def _():
        s_ref[...] = s0_ref[...]

    n_heads = s_ref.shape[1]
    heads = [slice(h * LANES, (h + 1) * LANES) for h in range(n_heads)]
    qs, ks, vs, decay, ss = [], [], [], [], []
    for d, (k_ref, v_ref, q_ref, cos_ref, sin_ref) in enumerate(
            ((kf_ref, vf_ref, qf_ref, cf_ref, sf_ref), (kb_ref, vb_ref, qb_ref, cb_ref, sb_ref))):
        q = q_ref[...]
        k = k_ref[...]
        if rope:
            cos = jnp.concatenate([cos_ref[...]] * n_heads, axis=1)
            sin = jnp.concatenate([sin_ref[...]] * n_heads, axis=1)
            q = _rope(q, cos, sin)
            k = _rope(k, cos, sin)
        k = k * scale
        v = v_ref[...].astype(BF16)
        q_cross = (q * fs_ref[d]).astype(BF16)
        k_end = (k * te_ref[d]).astype(BF16)
        q = q.astype(BF16)
        k = k.astype(BF16)
        for h, hs in enumerate(heads):
            qs.append((q[:, hs], q_cross[:, hs]))
            ks.append((k[:, hs], k_end[:, hs]))
            vs.append(v[:, hs])
            decay.append(dec_ref[d, h])
            ss.append(s_ref[d, h])
    scores = _each(lambda q, k, dm: (_dot_nt(q[0], k[0]) * dm).astype(BF16), qs, ks, decay)
    cross = _each(lambda q, s: _dot(q[1], s.astype(BF16)), qs, ss)
    intra = _each(_dot, scores, vs)
    upd = _each(lambda k, v: _dot_tn(k[1], v), ks, vs)
    for d, y_ref in enumerate((yf_ref, yb_ref)):
        for h, hs in enumerate(heads):
            i = d * n_heads + h
            y_ref[:, hs] = intra[i] + cross[i]
            s_ref[d, h] = gc_ref[h] * ss[i] + upd[i]

    @pl.when(c == pl.num_programs(1) - 1)
    def _():
        sout_ref[...] = s_ref[...]


def ret_scan(p, cos, sin, tables, s0, rope):
    bsz, seq_len, cols = p.shape
    width = cols // 4
    nc = seq_len // RET_CHUNK
    dec, fs, te, gc = tables

    def specs(chunk):
        cols_ = [pl.BlockSpec((None, RET_CHUNK, width), lambda b_, c, o=o: (b_, chunk(c), o)) for o in range(3)]
        return cols_ + [pl.BlockSpec((RET_CHUNK, LANES), lambda b_, c: (chunk(c), 0))] * 2

    fwd = lambda c: c
    bwd = lambda c: nc - 1 - c
    fixed = lambda a: pl.BlockSpec(a.shape, lambda b_, c: (0,) * a.ndim)
    s_spec = pl.BlockSpec((None,) + s0.shape[1:], lambda b_, c: (b_, 0, 0, 0, 0))
    y_shape = jax.ShapeDtypeStruct((bsz, seq_len, width), F32)
    return pl.pallas_call(
        functools.partial(_ret_scan_kernel, rope=rope, scale=float(LANES) ** -0.5),
        grid=(bsz, nc),
        in_specs=specs(fwd) + specs(bwd) + [fixed(dec), fixed(fs), fixed(te), fixed(gc), s_spec],
        out_specs=[pl.BlockSpec((None, RET_CHUNK, width), lambda b_, c: (b_, c, 0)),
                   pl.BlockSpec((None, RET_CHUNK, width), lambda b_, c: (b_, nc - 1 - c, 0)),
                   s_spec],
        out_shape=[y_shape, y_shape, jax.ShapeDtypeStruct(s0.shape, F32)],
        scratch_shapes=[pltpu.VMEM(s0.shape[1:], F32)],
        compiler_params=_cparams("parallel", "arbitrary"),
        name="ret_scan",
    )(p, p, p, cos, sin, p, p, p, cos, sin, dec, fs, te, gc, s0)


def _ret_tables(n_heads):
    n = RET_CHUNK
    lg = np.log(1.0 - 2.0 ** (-5.0 - np.arange(n_heads, dtype=np.float64)))
    pos = np.arange(n, dtype=np.float64)
    diff = pos[:, None] - pos[None, :]
    masks = (diff >= 0, diff < 0)
    dists = (diff, -diff)
    dec = np.stack([np.where(m, np.exp(np.where(m, dd, 0.0)[None] * lg[:, None, None]), 0.0)
                    for m, dd in zip(masks, dists)])
    done = np.stack([pos, n - 1.0 - pos])
    lanes = np.repeat(lg, LANES)[None, None, :]
    fs = np.exp((done[:, :, None] + 1.0) * lanes)
    te = np.exp((n - 1.0 - done[:, :, None]) * lanes)
    gc = np.broadcast_to(np.exp(n * lg)[:, None, None], (n_heads, 1, LANES))
    return tuple(jnp.asarray(a, F32) for a in (dec, fs, te, gc))


def _ret_finish_kernel(yf_ref, yb_ref, g_ref, o_ref):
    g = g_ref[...]
    gate = g * jax.nn.sigmoid(g)
    for h in range(o_ref.shape[1] // LANES):
        hs = slice(h * LANES, (h + 1) * LANES)
        y = yf_ref[:, hs] + yb_ref[:, hs]
        y = y * lax.rsqrt(jnp.mean(y * y, axis=-1, keepdims=True) + EPS)
        o_ref[:, hs] = (y * gate[:, hs]).astype(o_ref.dtype)


def ret_finish(yf, yb, p2, tl):
    m, width = yf.shape
    row = pl.BlockSpec((tl, width), lambda i: (i, 0))
    return pl.pallas_call(
        _ret_finish_kernel,
        grid=(m // tl,),
        in_specs=[row, row, pl.BlockSpec((tl, width), lambda i: (i, 3))],
        out_specs=row,
        out_shape=jax.ShapeDtypeStruct((m, width), BF16),
        compiler_params=_cparams("parallel"),
        name="ret_finish",
    )(yf, yb, p2)


def _rope_tables(seq_len):
    n = LANES // 4
    t = np.arange(seq_len)
    inv = ROPE_BASE ** (-np.arange(n, dtype=np.float64) / n)
    ang_row = (t // GRID_W)[:, None] * inv
    ang_col = (t % GRID_W)[:, None] * inv
    cr, sr, cc, sc = np.cos(ang_row), np.sin(ang_row), np.cos(ang_col), np.sin(ang_col)
    cos = np.concatenate([cr, cr, cc, cc], axis=-1)
    sin = np.concatenate([-sr, sr, -sc, sc], axis=-1)
    return jnp.asarray(cos, F32), jnp.asarray(sin, F32)


def _pad_rows(w, rows, at):
    return jnp.zeros((rows, w.shape[1]), w.dtype).at[at:at + w.shape[0]].set(w)


def kernel(x, c, ctx, c_ctx, w_ada, b_ada, norm_pre_mix, norm_post_mix, norm_pre_ffn, norm_post_ffn, w_in, rw_conv, rw_w0, rw_w_up, rw_a0, rw_a_up, rw_g_up, rw_k_k, rw_k_a, rw_r_k, rw_lnx_w, rw_lnx_b, w_out, ffn_w_gate, ffn_w_up, ffn_conv, ffn_conv_b, ffn_w_down):
    bsz, seq_len, d = x.shape
    ctx_len = ctx.shape[1]
    n_layers = w_ada.shape[0]
    assert n_layers == 1, "context-stream outputs are only needed between layers"
    rw_w = rw_k_k.shape[1]
    ret_w = w_out.shape[1] - rw_w
    rw_cols = 3 * rw_w + 2 * DECAY_LORA + 2 * AAA_LORA + GATE_LORA
    rw_pad = -rw_cols % (2 * LANES)
    d_ff = ffn_w_gate.shape[2]
    m = bsz * seq_len
    flat = lambda a: a.reshape(-1, a.shape[-1])
    l = 0

    cond = jnp.zeros((8, d), F32).at[:bsz].set(c).at[bsz].set(c_ctx)
    mod = ada_modulation(cond, w_ada[l], b_ada[l][None])
    sh1, sc1, g1, sh2, sc2, g2 = [mm[:bsz, None, :] for mm in jnp.split(mod, 6, axis=-1)]
    sh_c = jnp.broadcast_to(mod[bsz, :d], (bsz, 1, d))
    sc_c = jnp.broadcast_to(mod[bsz, d:2 * d], (bsz, 1, d))

    w_in_b = w_in[l].astype(BF16)
    w_rw = jnp.pad(w_in_b[:, :rw_cols], ((0, 0), (0, rw_pad)))
    w_ret = w_in_b[:, rw_cols:]
    g_pre = norm_pre_mix[l][None]

    def in_proj(a, sc, sh, rows, tm):
        p_a, xm = norm_mod_matmul(flat(a), g_pre, sc, sh, w_rw, rows, tm, TN_IN_RW)
        p_b = matmul(xm, w_ret, tm, TN_IN_RET, F32)
        return p_a.reshape(bsz, rows, -1), p_b.reshape(bsz, rows, -1)

    p_rw, p_ret = in_proj(x, sc1, sh1, seq_len, min(TM_IN, seq_len))
    pc_rw, pc_ret = in_proj(ctx, sc_c, sh_c, ctx_len, ctx_len)

    cw = jnp.pad(rw_conv[l], ((0, 0), (0, rw_pad)))
    lora_rows = 2 * DECAY_LORA + 2 * AAA_LORA
    split = lambda w: jnp.stack(_bf16_terms(w, 2))
    wup = split(jnp.stack([_pad_rows(rw_w_up[l][dd], lora_rows, dd * DECAY_LORA) for dd in range(2)]))
    aup = split(jnp.stack([_pad_rows(rw_a_up[l][dd], lora_rows, 2 * DECAY_LORA + dd * AAA_LORA) for dd in range(2)]))
    gup = split(_pad_rows(rw_g_up[l], 256, 0))
    vec = jnp.concatenate([rw_w0[l], rw_a0[l], rw_k_k[l][None], rw_k_a[l][None], rw_r_k[l].reshape(1, rw_w)], axis=0)
    vec = jnp.pad(vec, ((0, 1), (0, 0)))
    head_of_lane = jnp.arange(rw_w) // RW_HEAD_DIM
    e = (head_of_lane[:, None] == jnp.arange(LANES)[None, :]).astype(BF16)
    et = e.T
    prep = functools.partial(rw_prep, cw=cw, wup=wup, aup=aup, gup=gup, vec=vec, e=e, et=et, width=rw_w)
    scan_args = lambda o: (o[0:4] + [o[10], o[4]], o[5:9] + [o[10], o[9]])
    s0 = jnp.zeros((bsz, 2, rw_w // LANES, RW_CAT, RW_CAT), F32)
    outs = list(prep(pc_rw, seq_len=ctx_len, tl=min(TL_PREP, ctx_len)))
    _, _, s_rw = rw_scan(*scan_args(outs), s0)
    outs = list(prep(p_rw, seq_len=seq_len, tl=TL_PREP))
    yf, yb, _ = rw_scan(*scan_args(outs), s_rw)
    o_rw = rw_finish(flat(yf), flat(yb), flat(outs[11]), flat(outs[12]), rw_lnx_w[l][None], rw_lnx_b[l][None],
                     e, et, TL_FINISH)

    n_ret_heads = ret_w // LANES
    tables = _ret_tables(n_ret_heads)
    cos, sin = _rope_tables(seq_len)
    s0 = jnp.zeros((bsz, 2, n_ret_heads, LANES, LANES), F32)
    _, _, s_ret = ret_scan(pc_ret, cos[:ctx_len], sin[:ctx_len], tables, s0, rope=False)
    yf, yb, _ = ret_scan(p_ret, cos, sin, tables, s_ret, rope=True)
    o_ret = ret_finish(flat(yf), flat(yb), flat(p_ret), TL_FINISH)

    w_o = w_out[l].astype(BF16)
    x1, h = out_proj_residual(o_rw, o_ret, w_o[:rw_w], w_o[rw_w:], flat(x), g1, norm_post_mix[l][None],
                              norm_pre_ffn[l][None], sc2, sh2, seq_len, min(TM_OUT, seq_len))
    t = ffn_up(h, ffn_w_gate[l], ffn_w_up[l], ffn_conv[l].reshape(9, d_ff),
               ffn_conv_b[l][None], seq_len, min(TM_FFN, seq_len), TN_FFN)
    out = down_proj_residual(t, ffn_w_down[l].astype(BF16), x1, g2, norm_post_ffn[l][None], seq_len,
                             min(TM_DOWN, seq_len), TK_DOWN)
    return out.reshape(bsz, seq_len, d)
```

```python
import functools

import jax
import jax.numpy as jnp
import numpy as np
from jax import lax
from jax.experimental import pallas as pl
from jax.experimental.pallas import tpu as pltpu

F32 = jnp.float32
BF16 = jnp.bfloat16
HI = lax.Precision.HIGHEST

LANES = 128
EPS = 1e-6
GRID_W = 64
RW_HEAD_DIM = 64
DECAY_LORA = 64
AAA_LORA = 64
GATE_LORA = 160
LNX_EPS = 64e-5
RET_CHUNK = 128
ROPE_BASE = 10000.0
RW_CHUNK = 64
RW_SUB = 16
RW_PROBLEMS = 2
RW_CAT = RW_PROBLEMS * RW_CHUNK
VMEM_LIMIT = 56 * 1024 * 1024

TM_IN, TN_IN_RW, TN_IN_RET, TM_OUT, TM_FFN, TN_FFN, TM_DOWN, TK_DOWN = 1024, 512, 1024, 512, 1024, 512, 1024, 1408
TL_PREP, TL_FINISH = 256, 512


def _cparams(*sem):
    return pltpu.CompilerParams(dimension_semantics=sem, vmem_limit_bytes=VMEM_LIMIT)


def _dot(a, b, prec=None):
    return jnp.dot(a, b, precision=prec, preferred_element_type=F32)


def _dot_nt(a, b, prec=None):
    return lax.dot_general(a, b, (((1,), (1,)), ((), ())), precision=prec, preferred_element_type=F32)


def _dot_tn(a, b, prec=None):
    return lax.dot_general(a, b, (((0,), (0,)), ((), ())), precision=prec, preferred_element_type=F32)


def _bf16_terms(a, n):
    terms = []
    for _ in range(n):
        t = a.astype(BF16)
        terms.append(t)
        a = a - t.astype(F32)
    return terms


def _dot_sel(sel, x, left=True):
    hi, lo = _bf16_terms(x, 2)
    return (_dot(sel, hi) + _dot(sel, lo)) if left else (_dot(hi, sel) + _dot(lo, sel))


def _dot_x3(a, b_hi, b_lo):
    ah, al = _bf16_terms(a, 2)
    return _dot(ah, b_hi) + (_dot(ah, b_lo) + _dot(al, b_hi))


def _rms(x, g):
    return x * lax.rsqrt(jnp.mean(x * x, axis=-1, keepdims=True) + EPS) * g


def _each(f, *lists):
    return [f(*xs) for xs in zip(*lists)]


def _ada_kernel(s_ref, w_ref, b_ref, o_ref):
    s = s_ref[...]
    s = s * jax.nn.sigmoid(s)
    w_hi, w_lo = _bf16_terms(w_ref[...], 2)
    o_ref[...] = _dot_x3(s, w_hi, w_lo) + b_ref[...]


def ada_modulation(s, w, b, tn=1024):
    m, d = s.shape
    n = w.shape[1]
    return pl.pallas_call(
        _ada_kernel,
        grid=(n // tn,),
        in_specs=[pl.BlockSpec((m, d), lambda j: (0, 0)),
                  pl.BlockSpec((d, tn), lambda j: (0, j)),
                  pl.BlockSpec((1, tn), lambda j: (0, j))],
        out_specs=pl.BlockSpec((m, tn), lambda j: (0, j)),
        out_shape=jax.ShapeDtypeStruct((m, n), F32),
        compiler_params=_cparams("arbitrary"),
        name="ada_modulation",
    )(s, w, b)


def _norm_mm_kernel(x_ref, g_ref, sc_ref, sh_ref, w_ref, o_ref, xm_ref):
    j = pl.program_id(1)
    w = w_ref[...].astype(BF16)

    @pl.when(j == 0)
    def _():
        half = x_ref.shape[0] // 2
        for rows in (slice(0, half), slice(half, 2 * half)):
            y = _rms(x_ref[rows], g_ref[...])
            xm = (y * (1.0 + sc_ref[0]) + sh_ref[0]).astype(BF16)
            xm_ref[rows] = xm
            o_ref[rows] = _dot(xm, w)

    @pl.when(j > 0)
    def _():
        o_ref[...] = _dot(xm_ref[...], w)


def norm_mod_matmul(x, g, sc, sh, w, n, rows_per_batch, tm, tn):
    m, d = x.shape
    bpb = rows_per_batch // tm
    return pl.pallas_call(
        _norm_mm_kernel,
        grid=(m // tm, n // tn),
        in_specs=[pl.BlockSpec((tm, d), lambda i, j: (i, 0)),
                  pl.BlockSpec((1, d), lambda i, j: (0, 0)),
                  pl.BlockSpec((1, 1, d), lambda i, j: (i // bpb, 0, 0)),
                  pl.BlockSpec((1, 1, d), lambda i, j: (i // bpb, 0, 0)),
                  pl.BlockSpec((d, tn), lambda i, j: (0, j))],
        out_specs=[pl.BlockSpec((tm, tn), lambda i, j: (i, j)), pl.BlockSpec((tm, d), lambda i, j: (i, 0))],
        out_shape=[jax.ShapeDtypeStruct((m, n), F32), jax.ShapeDtypeStruct((m, d), BF16)],
        compiler_params=_cparams("parallel", "arbitrary"),
        name="norm_mod_matmul",
    )(x, g, sc, sh, w)


def _mm_kernel(a_ref, w_ref, o_ref):
    o_ref[...] = _dot(a_ref[...], w_ref[...]).astype(o_ref.dtype)


def matmul(a, w, tm, tn, out_dtype):
    m, k = a.shape
    n = w.shape[1]
    return pl.pallas_call(
        _mm_kernel,
        grid=(m // tm, n // tn),
        in_specs=[pl.BlockSpec((tm, k), lambda i, j: (i, 0)),
                  pl.BlockSpec((k, tn), lambda i, j: (0, j))],
        out_specs=pl.BlockSpec((tm, tn), lambda i, j: (i, j)),
        out_shape=jax.ShapeDtypeStruct((m, n), out_dtype),
        compiler_params=_cparams("parallel", "arbitrary"),
        name="matmul",
    )(a, w)


def _out_proj_kernel(oa_ref, ob_ref, wa_ref, wb_ref, x_ref, g1_ref, gpost_ref, gpre_ref, sc2_ref, sh2_ref,
                     x1_ref, h_ref):
    out = _dot(oa_ref[...], wa_ref[...]) + _dot(ob_ref[...], wb_ref[...])
    x1 = x_ref[...] + g1_ref[0] * _rms(out, gpost_ref[...])
    x1_ref[...] = x1
    h_ref[...] = (_rms(x1, gpre_ref[...]) * (1.0 + sc2_ref[0]) + sh2_ref[0]).astype(BF16)


def out_proj_residual(oa, ob, wa, wb, x, g1, gpost, gpre, sc2, sh2, rows_per_batch, tm):
    m, ka = oa.shape
    kb = ob.shape[1]
    d = x.shape[1]
    bpb = rows_per_batch // tm
    row = lambda i: (i, 0)
    fixed = lambda i: (0, 0)
    per_b = lambda i: (i // bpb, 0, 0)
    return pl.pallas_call(
        _out_proj_kernel,
        grid=(m // tm,),
        in_specs=[pl.BlockSpec((tm, ka), row), pl.BlockSpec((tm, kb), row),
                  pl.BlockSpec((ka, d), fixed), pl.BlockSpec((kb, d), fixed),
                  pl.BlockSpec((tm, d), row),
                  pl.BlockSpec((1, 1, d), per_b),
                  pl.BlockSpec((1, d), fixed), pl.BlockSpec((1, d), fixed),
                  pl.BlockSpec((1, 1, d), per_b), pl.BlockSpec((1, 1, d), per_b)],
        out_specs=[pl.BlockSpec((tm, d), row), pl.BlockSpec((tm, d), row)],
        out_shape=[jax.ShapeDtypeStruct((m, d), F32), jax.ShapeDtypeStruct((m, d), BF16)],
        compiler_params=_cparams("parallel"),
        name="out_proj_residual",
    )(oa, ob, wa, wb, x, g1, gpost, gpre, sc2, sh2)


def _ffn_up_kernel(top_ref, mid_ref, bot_ref, wg_ref, wu_ref, cw_ref, cb_ref, o_ref, hext_ref, *, n_blocks):
    i = pl.program_id(1)
    tm = mid_ref.shape[0]

    @pl.when(pl.program_id(2) == 0)
    def _():
        hext_ref[0:GRID_W] = top_ref[...]
        hext_ref[GRID_W:GRID_W + tm] = mid_ref[...]
        hext_ref[GRID_W + tm:] = bot_ref[...]

    gate = _dot(hext_ref[...], wg_ref[...].astype(BF16))
    up = _dot(mid_ref[...], wu_ref[...].astype(BF16))
    tn = gate.shape[1]
    mid = gate[GRID_W:GRID_W + tm]
    above = jnp.where(i > 0, gate[:GRID_W], 0.0)
    below = jnp.where(i < n_blocks - 1, gate[GRID_W + tm:], 0.0)
    above = jnp.concatenate([above, mid[:tm - GRID_W]], axis=0)
    below = jnp.concatenate([mid[GRID_W:], below], axis=0)
    w = cw_ref[...]
    col = lax.broadcasted_iota(jnp.int32, (tm, tn), 0) % GRID_W

    def column_sum(dx):
        return above * w[dx:dx + 1] + mid * w[3 + dx:4 + dx] + below * w[6 + dx:7 + dx]

    left = jnp.where(col > 0, pltpu.roll(column_sum(0), 1, axis=0), 0.0)
    right = jnp.where(col < GRID_W - 1, pltpu.roll(column_sum(2), tm - 1, axis=0), 0.0)
    gt = column_sum(1) + left + right + cb_ref[...]
    o_ref[...] = (gt * jax.nn.sigmoid(gt) * up).astype(o_ref.dtype)


def ffn_up(h, wg, wu, w9, b, seq_len, tm, tn):
    m, d = h.shape
    f = wg.shape[1]
    nb = seq_len // tm
    bsz = m // seq_len
    hpb = tm // GRID_W
    n_halo = seq_len // GRID_W
    main = lambda b_, i, j: (b_ * nb + i, 0)
    top = lambda b_, i, j: (b_ * n_halo + jnp.maximum(i * hpb - 1, 0), 0)
    bot = lambda b_, i, j: (b_ * n_halo + jnp.minimum((i + 1) * hpb, n_halo - 1), 0)
    col = lambda b_, i, j: (0, j)
    return pl.pallas_call(
        functools.partial(_ffn_up_kernel, n_blocks=nb),
        grid=(bsz, nb, f // tn),
        in_specs=[pl.BlockSpec((GRID_W, d), top), pl.BlockSpec((tm, d), main), pl.BlockSpec((GRID_W, d), bot),
                  pl.BlockSpec((d, tn), col), pl.BlockSpec((d, tn), col),
                  pl.BlockSpec((9, tn), col), pl.BlockSpec((1, tn), col)],
        out_specs=pl.BlockSpec((tm, tn), lambda b_, i, j: (b_ * nb + i, j)),
        out_shape=jax.ShapeDtypeStruct((m, f), BF16),
        scratch_shapes=[pltpu.VMEM((tm + 2 * GRID_W, d), BF16)],
        compiler_params=_cparams("parallel", "parallel", "arbitrary"),
        name="ffn_up",
    )(h, h, h, wg, wu, w9, b)


def _down_proj_kernel(t_ref, w_ref, x1_ref, g2_ref, gpost_ref, o_ref):
    k = pl.program_id(1)
    last = pl.num_programs(1) - 1

    @pl.when(k == 0)
    def _():
        o_ref[...] = _dot(t_ref[...], w_ref[...])

    @pl.when((k > 0) & (k < last))
    def _():
        o_ref[...] += _dot(t_ref[...], w_ref[...])

    @pl.when(k == last)
    def _():
        half = o_ref.shape[0] // 2
        for rows in (slice(0, half), slice(half, 2 * half)):
            acc = o_ref[rows] + _dot(t_ref[rows], w_ref[...])
            o_ref[rows] = x1_ref[rows] + g2_ref[0] * _rms(acc, gpost_ref[...])


def down_proj_residual(t, w, x1, g2, gpost, rows_per_batch, tm, tk):
    m, kk = t.shape
    d = w.shape[1]
    bpb = rows_per_batch // tm
    return pl.pallas_call(
        _down_proj_kernel,
        grid=(m // tm, kk // tk),
        in_specs=[pl.BlockSpec((tm, tk), lambda i, k: (i, k)),
                  pl.BlockSpec((tk, d), lambda i, k: (k, 0)),
                  pl.BlockSpec((tm, d), lambda i, k: (i, 0)),
                  pl.BlockSpec((1, 1, d), lambda i, k: (i // bpb, 0, 0)),
                  pl.BlockSpec((1, d), lambda i, k: (0, 0))],
        out_specs=pl.BlockSpec((tm, d), lambda i, k: (i, 0)),
        out_shape=jax.ShapeDtypeStruct((m, d), F32),
        compiler_params=_cparams("parallel", "arbitrary"),
        name="down_proj_residual",
    )(t, w, x1, g2, gpost)


def _head_sum(x, e, et):
    return _dot_sel(et, _dot_sel(e, x, left=False), left=False)


def _rw_prep_kernel(prev_ref, cur_ref, next_ref, cw_ref, wup_ref, aup_ref, gup_ref, vec_ref, e_ref, et_ref,
                    af_ref, btf_ref, ktf_ref, rf_ref, wtf_ref, ab_ref, btb_ref, ktb_ref, rb_ref, wtb_ref,
                    v_ref, bon_ref, g_ref, *, n_blocks, width):
    i = pl.program_id(1)
    tl = cur_ref.shape[0]
    cur = cur_ref[...]
    rows = lax.broadcasted_iota(jnp.int32, cur.shape, 0)
    prev_row = jnp.where(i > 0, prev_ref[7:8, :], 0.0)
    next_row = jnp.where(i < n_blocks - 1, next_ref[0:1, :], 0.0)
    before = jnp.where(rows == 0, prev_row, pltpu.roll(cur, 1, axis=0))
    after = jnp.where(rows == tl - 1, next_row, pltpu.roll(cur, tl - 1, axis=0))
    cw = cw_ref[...]
    c = before * cw[0:1] + cur * cw[1:2] + after * cw[2:3]

    w_ = width
    k = c[:, :w_]
    v = c[:, w_:2 * w_]
    lo = c[:, 2 * w_:2 * w_ + 256]
    r = c[:, 2 * w_ + 256:3 * w_ + 256]
    gl = c[:, 3 * w_ + 256:3 * w_ + 512]
    vec = vec_ref[...]
    e = e_ref[...]
    et = et_ref[...]
    k_k, k_a, r_k = vec[4:5], vec[5:6], vec[6:7]

    kk = k * k_k
    kk = kk * lax.rsqrt(_head_sum(kk * kk, e, et) + 1e-12)
    v_ref[...] = v.astype(BF16)
    tlo = jnp.tanh(lo)
    kd_sum = jnp.zeros_like(k)
    ti = lax.broadcasted_iota(jnp.int32, (tl, tl), 0)
    tj = lax.broadcasted_iota(jnp.int32, (tl, tl), 1)
    same_chunk = ti // RW_CHUNK == tj // RW_CHUNK
    wi = lax.broadcasted_iota(jnp.int32, (tl // 8, tl), 0)
    wj = lax.broadcasted_iota(jnp.int32, (tl // 8, tl), 1)
    chunk_rows = (wi * 8) // RW_CHUNK == wj // RW_CHUNK
    dirs = ((af_ref, btf_ref, ktf_ref, rf_ref, wtf_ref, tj <= ti), (ab_ref, btb_ref, ktb_ref, rb_ref, wtb_ref, tj >= ti))
    for d, (a_ref, bt_ref, kt_ref, r_ref, wt_ref, done) in enumerate(dirs):
        z = vec[d:d + 1] + _dot_x3(tlo, wup_ref[0, d], wup_ref[1, d])
        nz = -z
        softplus = jnp.maximum(nz, 0.0) + jnp.log1p(jnp.exp(-jnp.abs(nz)))
        w_log = -softplus - 0.5
        lw = -jnp.exp(w_log)
        a = jax.nn.sigmoid(vec[2 + d:3 + d] + _dot_x3(lo, aup_ref[0, d], aup_ref[1, d]))
        kd = k * (1.0 + (a - 1.0) * k_a)
        kd_sum = kd_sum + kd
        sel = jnp.concatenate([(same_chunk & done).astype(F32), chunk_rows.astype(F32)], axis=0).astype(BF16)
        sums = _dot_sel(sel, lw)
        cum = sums[:tl]
        w_inv = jnp.exp(-cum)
        a_ref[...] = (-kk * jnp.exp(cum - lw)).astype(BF16)
        bt_ref[...] = (kk * a * w_inv).astype(BF16)
        kt_ref[...] = (kd * w_inv).astype(BF16)
        r_ref[...] = (r * jnp.exp(cum)).astype(BF16)
        wt_ref[...] = jnp.exp(sums[tl:])
    bon_ref[...] = _head_sum(r * kd_sum * r_k, e, et) * v
    g_ref[...] = _dot_x3(jax.nn.sigmoid(gl), gup_ref[0], gup_ref[1])


def rw_prep(p, cw, wup, aup, gup, vec, e, et, seq_len, tl, width):
    bsz, _, rw_cols = p.shape
    nb = seq_len // tl
    hb = tl // 8
    n_halo = seq_len // 8
    fixed = lambda a: pl.BlockSpec(a.shape, lambda b_, i: (0,) * a.ndim)
    row_spec = pl.BlockSpec((None, tl, width), lambda b_, i: (b_, i, 0))
    wt_spec = pl.BlockSpec((None, tl // 8, width), lambda b_, i: (b_, i, 0))
    act = lambda dt: jax.ShapeDtypeStruct((bsz, seq_len, width), dt)
    wt = jax.ShapeDtypeStruct((bsz, seq_len // 8, width), F32)
    per_dir_specs = [row_spec] * 4 + [wt_spec]
    per_dir_shapes = [act(BF16)] * 4 + [wt]
    return pl.pallas_call(
        functools.partial(_rw_prep_kernel, n_blocks=nb, width=width),
        grid=(bsz, nb),
        in_specs=[pl.BlockSpec((None, 8, rw_cols), lambda b_, i: (b_, jnp.maximum(i * hb - 1, 0), 0)),
                  pl.BlockSpec((None, tl, rw_cols), lambda b_, i: (b_, i, 0)),
                  pl.BlockSpec((None, 8, rw_cols), lambda b_, i: (b_, jnp.minimum((i + 1) * hb, n_halo - 1), 0)),
                  fixed(cw), fixed(wup), fixed(aup), fixed(gup), fixed(vec), fixed(e), fixed(et)],
        out_specs=per_dir_specs * 2 + [row_spec] * 3,
        out_shape=per_dir_shapes * 2 + [act(BF16), act(F32), act(F32)],
        compiler_params=_cparams("parallel", "parallel"),
        name="rw_prep",
    )(p, p, p, cw, wup, aup, gup, vec, e, et)


def _block_diag(y, diag):
    yb = y.astype(BF16)
    tiled = jnp.concatenate([yb] * RW_PROBLEMS, axis=0)
    return jnp.where(diag, tiled, jnp.zeros_like(tiled))


def _unit_tri_inverse(a, eye, blk, off1, off2, diag):
    n = RW_CHUNK
    mm = lambda xs, ys: _each(lambda x, y: _dot(x.astype(BF16), _block_diag(y, diag)), xs, ys)
    add = lambda xs, ys: _each(jnp.add, xs, ys)
    stack = lambda xs, ys: _each(lambda x, y: jnp.concatenate([x, y], axis=0), xs, ys)
    d = [ai * blk for ai in a]
    x = [eye + di for di in d]
    d2 = mm(d, d)
    both = mm(stack(x, d2), d2)
    x = _each(lambda xi, p: xi + p[:n], x, both)
    d4 = [p[n:] for p in both]
    both = mm(stack(x, d4), d4)
    x = _each(lambda xi, p: xi + p[:n], x, both)
    d8 = [p[n:] for p in both]
    x = add(x, mm(x, d8))
    x = add(x, mm(mm(x, [ai * off1 for ai in a]), x))
    x = add(x, mm(mm(x, [ai * off2 for ai in a]), x))
    return x


def _rw_chunk(a, bt, kt, r, v, wtot, s, masks):
    strict, incl, eye, blk, off1, off2, diag = masks
    n = RW_CHUNK
    bf = lambda xs: [x.astype(BF16) for x in xs]
    bdiag = lambda xs: [_block_diag(x, diag) for x in xs]
    stack = lambda xs, ys: _each(lambda x, y: jnp.concatenate([x, y], axis=0), xs, ys)
    ar = stack(a, r)
    sc = _each(_dot_nt, ar, stack(bdiag(bt), bdiag(kt)))
    sc_s = _each(_dot_nt, ar, bf(s))
    a_ab = _each(lambda x, m: jnp.where(m, x[:n, :RW_CAT], 0.0), sc, strict)
    a_ak = _each(lambda x, m: jnp.where(m, x[:n, RW_CAT:], 0.0), sc, strict)
    r_rb = _each(lambda x, m: jnp.where(m, x[n:, :RW_CAT], 0.0), sc, incl)
    r_rk = _each(lambda x, m: jnp.where(m, x[n:, RW_CAT:], 0.0), sc, incl)
    inv = _unit_tri_inverse(a_ab, eye, blk, off1, off2, diag)
    kv = _each(_dot, bf(stack(a_ak, r_rk)), bdiag(v))
    rhs = _each(lambda x, y: x[:n] + y[:n], sc_s, kv)
    u = _each(_dot, bf(inv), bdiag(rhs))
    y = _each(lambda x, p, q: x[n:] + p + q[n:], sc_s, _each(_dot, bf(r_rb), bdiag(u)), kv)
    to_end = lambda x, w: (x.astype(F32) * w).astype(BF16)
    bk = _each(lambda x, z, w: jnp.concatenate([to_end(x, w), to_end(z, w)], axis=0), bt, kt, wtot)
    upd = _each(_dot_tn, stack(bf(u), v), bk)
    return y, _each(lambda si, w, x: si * w + jnp.where(diag, x, 0.0), s, wtot, upd)


def _rw_scan_kernel(*refs):
    fwd, bwd = refs[0:6], refs[6:12]
    s0_ref, yf_ref, yb_ref, sout_ref, s_ref = refs[12:]
    c = pl.program_id(1)

    @pl.when(c == 0)
    def _():
        s_ref[...] = s0_ref[...]

    n = RW_CHUNK
    t = lax.broadcasted_iota(jnp.int32, (n, RW_CAT), 0)
    lane = lax.broadcasted_iota(jnp.int32, (n, RW_CAT), 1)
    step = lane % n
    eye = (t == step).astype(F32)
    blk = (t // RW_SUB == step // RW_SUB).astype(F32)
    off1 = ((t // (2 * RW_SUB) == step // (2 * RW_SUB)) & (t // RW_SUB != step // RW_SUB)).astype(F32)
    off2 = (t // (2 * RW_SUB) != step // (2 * RW_SUB)).astype(F32)
    di = lax.broadcasted_iota(jnp.int32, (RW_CAT, RW_CAT), 0) // n
    dj = lax.broadcasted_iota(jnp.int32, (RW_CAT, RW_CAT), 1) // n
    pairs = s_ref.shape[1]
    lanes = [slice(g * LANES, (g + 1) * LANES) for g in range(pairs)]
    strict = [t > step] * pairs + [t < step] * pairs
    incl = [t >= step] * pairs + [t <= step] * pairs
    masks = (strict, incl, eye, blk, off1, off2, di == dj)
    ops = [[ref[:, ln] for ref in (f, b) for ln in lanes] for f, b in zip(fwd[:5], bwd[:5])]
    wtot = [ref[0:1, ln] for ref in (fwd[5], bwd[5]) for ln in lanes]
    states = [s_ref[d, g] for d in range(2) for g in range(pairs)]
    ys, s_new = _rw_chunk(*ops, wtot, states, masks)
    for d, y_ref in enumerate((yf_ref, yb_ref)):
        for g in range(pairs):
            y_ref[:, lanes[g]] = ys[d * pairs + g].astype(y_ref.dtype)
            s_ref[d, g] = s_new[d * pairs + g]

    @pl.when(c == pl.num_programs(1) - 1)
    def _():
        sout_ref[...] = s_ref[...]


def rw_scan(fwd, bwd, s0):
    bsz, seq_len, width = fwd[0].shape
    nc = seq_len // RW_CHUNK
    pairs = width // LANES
    f_spec = pl.BlockSpec((None, RW_CHUNK, width), lambda b_, c: (b_, c, 0))
    b_spec = pl.BlockSpec((None, RW_CHUNK, width), lambda b_, c: (b_, nc - 1 - c, 0))
    fw_spec = pl.BlockSpec((None, 8, width), lambda b_, c: (b_, c, 0))
    bw_spec = pl.BlockSpec((None, 8, width), lambda b_, c: (b_, nc - 1 - c, 0))
    s_spec = pl.BlockSpec((None, 2, pairs, RW_CAT, RW_CAT), lambda b_, c: (b_, 0, 0, 0, 0))
    y_shape = jax.ShapeDtypeStruct((bsz, seq_len, width), BF16)
    return pl.pallas_call(
        _rw_scan_kernel,
        grid=(bsz, nc),
        in_specs=[f_spec] * 5 + [fw_spec] + [b_spec] * 5 + [bw_spec] + [s_spec],
        out_specs=[f_spec, b_spec, s_spec],
        out_shape=[y_shape, y_shape, jax.ShapeDtypeStruct(s0.shape, F32)],
        scratch_shapes=[pltpu.VMEM((2, pairs, RW_CAT, RW_CAT), F32)],
        compiler_params=_cparams("parallel", "arbitrary"),
        name="rw_scan",
    )(*fwd, *bwd, s0)


def _rw_finish_kernel(yf_ref, yb_ref, bon_ref, g_ref, lnw_ref, lnb_ref, e_ref, et_ref, o_ref):
    y = yf_ref[...].astype(F32) + yb_ref[...].astype(F32)
    e = e_ref[...]
    et = et_ref[...]
    mu = _head_sum(y, e, et) * (1.0 / RW_HEAD_DIM)
    yc = y - mu
    var = _head_sum(yc * yc, e, et) * (1.0 / RW_HEAD_DIM)
    yn = yc * lax.rsqrt(var + LNX_EPS) * lnw_ref[...] + lnb_ref[...]
    o_ref[...] = ((yn + bon_ref[...]) * g_ref[...]).astype(o_ref.dtype)


def rw_finish(yf, yb, bon, g, lnw, lnb, e, et, tl):
    m, width = yf.shape
    row = pl.BlockSpec((tl, width), lambda i: (i, 0))
    fixed = lambda a: pl.BlockSpec(a.shape, lambda i: (0, 0))
    return pl.pallas_call(
        _rw_finish_kernel,
        grid=(m // tl,),
        in_specs=[row, row, row, row, fixed(lnw), fixed(lnb), fixed(e), fixed(et)],
        out_specs=row,
        out_shape=jax.ShapeDtypeStruct((m, width), BF16),
        compiler_params=_cparams("parallel"),
        name="rw_finish",
    )(yf, yb, bon, g, lnw, lnb, e, et)


def _rope(x, cos, sin):
    quarter = LANES // 4
    width = x.shape[1]
    lane = lax.broadcasted_iota(jnp.int32, x.shape, 1)
    first = (lane // quarter) % 2 == 0
    partner = jnp.where(first, pltpu.roll(x, width - quarter, axis=1), pltpu.roll(x, quarter, axis=1))
    return x * cos + partner * sin


def _ret_scan_kernel(kf_ref, vf_ref, qf_ref, cf_ref, sf_ref, kb_ref, vb_ref, qb_ref, cb_ref, sb_ref,
                     dec_ref, fs_ref, te_ref, gc_ref, s0_ref, yf_ref, yb_ref, sout_ref, s_ref, *, rope, scale):
    c = pl.program_id(1)

    @pl.when(c == 0)
    def _():
        s_ref[...] = s0_ref[...]

    n_heads = s_ref.shape[1]
    heads = [slice(h * LANES, (h + 1) * LANES) for h in range(n_heads)]
    qs, ks, vs, decay, ss = [], [], [], [], []
    for d, (k_ref, v_ref, q_ref, cos_ref, sin_ref) in enumerate(
            ((kf_ref, vf_ref, qf_ref, cf_ref, sf_ref), (kb_ref, vb_ref, qb_ref, cb_ref, sb_ref))):
        q = q_ref[...]
        k = k_ref[...]
        if rope:
            cos = jnp.concatenate([cos_ref[...]] * n_heads, axis=1)
            sin = jnp.concatenate([sin_ref[...]] * n_heads, axis=1)
            q = _rope(q, cos, sin)
            k = _rope(k, cos, sin)
        k = k * scale
        v = v_ref[...].astype(BF16)
        q_cross = (q * fs_ref[d]).astype(BF16)
        k_end = (k * te_ref[d]).astype(BF16)
        q = q.astype(BF16)
        k = k.astype(BF16)
        for h, hs in enumerate(heads):
            qs.append((q[:, hs], q_cross[:, hs]))
            ks.append((k[:, hs], k_end[:, hs]))
            vs.append(v[:, hs])
            decay.append(dec_ref[d, h])
            ss.append(s_ref[d, h])
    scores = _each(lambda q, k, dm: (_dot_nt(q[0], k[0]) * dm).astype(BF16), qs, ks, decay)
    cross = _each(lambda q, s: _dot(q[1], s.astype(BF16)), qs, ss)
    intra = _each(_dot, scores, vs)
    upd = _each(lambda k, v: _dot_tn(k[1], v), ks, vs)
    for d, y_ref in enumerate((yf_ref, yb_ref)):
        for h, hs in enumerate(heads):
            i = d * n_heads + h
            y_ref[:, hs] = (intra[i] + cross[i]).astype(y_ref.dtype)
            s_ref[d, h] = gc_ref[h] * ss[i] + upd[i]

    @pl.when(c == pl.num_programs(1) - 1)
    def _():
        sout_ref[...] = s_ref[...]


def ret_scan(p, cos, sin, tables, s0, rope):
    bsz, seq_len, cols = p.shape
    width = cols // 4
    nc = seq_len // RET_CHUNK
    dec, fs, te, gc = tables

    def specs(chunk):
        cols_ = [pl.BlockSpec((None, RET_CHUNK, width), lambda b_, c, o=o: (b_, chunk(c), o)) for o in range(3)]
        return cols_ + [pl.BlockSpec((RET_CHUNK, LANES), lambda b_, c: (chunk(c), 0))] * 2

    fwd = lambda c: c
    bwd = lambda c: nc - 1 - c
    fixed = lambda a: pl.BlockSpec(a.shape, lambda b_, c: (0,) * a.ndim)
    s_spec = pl.BlockSpec((None,) + s0.shape[1:], lambda b_, c: (b_, 0, 0, 0, 0))
    y_shape = jax.ShapeDtypeStruct((bsz, seq_len, width), BF16)
    return pl.pallas_call(
        functools.partial(_ret_scan_kernel, rope=rope, scale=float(LANES) ** -0.5),
        grid=(bsz, nc),
        in_specs=specs(fwd) + specs(bwd) + [fixed(dec), fixed(fs), fixed(te), fixed(gc), s_spec],
        out_specs=[pl.BlockSpec((None, RET_CHUNK, width), lambda b_, c: (b_, c, 0)),
                   pl.BlockSpec((None, RET_CHUNK, width), lambda b_, c: (b_, nc - 1 - c, 0)),
                   s_spec],
        out_shape=[y_shape, y_shape, jax.ShapeDtypeStruct(s0.shape, F32)],
        scratch_shapes=[pltpu.VMEM(s0.shape[1:], F32)],
        compiler_params=_cparams("parallel", "arbitrary"),
        name="ret_scan",
    )(p, p, p, cos, sin, p, p, p, cos, sin, dec, fs, te, gc, s0)


def _ret_tables(n_heads):
    n = RET_CHUNK
    lg = np.log(1.0 - 2.0 ** (-5.0 - np.arange(n_heads, dtype=np.float64)))
    pos = np.arange(n, dtype=np.float64)
    diff = pos[:, None] - pos[None, :]
    masks = (diff >= 0, diff < 0)
    dists = (diff, -diff)
    dec = np.stack([np.where(m, np.exp(np.where(m, dd, 0.0)[None] * lg[:, None, None]), 0.0)
                    for m, dd in zip(masks, dists)])
    done = np.stack([pos, n - 1.0 - pos])
    lanes = np.repeat(lg, LANES)[None, None, :]
    fs = np.exp((done[:, :, None] + 1.0) * lanes)
    te = np.exp((n - 1.0 - done[:, :, None]) * lanes)
    gc = np.broadcast_to(np.exp(n * lg)[:, None, None], (n_heads, 1, LANES))
    return tuple(jnp.asarray(a, F32) for a in (dec, fs, te, gc))


def _ret_finish_kernel(yf_ref, yb_ref, g_ref, o_ref):
    g = g_ref[...]
    gate = g * jax.nn.sigmoid(g)
    for h in range(o_ref.shape[1] // LANES):
        hs = slice(h * LANES, (h + 1) * LANES)
        y = yf_ref[:, hs].astype(F32) + yb_ref[:, hs].astype(F32)
        y = y * lax.rsqrt(jnp.mean(y * y, axis=-1, keepdims=True) + EPS)
        o_ref[:, hs] = (y * gate[:, hs]).astype(o_ref.dtype)


def ret_finish(yf, yb, p2, tl):
    m, width = yf.shape
    row = pl.BlockSpec((tl, width), lambda i: (i, 0))
    return pl.pallas_call(
        _ret_finish_kernel,
        grid=(m // tl,),
        in_specs=[row, row, pl.BlockSpec((tl, width), lambda i: (i, 3))],
        out_specs=row,
        out_shape=jax.ShapeDtypeStruct((m, width), BF16),
        compiler_params=_cparams("parallel"),
        name="ret_finish",
    )(yf, yb, p2)


def _rope_tables(seq_len):
    n = LANES // 4
    t = np.arange(seq_len)
    inv = ROPE_BASE ** (-np.arange(n, dtype=np.float64) / n)
    ang_row = (t // GRID_W)[:, None] * inv
    ang_col = (t % GRID_W)[:, None] * inv
    cr, sr, cc, sc = np.cos(ang_row), np.sin(ang_row), np.cos(ang_col), np.sin(ang_col)
    cos = np.concatenate([cr, cr, cc, cc], axis=-1)
    sin = np.concatenate([-sr, sr, -sc, sc], axis=-1)
    return jnp.asarray(cos, F32), jnp.asarray(sin, F32)


def _pad_rows(w, rows, at):
    return jnp.zeros((rows, w.shape[1]), w.dtype).at[at:at + w.shape[0]].set(w)


def kernel(x, c, ctx, c_ctx, w_ada, b_ada, norm_pre_mix, norm_post_mix, norm_pre_ffn, norm_post_ffn, w_in, rw_conv, rw_w0, rw_w_up, rw_a0, rw_a_up, rw_g_up, rw_k_k, rw_k_a, rw_r_k, rw_lnx_w, rw_lnx_b, w_out, ffn_w_gate, ffn_w_up, ffn_conv, ffn_conv_b, ffn_w_down):
    bsz, seq_len, d = x.shape
    ctx_len = ctx.shape[1]
    n_layers = w_ada.shape[0]
    assert n_layers == 1, "context-stream outputs are only needed between layers"
    rw_w = rw_k_k.shape[1]
    ret_w = w_out.shape[1] - rw_w
    rw_cols = 3 * rw_w + 2 * DECAY_LORA + 2 * AAA_LORA + GATE_LORA
    rw_pad = -rw_cols % (2 * LANES)
    d_ff = ffn_w_gate.shape[2]
    m = bsz * seq_len
    flat = lambda a: a.reshape(-1, a.shape[-1])
    l = 0

    cond = jnp.zeros((8, d), F32).at[:bsz].set(c).at[bsz].set(c_ctx)
    mod = ada_modulation(cond, w_ada[l], b_ada[l][None])
    sh1, sc1, g1, sh2, sc2, g2 = [mm[:bsz, None, :] for mm in jnp.split(mod, 6, axis=-1)]
    sh_c = jnp.broadcast_to(mod[bsz, :d], (bsz, 1, d))
    sc_c = jnp.broadcast_to(mod[bsz, d:2 * d], (bsz, 1, d))

    w_ret = w_in[l][:, rw_cols:].astype(BF16)
    g_pre = norm_pre_mix[l][None]

    def in_proj(a, sc, sh, rows, tm):
        p_a, xm = norm_mod_matmul(flat(a), g_pre, sc, sh, w_in[l], rw_cols + rw_pad, rows, tm, TN_IN_RW)
        p_b = matmul(xm, w_ret, tm, TN_IN_RET, F32)
        return p_a.reshape(bsz, rows, -1), p_b.reshape(bsz, rows, -1)

    p_rw, p_ret = in_proj(x, sc1, sh1, seq_len, min(TM_IN, seq_len))
    pc_rw, pc_ret = in_proj(ctx, sc_c, sh_c, ctx_len, ctx_len)

    cw = jnp.pad(rw_conv[l], ((0, 0), (0, rw_pad)))
    lora_rows = 2 * DECAY_LORA + 2 * AAA_LORA
    split = lambda w: jnp.stack(_bf16_terms(w, 2))
    wup = split(jnp.stack([_pad_rows(rw_w_up[l][dd], lora_rows, dd * DECAY_LORA) for dd in range(2)]))
    aup = split(jnp.stack([_pad_rows(rw_a_up[l][dd], lora_rows, 2 * DECAY_LORA + dd * AAA_LORA) for dd in range(2)]))
    gup = split(_pad_rows(rw_g_up[l], 256, 0))
    vec = jnp.concatenate([rw_w0[l], rw_a0[l], rw_k_k[l][None], rw_k_a[l][None], rw_r_k[l].reshape(1, rw_w)], axis=0)
    vec = jnp.pad(vec, ((0, 1), (0, 0)))
    head_of_lane = jnp.arange(rw_w) // RW_HEAD_DIM
    e = (head_of_lane[:, None] == jnp.arange(LANES)[None, :]).astype(BF16)
    et = e.T
    prep = functools.partial(rw_prep, cw=cw, wup=wup, aup=aup, gup=gup, vec=vec, e=e, et=et, width=rw_w)
    scan_args = lambda o: (o[0:4] + [o[10], o[4]], o[5:9] + [o[10], o[9]])
    s0 = jnp.zeros((bsz, 2, rw_w // LANES, RW_CAT, RW_CAT), F32)
    outs = list(prep(pc_rw, seq_len=ctx_len, tl=min(TL_PREP, ctx_len)))
    _, _, s_rw = rw_scan(*scan_args(outs), s0)
    outs = list(prep(p_rw, seq_len=seq_len, tl=TL_PREP))
    yf, yb, _ = rw_scan(*scan_args(outs), s_rw)
    o_rw = rw_finish(flat(yf), flat(yb), flat(outs[11]), flat(outs[12]), rw_lnx_w[l][None], rw_lnx_b[l][None],
                     e, et, TL_FINISH)

    n_ret_heads = ret_w // LANES
    tables = _ret_tables(n_ret_heads)
    cos, sin = _rope_tables(seq_len)
    s0 = jnp.zeros((bsz, 2, n_ret_heads, LANES, LANES), F32)
    _, _, s_ret = ret_scan(pc_ret, cos[:ctx_len], sin[:ctx_len], tables, s0, rope=False)
    yf, yb, _ = ret_scan(p_ret, cos, sin, tables, s_ret, rope=True)
    o_ret = ret_finish(flat(yf), flat(yb), flat(p_ret), TL_FINISH)

    w_o = w_out[l].astype(BF16)
    x1, h = out_proj_residual(o_rw, o_ret, w_o[:rw_w], w_o[rw_w:], flat(x), g1, norm_post_mix[l][None],
                              norm_pre_ffn[l][None], sc2, sh2, seq_len, min(TM_OUT, seq_len))
    t = ffn_up(h, ffn_w_gate[l], ffn_w_up[l], ffn_conv[l].reshape(9, d_ff),
               ffn_conv_b[l][None], seq_len, min(TM_FFN, seq_len), TN_FFN)
    out = down_proj_residual(t, ffn_w_down[l].astype(BF16), x1, g2, norm_post_ffn[l][None], seq_len,
                             min(TM_DOWN, seq_len), TK_DOWN)
    return out.reshape(bsz, seq_len, d)
```

```python
import functools

import jax
import jax.numpy as jnp
import numpy as np
from jax import lax
from jax.experimental import pallas as pl
from jax.experimental.pallas import tpu as pltpu

F32 = jnp.float32
BF16 = jnp.bfloat16
HI = lax.Precision.HIGHEST

LANES = 128
EPS = 1e-6
GRID_W = 64
RW_HEAD_DIM = 64
DECAY_LORA = 64
AAA_LORA = 64
GATE_LORA = 160
LNX_EPS = 64e-5
RET_CHUNK = 128
ROPE_BASE = 10000.0
RW_CHUNK = 64
RW_SUB = 16
RW_PROBLEMS = 2
RW_CAT = RW_PROBLEMS * RW_CHUNK
RW_STEP_CHUNKS = 2
VMEM_LIMIT = 56 * 1024 * 1024

TM_IN, TN_IN_RW, TN_IN_RET, TM_OUT, TM_FFN, TN_FFN, TM_DOWN, TK_DOWN = 1024, 512, 1024, 512, 1024, 512, 1024, 1408
TL_PREP, TL_FINISH = 256, 512


def _cparams(*sem):
    return pltpu.CompilerParams(dimension_semantics=sem, vmem_limit_bytes=VMEM_LIMIT)


def _dot(a, b, prec=None):
    return jnp.dot(a, b, precision=prec, preferred_element_type=F32)


def _dot_nt(a, b, prec=None):
    return lax.dot_general(a, b, (((1,), (1,)), ((), ())), precision=prec, preferred_element_type=F32)


def _dot_tn(a, b, prec=None):
    return lax.dot_general(a, b, (((0,), (0,)), ((), ())), precision=prec, preferred_element_type=F32)


def _bf16_terms(a, n):
    terms = []
    for _ in range(n):
        t = a.astype(BF16)
        terms.append(t)
        a = a - t.astype(F32)
    return terms


def _dot_sel(sel, x, left=True):
    hi, lo = _bf16_terms(x, 2)
    return (_dot(sel, hi) + _dot(sel, lo)) if left else (_dot(hi, sel) + _dot(lo, sel))


def _dot_x3(a, b_hi, b_lo):
    ah, al = _bf16_terms(a, 2)
    return _dot(ah, b_hi) + (_dot(ah, b_lo) + _dot(al, b_hi))


def _rms(x, g):
    return x * lax.rsqrt(jnp.mean(x * x, axis=-1, keepdims=True) + EPS) * g


def _each(f, *lists):
    return [f(*xs) for xs in zip(*lists)]


def _ada_kernel(s_ref, w_ref, b_ref, o_ref):
    s = s_ref[...]
    s = s * jax.nn.sigmoid(s)
    w_hi, w_lo = _bf16_terms(w_ref[...], 2)
    o_ref[...] = _dot_x3(s, w_hi, w_lo) + b_ref[...]


def ada_modulation(s, w, b, tn=1024):
    m, d = s.shape
    n = w.shape[1]
    return pl.pallas_call(
        _ada_kernel,
        grid=(n // tn,),
        in_specs=[pl.BlockSpec((m, d), lambda j: (0, 0)),
                  pl.BlockSpec((d, tn), lambda j: (0, j)),
                  pl.BlockSpec((1, tn), lambda j: (0, j))],
        out_specs=pl.BlockSpec((m, tn), lambda j: (0, j)),
        out_shape=jax.ShapeDtypeStruct((m, n), F32),
        compiler_params=_cparams("arbitrary"),
        name="ada_modulation",
    )(s, w, b)


def _norm_mm_kernel(x_ref, g_ref, sc_ref, sh_ref, w_ref, o_ref, xm_ref):
    j = pl.program_id(1)
    w = w_ref[...]

    @pl.when(j == 0)
    def _():
        half = x_ref.shape[0] // 2
        for rows in (slice(0, half), slice(half, 2 * half)):
            y = _rms(x_ref[rows], g_ref[...])
            xm = (y * (1.0 + sc_ref[0]) + sh_ref[0]).astype(BF16)
            xm_ref[rows] = xm
            o_ref[rows] = _dot_nt(xm, w)

    @pl.when(j > 0)
    def _():
        o_ref[...] = _dot_nt(xm_ref[...], w)


def norm_mod_matmul(x, g, sc, sh, wt, rows_per_batch, tm, tn):
    m, d = x.shape
    n = wt.shape[0]
    bpb = rows_per_batch // tm
    return pl.pallas_call(
        _norm_mm_kernel,
        grid=(m // tm, n // tn),
        in_specs=[pl.BlockSpec((tm, d), lambda i, j: (i, 0)),
                  pl.BlockSpec((1, d), lambda i, j: (0, 0)),
                  pl.BlockSpec((1, 1, d), lambda i, j: (i // bpb, 0, 0)),
                  pl.BlockSpec((1, 1, d), lambda i, j: (i // bpb, 0, 0)),
                  pl.BlockSpec((tn, d), lambda i, j: (j, 0))],
        out_specs=[pl.BlockSpec((tm, tn), lambda i, j: (i, j)), pl.BlockSpec((tm, d), lambda i, j: (i, 0))],
        out_shape=[jax.ShapeDtypeStruct((m, n), F32), jax.ShapeDtypeStruct((m, d), BF16)],
        compiler_params=_cparams("parallel", "arbitrary"),
        name="norm_mod_matmul",
    )(x, g, sc, sh, wt)


def _mm_kernel(a_ref, w_ref, o_ref):
    o_ref[...] = _dot_nt(a_ref[...], w_ref[...]).astype(o_ref.dtype)


def matmul_nt(a, wt, tm, tn, out_dtype):
    m, k = a.shape
    n = wt.shape[0]
    return pl.pallas_call(
        _mm_kernel,
        grid=(m // tm, n // tn),
        in_specs=[pl.BlockSpec((tm, k), lambda i, j: (i, 0)),
                  pl.BlockSpec((tn, k), lambda i, j: (j, 0))],
        out_specs=pl.BlockSpec((tm, tn), lambda i, j: (i, j)),
        out_shape=jax.ShapeDtypeStruct((m, n), out_dtype),
        compiler_params=_cparams("parallel", "arbitrary"),
        name="matmul_nt",
    )(a, wt)


def _out_proj_kernel(oa_ref, ob_ref, wa_ref, wb_ref, x_ref, g1_ref, gpost_ref, gpre_ref, sc2_ref, sh2_ref,
                     x1_ref, h_ref):
    out = _dot(oa_ref[...], wa_ref[...]) + _dot(ob_ref[...], wb_ref[...])
    x1 = x_ref[...] + g1_ref[0] * _rms(out, gpost_ref[...])
    x1_ref[...] = x1
    h_ref[...] = (_rms(x1, gpre_ref[...]) * (1.0 + sc2_ref[0]) + sh2_ref[0]).astype(BF16)


def out_proj_residual(oa, ob, wa, wb, x, g1, gpost, gpre, sc2, sh2, rows_per_batch, tm):
    m, ka = oa.shape
    kb = ob.shape[1]
    d = x.shape[1]
    bpb = rows_per_batch // tm
    row = lambda i: (i, 0)
    fixed = lambda i: (0, 0)
    per_b = lambda i: (i // bpb, 0, 0)
    return pl.pallas_call(
        _out_proj_kernel,
        grid=(m // tm,),
        in_specs=[pl.BlockSpec((tm, ka), row), pl.BlockSpec((tm, kb), row),
                  pl.BlockSpec((ka, d), fixed), pl.BlockSpec((kb, d), fixed),
                  pl.BlockSpec((tm, d), row),
                  pl.BlockSpec((1, 1, d), per_b),
                  pl.BlockSpec((1, d), fixed), pl.BlockSpec((1, d), fixed),
                  pl.BlockSpec((1, 1, d), per_b), pl.BlockSpec((1, 1, d), per_b)],
        out_specs=[pl.BlockSpec((tm, d), row), pl.BlockSpec((tm, d), row)],
        out_shape=[jax.ShapeDtypeStruct((m, d), F32), jax.ShapeDtypeStruct((m, d), BF16)],
        compiler_params=_cparams("parallel"),
        name="out_proj_residual",
    )(oa, ob, wa, wb, x, g1, gpost, gpre, sc2, sh2)


def _ffn_up_kernel(top_ref, mid_ref, bot_ref, wg_ref, wu_ref, cw_ref, cb_ref, o_ref, hext_ref, *, n_blocks):
    i = pl.program_id(1)
    tm = mid_ref.shape[0]

    @pl.when(pl.program_id(2) == 0)
    def _():
        hext_ref[0:GRID_W] = top_ref[...]
        hext_ref[GRID_W:GRID_W + tm] = mid_ref[...]
        hext_ref[GRID_W + tm:] = bot_ref[...]

    gate = _dot(hext_ref[...], wg_ref[...].astype(BF16))
    up = _dot(mid_ref[...], wu_ref[...].astype(BF16))
    tn = gate.shape[1]
    mid = gate[GRID_W:GRID_W + tm]
    above = jnp.where(i > 0, gate[:GRID_W], 0.0)
    below = jnp.where(i < n_blocks - 1, gate[GRID_W + tm:], 0.0)
    above = jnp.concatenate([above, mid[:tm - GRID_W]], axis=0)
    below = jnp.concatenate([mid[GRID_W:], below], axis=0)
    w = cw_ref[...]
    col = lax.broadcasted_iota(jnp.int32, (tm, tn), 0) % GRID_W

    def column_sum(dx):
        return above * w[dx:dx + 1] + mid * w[3 + dx:4 + dx] + below * w[6 + dx:7 + dx]

    left = jnp.where(col > 0, pltpu.roll(column_sum(0), 1, axis=0), 0.0)
    right = jnp.where(col < GRID_W - 1, pltpu.roll(column_sum(2), tm - 1, axis=0), 0.0)
    gt = column_sum(1) + left + right + cb_ref[...]
    o_ref[...] = (gt * jax.nn.sigmoid(gt) * up).astype(o_ref.dtype)


def ffn_up(h, wg, wu, w9, b, seq_len, tm, tn):
    m, d = h.shape
    f = wg.shape[1]
    nb = seq_len // tm
    bsz = m // seq_len
    hpb = tm // GRID_W
    n_halo = seq_len // GRID_W
    main = lambda b_, i, j: (b_ * nb + i, 0)
    top = lambda b_, i, j: (b_ * n_halo + jnp.maximum(i * hpb - 1, 0), 0)
    bot = lambda b_, i, j: (b_ * n_halo + jnp.minimum((i + 1) * hpb, n_halo - 1), 0)
    col = lambda b_, i, j: (0, j)
    return pl.pallas_call(
        functools.partial(_ffn_up_kernel, n_blocks=nb),
        grid=(bsz, nb, f // tn),
        in_specs=[pl.BlockSpec((GRID_W, d), top), pl.BlockSpec((tm, d), main), pl.BlockSpec((GRID_W, d), bot),
                  pl.BlockSpec((d, tn), col), pl.BlockSpec((d, tn), col),
                  pl.BlockSpec((9, tn), col), pl.BlockSpec((1, tn), col)],
        out_specs=pl.BlockSpec((tm, tn), lambda b_, i, j: (b_ * nb + i, j)),
        out_shape=jax.ShapeDtypeStruct((m, f), BF16),
        scratch_shapes=[pltpu.VMEM((tm + 2 * GRID_W, d), BF16)],
        compiler_params=_cparams("parallel", "parallel", "arbitrary"),
        name="ffn_up",
    )(h, h, h, wg, wu, w9, b)


def _down_proj_kernel(t_ref, w_ref, x1_ref, g2_ref, gpost_ref, o_ref):
    k = pl.program_id(1)
    last = pl.num_programs(1) - 1

    @pl.when(k == 0)
    def _():
        o_ref[...] = _dot(t_ref[...], w_ref[...])

    @pl.when((k > 0) & (k < last))
    def _():
        o_ref[...] += _dot(t_ref[...], w_ref[...])

    @pl.when(k == last)
    def _():
        half = o_ref.shape[0] // 2
        for rows in (slice(0, half), slice(half, 2 * half)):
            acc = o_ref[rows] + _dot(t_ref[rows], w_ref[...])
            o_ref[rows] = x1_ref[rows] + g2_ref[0] * _rms(acc, gpost_ref[...])


def down_proj_residual(t, w, x1, g2, gpost, rows_per_batch, tm, tk):
    m, kk = t.shape
    d = w.shape[1]
    bpb = rows_per_batch // tm
    return pl.pallas_call(
        _down_proj_kernel,
        grid=(m // tm, kk // tk),
        in_specs=[pl.BlockSpec((tm, tk), lambda i, k: (i, k)),
                  pl.BlockSpec((tk, d), lambda i, k: (k, 0)),
                  pl.BlockSpec((tm, d), lambda i, k: (i, 0)),
                  pl.BlockSpec((1, 1, d), lambda i, k: (i // bpb, 0, 0)),
                  pl.BlockSpec((1, d), lambda i, k: (0, 0))],
        out_specs=pl.BlockSpec((tm, d), lambda i, k: (i, 0)),
        out_shape=jax.ShapeDtypeStruct((m, d), F32),
        compiler_params=_cparams("parallel", "arbitrary"),
        name="down_proj_residual",
    )(t, w, x1, g2, gpost)


def _head_sum(x, e, et):
    return _dot_sel(et, _dot_sel(e, x, left=False), left=False)


def _rw_prep_kernel(prev_ref, cur_ref, next_ref, cw_ref, wup_ref, aup_ref, gup_ref, vec_ref, e_ref, et_ref,
                    af_ref, btf_ref, ktf_ref, rf_ref, wtf_ref, ab_ref, btb_ref, ktb_ref, rb_ref, wtb_ref,
                    v_ref, bon_ref, g_ref, *, n_blocks, width):
    i = pl.program_id(1)
    tl = cur_ref.shape[0]
    cur = cur_ref[...]
    rows = lax.broadcasted_iota(jnp.int32, cur.shape, 0)
    prev_row = jnp.where(i > 0, prev_ref[7:8, :], 0.0)
    next_row = jnp.where(i < n_blocks - 1, next_ref[0:1, :], 0.0)
    before = jnp.where(rows == 0, prev_row, pltpu.roll(cur, 1, axis=0))
    after = jnp.where(rows == tl - 1, next_row, pltpu.roll(cur, tl - 1, axis=0))
    cw = cw_ref[...]
    c = before * cw[0:1] + cur * cw[1:2] + after * cw[2:3]

    w_ = width
    k = c[:, :w_]
    v = c[:, w_:2 * w_]
    lo = c[:, 2 * w_:2 * w_ + 256]
    r = c[:, 2 * w_ + 256:3 * w_ + 256]
    gl = c[:, 3 * w_ + 256:3 * w_ + 512]
    vec = vec_ref[...]
    e = e_ref[...]
    et = et_ref[...]
    k_k, k_a, r_k = vec[4:5], vec[5:6], vec[6:7]

    kk = k * k_k
    kk = kk * lax.rsqrt(_head_sum(kk * kk, e, et) + 1e-12)
    v_ref[...] = v.astype(BF16)
    tlo = jnp.tanh(lo)
    kd_sum = jnp.zeros_like(k)
    ti = lax.broadcasted_iota(jnp.int32, (tl, tl), 0)
    tj = lax.broadcasted_iota(jnp.int32, (tl, tl), 1)
    same_chunk = ti // RW_CHUNK == tj // RW_CHUNK
    wi = lax.broadcasted_iota(jnp.int32, (tl // 8, tl), 0)
    wj = lax.broadcasted_iota(jnp.int32, (tl // 8, tl), 1)
    chunk_rows = (wi * 8) // RW_CHUNK == wj // RW_CHUNK
    dirs = ((af_ref, btf_ref, ktf_ref, rf_ref, wtf_ref, tj <= ti), (ab_ref, btb_ref, ktb_ref, rb_ref, wtb_ref, tj >= ti))
    for d, (a_ref, bt_ref, kt_ref, r_ref, wt_ref, done) in enumerate(dirs):
        z = vec[d:d + 1] + _dot_x3(tlo, wup_ref[0, d], wup_ref[1, d])
        nz = -z
        softplus = jnp.maximum(nz, 0.0) + jnp.log1p(jnp.exp(-jnp.abs(nz)))
        w_log = -softplus - 0.5
        lw = -jnp.exp(w_log)
        a = jax.nn.sigmoid(vec[2 + d:3 + d] + _dot_x3(lo, aup_ref[0, d], aup_ref[1, d]))
        kd = k * (1.0 + (a - 1.0) * k_a)
        kd_sum = kd_sum + kd
        sel = jnp.concatenate([(same_chunk & done).astype(F32), chunk_rows.astype(F32)], axis=0).astype(BF16)
        sums = _dot_sel(sel, lw)
        cum = sums[:tl]
        w_inv = jnp.exp(-cum)
        a_ref[...] = (-kk * jnp.exp(cum - lw)).astype(BF16)
        bt_ref[...] = (kk * a * w_inv).astype(BF16)
        kt_ref[...] = (kd * w_inv).astype(BF16)
        r_ref[...] = (r * jnp.exp(cum)).astype(BF16)
        wt_ref[...] = jnp.exp(sums[tl:])
    bon_ref[...] = _head_sum(r * kd_sum * r_k, e, et) * v
    g_ref[...] = _dot_x3(jax.nn.sigmoid(gl), gup_ref[0], gup_ref[1])


def rw_prep(p, cw, wup, aup, gup, vec, e, et, seq_len, tl, width):
    bsz, _, rw_cols = p.shape
    nb = seq_len // tl
    hb = tl // 8
    n_halo = seq_len // 8
    fixed = lambda a: pl.BlockSpec(a.shape, lambda b_, i: (0,) * a.ndim)
    row_spec = pl.BlockSpec((None, tl, width), lambda b_, i: (b_, i, 0))
    wt_spec = pl.BlockSpec((None, tl // 8, width), lambda b_, i: (b_, i, 0))
    act = lambda dt: jax.ShapeDtypeStruct((bsz, seq_len, width), dt)
    wt = jax.ShapeDtypeStruct((bsz, seq_len // 8, width), F32)
    per_dir_specs = [row_spec] * 4 + [wt_spec]
    per_dir_shapes = [act(BF16)] * 4 + [wt]
    return pl.pallas_call(
        functools.partial(_rw_prep_kernel, n_blocks=nb, width=width),
        grid=(bsz, nb),
        in_specs=[pl.BlockSpec((None, 8, rw_cols), lambda b_, i: (b_, jnp.maximum(i * hb - 1, 0), 0)),
                  pl.BlockSpec((None, tl, rw_cols), lambda b_, i: (b_, i, 0)),
                  pl.BlockSpec((None, 8, rw_cols), lambda b_, i: (b_, jnp.minimum((i + 1) * hb, n_halo - 1), 0)),
                  fixed(cw), fixed(wup), fixed(aup), fixed(gup), fixed(vec), fixed(e), fixed(et)],
        out_specs=per_dir_specs * 2 + [row_spec] * 3,
        out_shape=per_dir_shapes * 2 + [act(BF16), act(F32), act(F32)],
        compiler_params=_cparams("parallel", "parallel"),
        name="rw_prep",
    )(p, p, p, cw, wup, aup, gup, vec, e, et)


def _block_diag(y, diag):
    yb = y.astype(BF16)
    tiled = jnp.concatenate([yb] * RW_PROBLEMS, axis=0)
    return jnp.where(diag, tiled, jnp.zeros_like(tiled))


def _unit_tri_inverse(a, eye, blk, off1, off2, diag):
    n = RW_CHUNK
    mm = lambda xs, ys: _each(lambda x, y: _dot(x.astype(BF16), _block_diag(y, diag)), xs, ys)
    add = lambda xs, ys: _each(jnp.add, xs, ys)
    stack = lambda xs, ys: _each(lambda x, y: jnp.concatenate([x, y], axis=0), xs, ys)
    d = [ai * blk for ai in a]
    x = [eye + di for di in d]
    d2 = mm(d, d)
    both = mm(stack(x, d2), d2)
    x = _each(lambda xi, p: xi + p[:n], x, both)
    d4 = [p[n:] for p in both]
    both = mm(stack(x, d4), d4)
    x = _each(lambda xi, p: xi + p[:n], x, both)
    d8 = [p[n:] for p in both]
    x = add(x, mm(x, d8))
    x = add(x, mm(mm(x, [ai * off1 for ai in a]), x))
    x = add(x, mm(mm(x, [ai * off2 for ai in a]), x))
    return x


def _rw_chunk_local(a, bt, kt, r, v, masks):
    strict, incl, eye, blk, off1, off2, diag = masks
    n = RW_CHUNK
    bf = lambda xs: [x.astype(BF16) for x in xs]
    bdiag = lambda xs: [_block_diag(x, diag) for x in xs]
    stack = lambda xs, ys: _each(lambda x, y: jnp.concatenate([x, y], axis=0), xs, ys)
    ar = stack(a, r)
    sc = _each(_dot_nt, ar, stack(bdiag(bt), bdiag(kt)))
    a_ab = _each(lambda x, m: jnp.where(m, x[:n, :RW_CAT], 0.0), sc, strict)
    a_ak = _each(lambda x, m: jnp.where(m, x[:n, RW_CAT:], 0.0), sc, strict)
    r_rb = _each(lambda x, m: jnp.where(m, x[n:, :RW_CAT], 0.0), sc, incl)
    r_rk = _each(lambda x, m: jnp.where(m, x[n:, RW_CAT:], 0.0), sc, incl)
    inv = _unit_tri_inverse(a_ab, eye, blk, off1, off2, diag)
    kv = _each(_dot, bf(stack(a_ak, r_rk)), bdiag(v))
    return ar, bf(inv), bf(r_rb), kv


def _rw_chunk_state(local, bt, kt, v, wtot, s, diag):
    ar, inv, r_rb, kv = local
    n = RW_CHUNK
    bdiag = lambda xs: [_block_diag(x, diag) for x in xs]
    sc_s = _each(lambda x, si: _dot_nt(x, si.astype(BF16)), ar, s)
    rhs = _each(lambda x, y: x[:n] + y[:n], sc_s, kv)
    u = _each(_dot, inv, bdiag(rhs))
    y = _each(lambda x, p, q: x[n:] + p + q[n:], sc_s, _each(_dot, r_rb, bdiag(u)), kv)
    to_end = lambda x, w: (x.astype(F32) * w).astype(BF16)
    uv = _each(lambda x, z: jnp.concatenate([x.astype(BF16), z], axis=0), u, v)
    bk = _each(lambda x, z, w: jnp.concatenate([to_end(x, w), to_end(z, w)], axis=0), bt, kt, wtot)
    upd = _each(_dot_tn, uv, bk)
    return y, _each(lambda si, w, x: si * w + jnp.where(diag, x, 0.0), s, wtot, upd)


def _rw_scan_kernel(*refs):
    fwd, bwd = refs[0:6], refs[6:12]
    s0_ref, yf_ref, yb_ref, sout_ref, s_ref = refs[12:]
    c = pl.program_id(1)

    @pl.when(c == 0)
    def _():
        s_ref[...] = s0_ref[...]

    n = RW_CHUNK
    t = lax.broadcasted_iota(jnp.int32, (n, RW_CAT), 0)
    lane = lax.broadcasted_iota(jnp.int32, (n, RW_CAT), 1)
    step = lane % n
    eye = (t == step).astype(F32)
    blk = (t // RW_SUB == step // RW_SUB).astype(F32)
    off1 = ((t // (2 * RW_SUB) == step // (2 * RW_SUB)) & (t // RW_SUB != step // RW_SUB)).astype(F32)
    off2 = (t // (2 * RW_SUB) != step // (2 * RW_SUB)).astype(F32)
    di = lax.broadcasted_iota(jnp.int32, (RW_CAT, RW_CAT), 0) // n
    dj = lax.broadcasted_iota(jnp.int32, (RW_CAT, RW_CAT), 1) // n
    diag = di == dj
    pairs = s_ref.shape[1]
    lanes = [slice(g * LANES, (g + 1) * LANES) for g in range(pairs)]
    order = [(q, RW_STEP_CHUNKS - 1 - q) for q in range(RW_STEP_CHUNKS)]
    strict = ([t > step] * pairs + [t < step] * pairs) * RW_STEP_CHUNKS
    incl = ([t >= step] * pairs + [t <= step] * pairs) * RW_STEP_CHUNKS
    masks = (strict, incl, eye, blk, off1, off2, diag)
    rows = lambda q: slice(q * n, (q + 1) * n)
    ops = [[ref[rows(q), ln] for qs in order for ref, q in zip((f, b), qs) for ln in lanes]
           for f, b in zip(fwd[:5], bwd[:5])]
    wtot = [ref[8 * q:8 * q + 1, ln] for qs in order for ref, q in zip((fwd[5], bwd[5]), qs) for ln in lanes]
    local = _rw_chunk_local(*ops, masks)
    states = [s_ref[d, g] for d in range(2) for g in range(pairs)]
    per = 2 * pairs
    for i, qs in enumerate(order):
        part = slice(i * per, (i + 1) * per)
        ys, states = _rw_chunk_state([x[part] for x in local], ops[1][part], ops[2][part], ops[4][part], wtot[part],
                                     states, diag)
        for d, (y_ref, q) in enumerate(zip((yf_ref, yb_ref), qs)):
            for g in range(pairs):
                y_ref[rows(q), lanes[g]] = ys[d * pairs + g].astype(y_ref.dtype)
    for d in range(2):
        for g in range(pairs):
            s_ref[d, g] = states[d * pairs + g]

    @pl.when(c == pl.num_programs(1) - 1)
    def _():
        sout_ref[...] = s_ref[...]


def rw_scan(fwd, bwd, s0):
    bsz, seq_len, width = fwd[0].shape
    rows = RW_STEP_CHUNKS * RW_CHUNK
    nc = seq_len // rows
    pairs = width // LANES
    f_spec = pl.BlockSpec((None, rows, width), lambda b_, c: (b_, c, 0))
    b_spec = pl.BlockSpec((None, rows, width), lambda b_, c: (b_, nc - 1 - c, 0))
    fw_spec = pl.BlockSpec((None, rows // 8, width), lambda b_, c: (b_, c, 0))
    bw_spec = pl.BlockSpec((None, rows // 8, width), lambda b_, c: (b_, nc - 1 - c, 0))
    s_spec = pl.BlockSpec((None, 2, pairs, RW_CAT, RW_CAT), lambda b_, c: (b_, 0, 0, 0, 0))
    y_shape = jax.ShapeDtypeStruct((bsz, seq_len, width), BF16)
    return pl.pallas_call(
        _rw_scan_kernel,
        grid=(bsz, nc),
        in_specs=[f_spec] * 5 + [fw_spec] + [b_spec] * 5 + [bw_spec] + [s_spec],
        out_specs=[f_spec, b_spec, s_spec],
        out_shape=[y_shape, y_shape, jax.ShapeDtypeStruct(s0.shape, F32)],
        scratch_shapes=[pltpu.VMEM((2, pairs, RW_CAT, RW_CAT), F32)],
        compiler_params=_cparams("parallel", "arbitrary"),
        name="rw_scan",
    )(*fwd, *bwd, s0)


def _rw_finish_kernel(yf_ref, yb_ref, bon_ref, g_ref, lnw_ref, lnb_ref, e_ref, et_ref, o_ref):
    y = yf_ref[...].astype(F32) + yb_ref[...].astype(F32)
    e = e_ref[...]
    et = et_ref[...]
    mu = _head_sum(y, e, et) * (1.0 / RW_HEAD_DIM)
    yc = y - mu
    var = _head_sum(yc * yc, e, et) * (1.0 / RW_HEAD_DIM)
    yn = yc * lax.rsqrt(var + LNX_EPS) * lnw_ref[...] + lnb_ref[...]
    o_ref[...] = ((yn + bon_ref[...]) * g_ref[...]).astype(o_ref.dtype)


def rw_finish(yf, yb, bon, g, lnw, lnb, e, et, tl):
    m, width = yf.shape
    row = pl.BlockSpec((tl, width), lambda i: (i, 0))
    fixed = lambda a: pl.BlockSpec(a.shape, lambda i: (0, 0))
    return pl.pallas_call(
        _rw_finish_kernel,
        grid=(m // tl,),
        in_specs=[row, row, row, row, fixed(lnw), fixed(lnb), fixed(e), fixed(et)],
        out_specs=row,
        out_shape=jax.ShapeDtypeStruct((m, width), BF16),
        compiler_params=_cparams("parallel"),
        name="rw_finish",
    )(yf, yb, bon, g, lnw, lnb, e, et)


def _rope(x, cos, sin):
    quarter = LANES // 4
    width = x.shape[1]
    lane = lax.broadcasted_iota(jnp.int32, x.shape, 1)
    first = (lane // quarter) % 2 == 0
    partner = jnp.where(first, pltpu.roll(x, width - quarter, axis=1), pltpu.roll(x, quarter, axis=1))
    return x * cos + partner * sin


def _ret_scan_kernel(kf_ref, vf_ref, qf_ref, cf_ref, sf_ref, kb_ref, vb_ref, qb_ref, cb_ref, sb_ref,
                     dec_ref, fs_ref, te_ref, gc_ref, s0_ref, yf_ref, yb_ref, sout_ref, s_ref, *, rope, scale):
    c = pl.program_id(1)

    @pl.when(c == 0)
    def _():
        s_ref[...] = s0_ref[...]

    n_heads = s_ref.shape[1]
    heads = [slice(h * LANES, (h + 1) * LANES) for h in range(n_heads)]
    qs, ks, vs, decay, ss = [], [], [], [], []
    for d, (k_ref, v_ref, q_ref, cos_ref, sin_ref) in enumerate(
            ((kf_ref, vf_ref, qf_ref, cf_ref, sf_ref), (kb_ref, vb_ref, qb_ref, cb_ref, sb_ref))):
        q = q_ref[...]
        k = k_ref[...]
        if rope:
            cos = jnp.concatenate([cos_ref[...]] * n_heads, axis=1)
            sin = jnp.concatenate([sin_ref[...]] * n_heads, axis=1)
            q = _rope(q, cos, sin)
            k = _rope(k, cos, sin)
        k = k * scale
        v = v_ref[...].astype(BF16)
        q_cross = (q * fs_ref[d]).astype(BF16)
        k_end = (k * te_ref[d]).astype(BF16)
        q = q.astype(BF16)
        k = k.astype(BF16)
        for h, hs in enumerate(heads):
            qs.append((q[:, hs], q_cross[:, hs]))
            ks.append((k[:, hs], k_end[:, hs]))
            vs.append(v[:, hs])
            decay.append(dec_ref[d, h])
            ss.append(s_ref[d, h])
    scores = _each(lambda q, k, dm: (_dot_nt(q[0], k[0]) * dm).astype(BF16), qs, ks, decay)
    cross = _each(lambda q, s: _dot(q[1], s.astype(BF16)), qs, ss)
    intra = _each(_dot, scores, vs)
    upd = _each(lambda k, v: _dot_tn(k[1], v), ks, vs)
    for d, y_ref in enumerate((yf_ref, yb_ref)):
        for h, hs in enumerate(heads):
            i = d * n_heads + h
            y_ref[:, hs] = (intra[i] + cross[i]).astype(y_ref.dtype)
            s_ref[d, h] = gc_ref[h] * ss[i] + upd[i]

    @pl.when(c == pl.num_programs(1) - 1)
    def _():
        sout_ref[...] = s_ref[...]


def ret_scan(p, cos, sin, tables, s0, rope):
    bsz, seq_len, cols = p.shape
    width = cols // 4
    nc = seq_len // RET_CHUNK
    dec, fs, te, gc = tables

    def specs(chunk):
        cols_ = [pl.BlockSpec((None, RET_CHUNK, width), lambda b_, c, o=o: (b_, chunk(c), o)) for o in range(3)]
        return cols_ + [pl.BlockSpec((RET_CHUNK, LANES), lambda b_, c: (chunk(c), 0))] * 2

    fwd = lambda c: c
    bwd = lambda c: nc - 1 - c
    fixed = lambda a: pl.BlockSpec(a.shape, lambda b_, c: (0,) * a.ndim)
    s_spec = pl.BlockSpec((None,) + s0.shape[1:], lambda b_, c: (b_, 0, 0, 0, 0))
    y_shape = jax.ShapeDtypeStruct((bsz, seq_len, width), BF16)
    return pl.pallas_call(
        functools.partial(_ret_scan_kernel, rope=rope, scale=float(LANES) ** -0.5),
        grid=(bsz, nc),
        in_specs=specs(fwd) + specs(bwd) + [fixed(dec), fixed(fs), fixed(te), fixed(gc), s_spec],
        out_specs=[pl.BlockSpec((None, RET_CHUNK, width), lambda b_, c: (b_, c, 0)),
                   pl.BlockSpec((None, RET_CHUNK, width), lambda b_, c: (b_, nc - 1 - c, 0)),
                   s_spec],
        out_shape=[y_shape, y_shape, jax.ShapeDtypeStruct(s0.shape, F32)],
        scratch_shapes=[pltpu.VMEM(s0.shape[1:], F32)],
        compiler_params=_cparams("parallel", "arbitrary"),
        name="ret_scan",
    )(p, p, p, cos, sin, p, p, p, cos, sin, dec, fs, te, gc, s0)


def _ret_tables(n_heads):
    n = RET_CHUNK
    lg = np.log(1.0 - 2.0 ** (-5.0 - np.arange(n_heads, dtype=np.float64)))
    pos = np.arange(n, dtype=np.float64)
    diff = pos[:, None] - pos[None, :]
    masks = (diff >= 0, diff < 0)
    dists = (diff, -diff)
    dec = np.stack([np.where(m, np.exp(np.where(m, dd, 0.0)[None] * lg[:, None, None]), 0.0)
                    for m, dd in zip(masks, dists)])
    done = np.stack([pos, n - 1.0 - pos])
    lanes = np.repeat(lg, LANES)[None, None, :]
    fs = np.exp((done[:, :, None] + 1.0) * lanes)
    te = np.exp((n - 1.0 - done[:, :, None]) * lanes)
    gc = np.broadcast_to(np.exp(n * lg)[:, None, None], (n_heads, 1, LANES))
    return tuple(jnp.asarray(a, F32) for a in (dec, fs, te, gc))


def _ret_finish_kernel(yf_ref, yb_ref, g_ref, o_ref):
    g = g_ref[...]
    gate = g * jax.nn.sigmoid(g)
    for h in range(o_ref.shape[1] // LANES):
        hs = slice(h * LANES, (h + 1) * LANES)
        y = yf_ref[:, hs].astype(F32) + yb_ref[:, hs].astype(F32)
        y = y * lax.rsqrt(jnp.mean(y * y, axis=-1, keepdims=True) + EPS)
        o_ref[:, hs] = (y * gate[:, hs]).astype(o_ref.dtype)


def ret_finish(yf, yb, p2, tl):
    m, width = yf.shape
    row = pl.BlockSpec((tl, width), lambda i: (i, 0))
    return pl.pallas_call(
        _ret_finish_kernel,
        grid=(m // tl,),
        in_specs=[row, row, pl.BlockSpec((tl, width), lambda i: (i, 3))],
        out_specs=row,
        out_shape=jax.ShapeDtypeStruct((m, width), BF16),
        compiler_params=_cparams("parallel"),
        name="ret_finish",
    )(yf, yb, p2)


def _rope_tables(seq_len):
    n = LANES // 4
    t = np.arange(seq_len)
    inv = ROPE_BASE ** (-np.arange(n, dtype=np.float64) / n)
    ang_row = (t // GRID_W)[:, None] * inv
    ang_col = (t % GRID_W)[:, None] * inv
    cr, sr, cc, sc = np.cos(ang_row), np.sin(ang_row), np.cos(ang_col), np.sin(ang_col)
    cos = np.concatenate([cr, cr, cc, cc], axis=-1)
    sin = np.concatenate([-sr, sr, -sc, sc], axis=-1)
    return jnp.asarray(cos, F32), jnp.asarray(sin, F32)


def _pad_rows(w, rows, at):
    return jnp.zeros((rows, w.shape[1]), w.dtype).at[at:at + w.shape[0]].set(w)


def kernel(x, c, ctx, c_ctx, w_ada, b_ada, norm_pre_mix, norm_post_mix, norm_pre_ffn, norm_post_ffn, w_in, rw_conv, rw_w0, rw_w_up, rw_a0, rw_a_up, rw_g_up, rw_k_k, rw_k_a, rw_r_k, rw_lnx_w, rw_lnx_b, w_out, ffn_w_gate, ffn_w_up, ffn_conv, ffn_conv_b, ffn_w_down):
    bsz, seq_len, d = x.shape
    ctx_len = ctx.shape[1]
    n_layers = w_ada.shape[0]
    assert n_layers == 1, "context-stream outputs are only needed between layers"
    rw_w = rw_k_k.shape[1]
    ret_w = w_out.shape[1] - rw_w
    rw_cols = 3 * rw_w + 2 * DECAY_LORA + 2 * AAA_LORA + GATE_LORA
    rw_pad = -rw_cols % (2 * LANES)
    d_ff = ffn_w_gate.shape[2]
    m = bsz * seq_len
    flat = lambda a: a.reshape(-1, a.shape[-1])
    l = 0

    cond = jnp.zeros((8, d), F32).at[:bsz].set(c).at[bsz].set(c_ctx)
    mod = ada_modulation(cond, w_ada[l], b_ada[l][None])
    sh1, sc1, g1, sh2, sc2, g2 = [mm[:bsz, None, :] for mm in jnp.split(mod, 6, axis=-1)]
    sh_c = jnp.broadcast_to(mod[bsz, :d], (bsz, 1, d))
    sc_c = jnp.broadcast_to(mod[bsz, d:2 * d], (bsz, 1, d))

    w_in_t = w_in[l].T
    w_rw_t = w_in_t[:rw_cols + rw_pad].astype(BF16)
    w_ret_t = w_in_t[rw_cols:].astype(BF16)
    g_pre = norm_pre_mix[l][None]

    def in_proj(a, sc, sh, rows, tm):
        p_a, xm = norm_mod_matmul(flat(a), g_pre, sc, sh, w_rw_t, rows, tm, TN_IN_RW)
        p_b = matmul_nt(xm, w_ret_t, tm, TN_IN_RET, F32)
        return p_a.reshape(bsz, rows, -1), p_b.reshape(bsz, rows, -1)

    p_rw, p_ret = in_proj(x, sc1, sh1, seq_len, min(TM_IN, seq_len))
    pc_rw, pc_ret = in_proj(ctx, sc_c, sh_c, ctx_len, ctx_len)

    cw = jnp.pad(rw_conv[l], ((0, 0), (0, rw_pad)))
    lora_rows = 2 * DECAY_LORA + 2 * AAA_LORA
    split = lambda w: jnp.stack(_bf16_terms(w, 2))
    wup = split(jnp.stack([_pad_rows(rw_w_up[l][dd], lora_rows, dd * DECAY_LORA) for dd in range(2)]))
    aup = split(jnp.stack([_pad_rows(rw_a_up[l][dd], lora_rows, 2 * DECAY_LORA + dd * AAA_LORA) for dd in range(2)]))
    gup = split(_pad_rows(rw_g_up[l], 256, 0))
    vec = jnp.concatenate([rw_w0[l], rw_a0[l], rw_k_k[l][None], rw_k_a[l][None], rw_r_k[l].reshape(1, rw_w)], axis=0)
    vec = jnp.pad(vec, ((0, 1), (0, 0)))
    head_of_lane = jnp.arange(rw_w) // RW_HEAD_DIM
    e = (head_of_lane[:, None] == jnp.arange(LANES)[None, :]).astype(BF16)
    et = e.T
    prep = functools.partial(rw_prep, cw=cw, wup=wup, aup=aup, gup=gup, vec=vec, e=e, et=et, width=rw_w)
    scan_args = lambda o: (o[0:4] + [o[10], o[4]], o[5:9] + [o[10], o[9]])
    s0 = jnp.zeros((bsz, 2, rw_w // LANES, RW_CAT, RW_CAT), F32)
    outs = list(prep(pc_rw, seq_len=ctx_len, tl=min(TL_PREP, ctx_len)))
    _, _, s_rw = rw_scan(*scan_args(outs), s0)
    outs = list(prep(p_rw, seq_len=seq_len, tl=TL_PREP))
    yf, yb, _ = rw_scan(*scan_args(outs), s_rw)
    o_rw = rw_finish(flat(yf), flat(yb), flat(outs[11]), flat(outs[12]), rw_lnx_w[l][None], rw_lnx_b[l][None],
                     e, et, TL_FINISH)

    n_ret_heads = ret_w // LANES
    tables = _ret_tables(n_ret_heads)
    cos, sin = _rope_tables(seq_len)
    s0 = jnp.zeros((bsz, 2, n_ret_heads, LANES, LANES), F32)
    _, _, s_ret = ret_scan(pc_ret, cos[:ctx_len], sin[:ctx_len], tables, s0, rope=False)
    yf, yb, _ = ret_scan(p_ret, cos, sin, tables, s_ret, rope=True)
    o_ret = ret_finish(flat(yf), flat(yb), flat(p_ret), TL_FINISH)

    w_o = w_out[l].astype(BF16)
    x1, h = out_proj_residual(o_rw, o_ret, w_o[:rw_w], w_o[rw_w:], flat(x), g1, norm_post_mix[l][None],
                              norm_pre_ffn[l][None], sc2, sh2, seq_len, min(TM_OUT, seq_len))
    t = ffn_up(h, ffn_w_gate[l], ffn_w_up[l], ffn_conv[l].reshape(9, d_ff),
               ffn_conv_b[l][None], seq_len, min(TM_FFN, seq_len), TN_FFN)
    out = down_proj_residual(t, ffn_w_down[l].astype(BF16), x1, g2, norm_post_ffn[l][None], seq_len,
                             min(TM_DOWN, seq_len), TK_DOWN)
    return out.reshape(bsz, seq_len, d)
```

```python
import functools

import jax
import jax.numpy as jnp
import numpy as np
from jax import lax
from jax.experimental import pallas as pl
from jax.experimental.pallas import tpu as pltpu

F32 = jnp.float32
BF16 = jnp.bfloat16

LANES = 128
EPS = 1e-6
GRID_W = 64
RW_HEAD_DIM = 64
DECAY_LORA = 64
AAA_LORA = 64
GATE_LORA = 160
LNX_EPS = 64e-5
RET_CHUNK = 128
ROPE_BASE = 10000.0
LOG2_E = 1.4426950408889634
RW_CHUNK = 64
RW_SUB = 16
RW_PROBLEMS = 2
RW_CAT = RW_PROBLEMS * RW_CHUNK
RW_STEP_CHUNKS = 4
VMEM_LIMIT = 56 * 1024 * 1024

TM_IN, TN_IN_RW, TN_IN_RET, TM_OUT, TM_FFN, TN_FFN, TM_DOWN, TK_DOWN = 1024, 512, 1024, 512, 1024, 512, 1024, 1408
TL_PREP, TL_FINISH = 256, 512


def _cparams(*sem):
    return pltpu.CompilerParams(dimension_semantics=sem, vmem_limit_bytes=VMEM_LIMIT)


def _dot(a, b, prec=None):
    return jnp.dot(a, b, precision=prec, preferred_element_type=F32)


def _dot_nt(a, b, prec=None):
    return lax.dot_general(a, b, (((1,), (1,)), ((), ())), precision=prec, preferred_element_type=F32)


def _dot_tn(a, b, prec=None):
    return lax.dot_general(a, b, (((0,), (0,)), ((), ())), precision=prec, preferred_element_type=F32)


def _bf16_terms(a, n):
    terms = []
    for _ in range(n):
        t = a.astype(BF16)
        terms.append(t)
        a = a - t.astype(F32)
    return terms


def _dot_sel(sel, x, left=True):
    hi, lo = _bf16_terms(x, 2)
    return (_dot(sel, hi) + _dot(sel, lo)) if left else (_dot(hi, sel) + _dot(lo, sel))


def _dot_x3(a, b_hi, b_lo):
    ah, al = _bf16_terms(a, 2)
    return _dot(ah, b_hi) + (_dot(ah, b_lo) + _dot(al, b_hi))


def _rms(x, g):
    return x * lax.rsqrt(jnp.mean(x * x, axis=-1, keepdims=True) + EPS) * g


def _each(f, *lists):
    return [f(*xs) for xs in zip(*lists)]


def _ada_kernel(s_ref, w_ref, b_ref, o_ref):
    s = s_ref[...]
    s = s * jax.nn.sigmoid(s)
    w_hi, w_lo = _bf16_terms(w_ref[...], 2)
    o_ref[...] = _dot_x3(s, w_hi, w_lo) + b_ref[...]


def ada_modulation(s, w, b, tn=1024):
    m, d = s.shape
    n = w.shape[1]
    return pl.pallas_call(
        _ada_kernel,
        grid=(n // tn,),
        in_specs=[pl.BlockSpec((m, d), lambda j: (0, 0)),
                  pl.BlockSpec((d, tn), lambda j: (0, j)),
                  pl.BlockSpec((1, tn), lambda j: (0, j))],
        out_specs=pl.BlockSpec((m, tn), lambda j: (0, j)),
        out_shape=jax.ShapeDtypeStruct((m, n), F32),
        compiler_params=_cparams("arbitrary"),
        name="ada_modulation",
    )(s, w, b)


def _norm_mm_kernel(x_ref, g_ref, sc_ref, sh_ref, w_ref, o_ref, xm_ref):
    j = pl.program_id(1)
    w = w_ref[...]

    @pl.when(j == 0)
    def _():
        half = x_ref.shape[0] // 2
        for rows in (slice(0, half), slice(half, 2 * half)):
            y = _rms(x_ref[rows], g_ref[...])
            xm = (y * (1.0 + sc_ref[0]) + sh_ref[0]).astype(BF16)
            xm_ref[rows] = xm
            o_ref[rows] = _dot_nt(xm, w)

    @pl.when(j > 0)
    def _():
        o_ref[...] = _dot_nt(xm_ref[...], w)


def norm_mod_matmul(x, g, sc, sh, wt, rows_per_batch, tm, tn):
    m, d = x.shape
    n = wt.shape[0]
    bpb = rows_per_batch // tm
    return pl.pallas_call(
        _norm_mm_kernel,
        grid=(m // tm, n // tn),
        in_specs=[pl.BlockSpec((tm, d), lambda i, j: (i, 0)),
                  pl.BlockSpec((1, d), lambda i, j: (0, 0)),
                  pl.BlockSpec((1, 1, d), lambda i, j: (i // bpb, 0, 0)),
                  pl.BlockSpec((1, 1, d), lambda i, j: (i // bpb, 0, 0)),
                  pl.BlockSpec((tn, d), lambda i, j: (j, 0))],
        out_specs=[pl.BlockSpec((tm, tn), lambda i, j: (i, j)), pl.BlockSpec((tm, d), lambda i, j: (i, 0))],
        out_shape=[jax.ShapeDtypeStruct((m, n), F32), jax.ShapeDtypeStruct((m, d), BF16)],
        compiler_params=_cparams("parallel", "arbitrary"),
        name="norm_mod_matmul",
    )(x, g, sc, sh, wt)


def _mm_kernel(a_ref, w_ref, o_ref):
    o_ref[...] = _dot_nt(a_ref[...], w_ref[...]).astype(o_ref.dtype)


def matmul_nt(a, wt, tm, tn, out_dtype):
    m, k = a.shape
    n = wt.shape[0]
    return pl.pallas_call(
        _mm_kernel,
        grid=(m // tm, n // tn),
        in_specs=[pl.BlockSpec((tm, k), lambda i, j: (i, 0)),
                  pl.BlockSpec((tn, k), lambda i, j: (j, 0))],
        out_specs=pl.BlockSpec((tm, tn), lambda i, j: (i, j)),
        out_shape=jax.ShapeDtypeStruct((m, n), out_dtype),
        compiler_params=_cparams("parallel", "arbitrary"),
        name="matmul_nt",
    )(a, wt)


def _out_proj_kernel(oa_ref, ob_ref, wa_ref, wb_ref, x_ref, g1_ref, gpost_ref, gpre_ref, sc2_ref, sh2_ref,
                     x1_ref, h_ref):
    out = _dot(oa_ref[...], wa_ref[...]) + _dot(ob_ref[...], wb_ref[...])
    x1 = x_ref[...] + g1_ref[0] * _rms(out, gpost_ref[...])
    x1_ref[...] = x1
    h_ref[...] = (_rms(x1, gpre_ref[...]) * (1.0 + sc2_ref[0]) + sh2_ref[0]).astype(BF16)


def out_proj_residual(oa, ob, wa, wb, x, g1, gpost, gpre, sc2, sh2, rows_per_batch, tm):
    m, ka = oa.shape
    kb = ob.shape[1]
    d = x.shape[1]
    bpb = rows_per_batch // tm
    row = lambda i: (i, 0)
    fixed = lambda i: (0, 0)
    per_b = lambda i: (i // bpb, 0, 0)
    return pl.pallas_call(
        _out_proj_kernel,
        grid=(m // tm,),
        in_specs=[pl.BlockSpec((tm, ka), row), pl.BlockSpec((tm, kb), row),
                  pl.BlockSpec((ka, d), fixed), pl.BlockSpec((kb, d), fixed),
                  pl.BlockSpec((tm, d), row),
                  pl.BlockSpec((1, 1, d), per_b),
                  pl.BlockSpec((1, d), fixed), pl.BlockSpec((1, d), fixed),
                  pl.BlockSpec((1, 1, d), per_b), pl.BlockSpec((1, 1, d), per_b)],
        out_specs=[pl.BlockSpec((tm, d), row), pl.BlockSpec((tm, d), row)],
        out_shape=[jax.ShapeDtypeStruct((m, d), F32), jax.ShapeDtypeStruct((m, d), BF16)],
        compiler_params=_cparams("parallel"),
        name="out_proj_residual",
    )(oa, ob, wa, wb, x, g1, gpost, gpre, sc2, sh2)


def _ffn_up_kernel(top_ref, mid_ref, bot_ref, wg_ref, wu_ref, cw_ref, cb_ref, o_ref, hext_ref, *, n_blocks):
    i = pl.program_id(1)
    tm = mid_ref.shape[0]

    @pl.when(pl.program_id(2) == 0)
    def _():
        hext_ref[0:GRID_W] = top_ref[...]
        hext_ref[GRID_W:GRID_W + tm] = mid_ref[...]
        hext_ref[GRID_W + tm:] = bot_ref[...]

    gate = _dot(hext_ref[...], wg_ref[...].astype(BF16))
    up = _dot(mid_ref[...], wu_ref[...].astype(BF16))
    tn = gate.shape[1]
    mid = gate[GRID_W:GRID_W + tm]
    above = jnp.where(i > 0, gate[:GRID_W], 0.0)
    below = jnp.where(i < n_blocks - 1, gate[GRID_W + tm:], 0.0)
    above = jnp.concatenate([above, mid[:tm - GRID_W]], axis=0)
    below = jnp.concatenate([mid[GRID_W:], below], axis=0)
    w = cw_ref[...]
    col = lax.broadcasted_iota(jnp.int32, (tm, tn), 0) % GRID_W

    def column_sum(dx):
        return above * w[dx:dx + 1] + mid * w[3 + dx:4 + dx] + below * w[6 + dx:7 + dx]

    left = jnp.where(col > 0, pltpu.roll(column_sum(0), 1, axis=0), 0.0)
    right = jnp.where(col < GRID_W - 1, pltpu.roll(column_sum(2), tm - 1, axis=0), 0.0)
    gt = column_sum(1) + left + right + cb_ref[...]
    o_ref[...] = (gt * jax.nn.sigmoid(gt) * up).astype(o_ref.dtype)


def ffn_up(h, wg, wu, w9, b, seq_len, tm, tn):
    m, d = h.shape
    f = wg.shape[1]
    nb = seq_len // tm
    bsz = m // seq_len
    hpb = tm // GRID_W
    n_halo = seq_len // GRID_W
    main = lambda b_, i, j: (b_ * nb + i, 0)
    top = lambda b_, i, j: (b_ * n_halo + jnp.maximum(i * hpb - 1, 0), 0)
    bot = lambda b_, i, j: (b_ * n_halo + jnp.minimum((i + 1) * hpb, n_halo - 1), 0)
    col = lambda b_, i, j: (0, j)
    return pl.pallas_call(
        functools.partial(_ffn_up_kernel, n_blocks=nb),
        grid=(bsz, nb, f // tn),
        in_specs=[pl.BlockSpec((GRID_W, d), top), pl.BlockSpec((tm, d), main), pl.BlockSpec((GRID_W, d), bot),
                  pl.BlockSpec((d, tn), col), pl.BlockSpec((d, tn), col),
                  pl.BlockSpec((9, tn), col), pl.BlockSpec((1, tn), col)],
        out_specs=pl.BlockSpec((tm, tn), lambda b_, i, j: (b_ * nb + i, j)),
        out_shape=jax.ShapeDtypeStruct((m, f), BF16),
        scratch_shapes=[pltpu.VMEM((tm + 2 * GRID_W, d), BF16)],
        compiler_params=_cparams("parallel", "parallel", "arbitrary"),
        name="ffn_up",
    )(h, h, h, wg, wu, w9, b)


def _down_proj_kernel(t_ref, w_ref, x1_ref, g2_ref, gpost_ref, o_ref):
    k = pl.program_id(1)
    last = pl.num_programs(1) - 1

    @pl.when(k == 0)
    def _():
        o_ref[...] = _dot(t_ref[...], w_ref[...])

    @pl.when((k > 0) & (k < last))
    def _():
        o_ref[...] += _dot(t_ref[...], w_ref[...])

    @pl.when(k == last)
    def _():
        half = o_ref.shape[0] // 2
        for rows in (slice(0, half), slice(half, 2 * half)):
            acc = o_ref[rows] + _dot(t_ref[rows], w_ref[...])
            o_ref[rows] = x1_ref[rows] + g2_ref[0] * _rms(acc, gpost_ref[...])


def down_proj_residual(t, w, x1, g2, gpost, rows_per_batch, tm, tk):
    m, kk = t.shape
    d = w.shape[1]
    bpb = rows_per_batch // tm
    return pl.pallas_call(
        _down_proj_kernel,
        grid=(m // tm, kk // tk),
        in_specs=[pl.BlockSpec((tm, tk), lambda i, k: (i, k)),
                  pl.BlockSpec((tk, d), lambda i, k: (k, 0)),
                  pl.BlockSpec((tm, d), lambda i, k: (i, 0)),
                  pl.BlockSpec((1, 1, d), lambda i, k: (i // bpb, 0, 0)),
                  pl.BlockSpec((1, d), lambda i, k: (0, 0))],
        out_specs=pl.BlockSpec((tm, d), lambda i, k: (i, 0)),
        out_shape=jax.ShapeDtypeStruct((m, d), F32),
        compiler_params=_cparams("parallel", "arbitrary"),
        name="down_proj_residual",
    )(t, w, x1, g2, gpost)


def _head_sum(x, e, et):
    return _dot_sel(et, _dot_sel(e, x, left=False), left=False)


def _rw_prep_kernel(prev_ref, cur_ref, next_ref, cw_ref, wup_ref, aup_ref, gup_ref, vec_ref, e_ref, et_ref,
                    af_ref, btf_ref, ktf_ref, rf_ref, wtf_ref, ab_ref, btb_ref, ktb_ref, rb_ref, wtb_ref,
                    v_ref, bon_ref, g_ref, *, n_blocks, width):
    i = pl.program_id(1)
    tl = cur_ref.shape[0]
    cur = cur_ref[...]
    rows = lax.broadcasted_iota(jnp.int32, cur.shape, 0)
    prev_row = jnp.where(i > 0, prev_ref[7:8, :], 0.0)
    next_row = jnp.where(i < n_blocks - 1, next_ref[0:1, :], 0.0)
    before = jnp.where(rows == 0, prev_row, pltpu.roll(cur, 1, axis=0))
    after = jnp.where(rows == tl - 1, next_row, pltpu.roll(cur, tl - 1, axis=0))
    cw = cw_ref[...]
    c = before * cw[0:1] + cur * cw[1:2] + after * cw[2:3]

    w_ = width
    k = c[:, :w_]
    v = c[:, w_:2 * w_]
    lo = c[:, 2 * w_:2 * w_ + 256]
    r = c[:, 2 * w_ + 256:3 * w_ + 256]
    gl = c[:, 3 * w_ + 256:3 * w_ + 512]
    vec = vec_ref[...]
    e = e_ref[...]
    et = et_ref[...]
    k_k, k_a, r_k = vec[4:5], vec[5:6], vec[6:7]

    kk = k * k_k
    kk = kk * lax.rsqrt(_head_sum(kk * kk, e, et) + 1e-12)
    neg_kk = -kk
    one_minus_k_a = 1.0 - k_a
    v_ref[...] = v.astype(BF16)
    tlo = jnp.tanh(lo)
    kd_sum = jnp.zeros_like(k)
    ti = lax.broadcasted_iota(jnp.int32, (tl, tl), 0)
    tj = lax.broadcasted_iota(jnp.int32, (tl, tl), 1)
    same_chunk = ti // RW_CHUNK == tj // RW_CHUNK
    wi = lax.broadcasted_iota(jnp.int32, (tl // 8, tl), 0)
    wj = lax.broadcasted_iota(jnp.int32, (tl // 8, tl), 1)
    chunk_rows = (wi * 8) // RW_CHUNK == wj // RW_CHUNK
    dirs = ((af_ref, btf_ref, ktf_ref, rf_ref, wtf_ref, tj <= ti), (ab_ref, btb_ref, ktb_ref, rb_ref, wtb_ref, tj >= ti))
    for d, (a_ref, bt_ref, kt_ref, r_ref, wt_ref, done) in enumerate(dirs):
        z = vec[d:d + 1] + _dot_x3(tlo, wup_ref[0, d], wup_ref[1, d])
        nz = -z
        softplus = jnp.maximum(nz, 0.0) + jnp.log1p(jnp.exp(-jnp.abs(nz)))
        w_log = -softplus - 0.5
        lw = jnp.exp(w_log) * -LOG2_E
        a = jax.nn.sigmoid(vec[2 + d:3 + d] + _dot_x3(lo, aup_ref[0, d], aup_ref[1, d]))
        kd = k * (a * k_a + one_minus_k_a)
        kd_sum = kd_sum + kd
        sel = jnp.concatenate([(same_chunk & done).astype(F32), chunk_rows.astype(F32)], axis=0).astype(BF16)
        sums = _dot_sel(sel, lw)
        cum = sums[:tl]
        w_inv = jnp.exp2(-cum)
        a_ref[...] = (neg_kk * jnp.exp2(cum - lw)).astype(BF16)
        bt_ref[...] = (kk * a * w_inv).astype(BF16)
        kt_ref[...] = (kd * w_inv).astype(BF16)
        r_ref[...] = (r * jnp.exp2(cum)).astype(BF16)
        wt_ref[...] = jnp.exp2(sums[tl:])
    bon_ref[...] = _head_sum(r * kd_sum * r_k, e, et) * v
    g_ref[...] = _dot_x3(jax.nn.sigmoid(gl), gup_ref[0], gup_ref[1])


def rw_prep(p, cw, wup, aup, gup, vec, e, et, seq_len, tl, width):
    bsz, _, rw_cols = p.shape
    nb = seq_len // tl
    hb = tl // 8
    n_halo = seq_len // 8
    fixed = lambda a: pl.BlockSpec(a.shape, lambda b_, i: (0,) * a.ndim)
    row_spec = pl.BlockSpec((None, tl, width), lambda b_, i: (b_, i, 0))
    wt_spec = pl.BlockSpec((None, tl // 8, width), lambda b_, i: (b_, i, 0))
    act = lambda dt: jax.ShapeDtypeStruct((bsz, seq_len, width), dt)
    wt = jax.ShapeDtypeStruct((bsz, seq_len // 8, width), F32)
    per_dir_specs = [row_spec] * 4 + [wt_spec]
    per_dir_shapes = [act(BF16)] * 4 + [wt]
    return pl.pallas_call(
        functools.partial(_rw_prep_kernel, n_blocks=nb, width=width),
        grid=(bsz, nb),
        in_specs=[pl.BlockSpec((None, 8, rw_cols), lambda b_, i: (b_, jnp.maximum(i * hb - 1, 0), 0)),
                  pl.BlockSpec((None, tl, rw_cols), lambda b_, i: (b_, i, 0)),
                  pl.BlockSpec((None, 8, rw_cols), lambda b_, i: (b_, jnp.minimum((i + 1) * hb, n_halo - 1), 0)),
                  fixed(cw), fixed(wup), fixed(aup), fixed(gup), fixed(vec), fixed(e), fixed(et)],
        out_specs=per_dir_specs * 2 + [row_spec] * 3,
        out_shape=per_dir_shapes * 2 + [act(BF16), act(F32), act(F32)],
        compiler_params=_cparams("parallel", "parallel"),
        name="rw_prep",
    )(p, p, p, cw, wup, aup, gup, vec, e, et)


def _block_diag(y, diag):
    yb = y.astype(BF16)
    tiled = jnp.concatenate([yb] * RW_PROBLEMS, axis=0)
    return jnp.where(diag, tiled, jnp.zeros_like(tiled))


def _unit_tri_inverse(a, eye, blk, off1, off2, diag):
    n = RW_CHUNK
    mm = lambda xs, ys: _each(lambda x, y: _dot(x.astype(BF16), _block_diag(y, diag)), xs, ys)
    add = lambda xs, ys: _each(jnp.add, xs, ys)
    stack = lambda xs, ys: _each(lambda x, y: jnp.concatenate([x, y], axis=0), xs, ys)
    d = [ai * blk for ai in a]
    x = [eye + di for di in d]
    d2 = mm(d, d)
    both = mm(stack(x, d2), d2)
    x = _each(lambda xi, p: xi + p[:n], x, both)
    d4 = [p[n:] for p in both]
    both = mm(stack(x, d4), d4)
    x = _each(lambda xi, p: xi + p[:n], x, both)
    d8 = [p[n:] for p in both]
    x = add(x, mm(x, d8))
    x = add(x, mm(mm(x, [ai * off1 for ai in a]), x))
    x = add(x, mm(mm(x, [ai * off2 for ai in a]), x))
    return x


def _rw_chunk_local(a, bt, kt, r, v, masks):
    strict, incl, eye, blk, off1, off2, diag = masks
    n = RW_CHUNK
    bf = lambda xs: [x.astype(BF16) for x in xs]
    bdiag = lambda xs: [_block_diag(x, diag) for x in xs]
    stack = lambda xs, ys: _each(lambda x, y: jnp.concatenate([x, y], axis=0), xs, ys)
    ar = stack(a, r)
    sc = _each(_dot_nt, ar, stack(bdiag(bt), bdiag(kt)))
    a_ab = _each(lambda x, m: jnp.where(m, x[:n, :RW_CAT], 0.0), sc, strict)
    a_ak = _each(lambda x, m: jnp.where(m, x[:n, RW_CAT:], 0.0), sc, strict)
    r_rb = _each(lambda x, m: jnp.where(m, x[n:, :RW_CAT], 0.0), sc, incl)
    r_rk = _each(lambda x, m: jnp.where(m, x[n:, RW_CAT:], 0.0), sc, incl)
    inv = _unit_tri_inverse(a_ab, eye, blk, off1, off2, diag)
    kv = _each(_dot, bf(stack(a_ak, r_rk)), bdiag(v))
    return ar, bf(inv), bf(r_rb), kv


def _rw_chunk_state(local, bt, kt, v, wtot, s, diag):
    ar, inv, r_rb, kv = local
    n = RW_CHUNK
    bdiag = lambda xs: [_block_diag(x, diag) for x in xs]
    sc_s = _each(lambda x, si: _dot_nt(x, si.astype(BF16)), ar, s)
    rhs = _each(lambda x, y: x[:n] + y[:n], sc_s, kv)
    u = _each(_dot, inv, bdiag(rhs))
    y = _each(lambda x, p, q: x[n:] + p + q[n:], sc_s, _each(_dot, r_rb, bdiag(u)), kv)
    to_end = lambda x, w: (x.astype(F32) * w).astype(BF16)
    uv = _each(lambda x, z: jnp.concatenate([x.astype(BF16), z], axis=0), u, v)
    bk = _each(lambda x, z, w: jnp.concatenate([to_end(x, w), to_end(z, w)], axis=0), bt, kt, wtot)
    upd = _each(_dot_tn, uv, bk)
    return y, _each(lambda si, w, x: si * w + jnp.where(diag, x, 0.0), s, wtot, upd)


def _rw_scan_kernel(*refs):
    fwd, bwd = refs[0:6], refs[6:12]
    s0_ref, yf_ref, yb_ref, sout_ref, s_ref = refs[12:]
    c = pl.program_id(1)

    @pl.when(c == 0)
    def _():
        s_ref[...] = s0_ref[...]

    n = RW_CHUNK
    t = lax.broadcasted_iota(jnp.int32, (n, RW_CAT), 0)
    lane = lax.broadcasted_iota(jnp.int32, (n, RW_CAT), 1)
    step = lane % n
    eye = (t == step).astype(F32)
    blk = (t // RW_SUB == step // RW_SUB).astype(F32)
    off1 = ((t // (2 * RW_SUB) == step // (2 * RW_SUB)) & (t // RW_SUB != step // RW_SUB)).astype(F32)
    off2 = (t // (2 * RW_SUB) != step // (2 * RW_SUB)).astype(F32)
    di = lax.broadcasted_iota(jnp.int32, (RW_CAT, RW_CAT), 0) // n
    dj = lax.broadcasted_iota(jnp.int32, (RW_CAT, RW_CAT), 1) // n
    diag = di == dj
    pairs = s_ref.shape[1]
    lanes = [slice(g * LANES, (g + 1) * LANES) for g in range(pairs)]
    order = [(q, RW_STEP_CHUNKS - 1 - q) for q in range(RW_STEP_CHUNKS)]
    strict = ([t > step] * pairs + [t < step] * pairs) * RW_STEP_CHUNKS
    incl = ([t >= step] * pairs + [t <= step] * pairs) * RW_STEP_CHUNKS
    masks = (strict, incl, eye, blk, off1, off2, diag)
    rows = lambda q: slice(q * n, (q + 1) * n)
    ops = [[ref[rows(q), ln] for qs in order for ref, q in zip((f, b), qs) for ln in lanes]
           for f, b in zip(fwd[:5], bwd[:5])]
    wtot = [ref[8 * q:8 * q + 1, ln] for qs in order for ref, q in zip((fwd[5], bwd[5]), qs) for ln in lanes]
    local = _rw_chunk_local(*ops, masks)
    states = [s_ref[d, g] for d in range(2) for g in range(pairs)]
    per = 2 * pairs
    for i, qs in enumerate(order):
        part = slice(i * per, (i + 1) * per)
        ys, states = _rw_chunk_state([x[part] for x in local], ops[1][part], ops[2][part], ops[4][part], wtot[part],
                                     states, diag)
        for d, (y_ref, q) in enumerate(zip((yf_ref, yb_ref), qs)):
            for g in range(pairs):
                y_ref[rows(q), lanes[g]] = ys[d * pairs + g].astype(y_ref.dtype)
    for d in range(2):
        for g in range(pairs):
            s_ref[d, g] = states[d * pairs + g]

    @pl.when(c == pl.num_programs(1) - 1)
    def _():
        sout_ref[...] = s_ref[...]


def rw_scan(fwd, bwd, s0):
    bsz, seq_len, width = fwd[0].shape
    rows = RW_STEP_CHUNKS * RW_CHUNK
    nc = seq_len // rows
    pairs = width // LANES
    f_spec = pl.BlockSpec((None, rows, width), lambda b_, c: (b_, c, 0))
    b_spec = pl.BlockSpec((None, rows, width), lambda b_, c: (b_, nc - 1 - c, 0))
    fw_spec = pl.BlockSpec((None, rows // 8, width), lambda b_, c: (b_, c, 0))
    bw_spec = pl.BlockSpec((None, rows // 8, width), lambda b_, c: (b_, nc - 1 - c, 0))
    s_spec = pl.BlockSpec((None, 2, pairs, RW_CAT, RW_CAT), lambda b_, c: (b_, 0, 0, 0, 0))
    y_shape = jax.ShapeDtypeStruct((bsz, seq_len, width), BF16)
    return pl.pallas_call(
        _rw_scan_kernel,
        grid=(bsz, nc),
        in_specs=[f_spec] * 5 + [fw_spec] + [b_spec] * 5 + [bw_spec] + [s_spec],
        out_specs=[f_spec, b_spec, s_spec],
        out_shape=[y_shape, y_shape, jax.ShapeDtypeStruct(s0.shape, F32)],
        scratch_shapes=[pltpu.VMEM((2, pairs, RW_CAT, RW_CAT), F32)],
        compiler_params=_cparams("parallel", "arbitrary"),
        name="rw_scan",
    )(*fwd, *bwd, s0)


def _rw_finish_kernel(yf_ref, yb_ref, bon_ref, g_ref, lnw_ref, lnb_ref, e_ref, et_ref, o_ref):
    y = yf_ref[...].astype(F32) + yb_ref[...].astype(F32)
    e = e_ref[...]
    et = et_ref[...]
    mu = _head_sum(y, e, et) * (1.0 / RW_HEAD_DIM)
    yc = y - mu
    var = _head_sum(yc * yc, e, et) * (1.0 / RW_HEAD_DIM)
    yn = yc * lax.rsqrt(var + LNX_EPS) * lnw_ref[...] + lnb_ref[...]
    o_ref[...] = ((yn + bon_ref[...]) * g_ref[...]).astype(o_ref.dtype)


def rw_finish(yf, yb, bon, g, lnw, lnb, e, et, tl):
    m, width = yf.shape
    row = pl.BlockSpec((tl, width), lambda i: (i, 0))
    fixed = lambda a: pl.BlockSpec(a.shape, lambda i: (0, 0))
    return pl.pallas_call(
        _rw_finish_kernel,
        grid=(m // tl,),
        in_specs=[row, row, row, row, fixed(lnw), fixed(lnb), fixed(e), fixed(et)],
        out_specs=row,
        out_shape=jax.ShapeDtypeStruct((m, width), BF16),
        compiler_params=_cparams("parallel"),
        name="rw_finish",
    )(yf, yb, bon, g, lnw, lnb, e, et)


def _rope(x, cos, sin):
    quarter = LANES // 4
    width = x.shape[1]
    lane = lax.broadcasted_iota(jnp.int32, x.shape, 1)
    first = (lane // quarter) % 2 == 0
    partner = jnp.where(first, pltpu.roll(x, width - quarter, axis=1), pltpu.roll(x, quarter, axis=1))
    return x * cos + partner * sin


def _ret_scan_kernel(kf_ref, vf_ref, qf_ref, cf_ref, sf_ref, kb_ref, vb_ref, qb_ref, cb_ref, sb_ref,
                     dec_ref, fs_ref, te_ref, gc_ref, s0_ref, yf_ref, yb_ref, sout_ref, s_ref, *, rope, scale):
    c = pl.program_id(1)

    @pl.when(c == 0)
    def _():
        s_ref[...] = s0_ref[...]

    n_heads = s_ref.shape[1]
    heads = [slice(h * LANES, (h + 1) * LANES) for h in range(n_heads)]
    qs, ks, vs, decay, ss = [], [], [], [], []
    for d, (k_ref, v_ref, q_ref, cos_ref, sin_ref) in enumerate(
            ((kf_ref, vf_ref, qf_ref, cf_ref, sf_ref), (kb_ref, vb_ref, qb_ref, cb_ref, sb_ref))):
        q = q_ref[...].astype(F32)
        k = k_ref[...].astype(F32)
        if rope:
            cos = jnp.concatenate([cos_ref[...]] * n_heads, axis=1)
            sin = jnp.concatenate([sin_ref[...]] * n_heads, axis=1)
            q = _rope(q, cos, sin)
            k = _rope(k, cos, sin)
        k = k * scale
        v = v_ref[...].astype(BF16)
        q_cross = (q * fs_ref[d]).astype(BF16)
        k_end = (k * te_ref[d]).astype(BF16)
        q = q.astype(BF16)
        k = k.astype(BF16)
        for h, hs in enumerate(heads):
            qs.append((q[:, hs], q_cross[:, hs]))
            ks.append((k[:, hs], k_end[:, hs]))
            vs.append(v[:, hs])
            decay.append(dec_ref[d, h])
            ss.append(s_ref[d, h])
    scores = _each(lambda q, k, dm: (_dot_nt(q[0], k[0]) * dm).astype(BF16), qs, ks, decay)
    cross = _each(lambda q, s: _dot(q[1], s.astype(BF16)), qs, ss)
    intra = _each(_dot, scores, vs)
    upd = _each(lambda k, v: _dot_tn(k[1], v), ks, vs)
    for d, y_ref in enumerate((yf_ref, yb_ref)):
        for h, hs in enumerate(heads):
            i = d * n_heads + h
            y_ref[:, hs] = (intra[i] + cross[i]).astype(y_ref.dtype)
            s_ref[d, h] = gc_ref[h] * ss[i] + upd[i]

    @pl.when(c == pl.num_programs(1) - 1)
    def _():
        sout_ref[...] = s_ref[...]


def ret_scan(p, cos, sin, tables, s0, rope):
    bsz, seq_len, cols = p.shape
    width = cols // 4
    nc = seq_len // RET_CHUNK
    dec, fs, te, gc = tables

    def specs(chunk):
        cols_ = [pl.BlockSpec((None, RET_CHUNK, width), lambda b_, c, o=o: (b_, chunk(c), o)) for o in range(3)]
        return cols_ + [pl.BlockSpec((RET_CHUNK, LANES), lambda b_, c: (chunk(c), 0))] * 2

    fwd = lambda c: c
    bwd = lambda c: nc - 1 - c
    fixed = lambda a: pl.BlockSpec(a.shape, lambda b_, c: (0,) * a.ndim)
    s_spec = pl.BlockSpec((None,) + s0.shape[1:], lambda b_, c: (b_, 0, 0, 0, 0))
    y_shape = jax.ShapeDtypeStruct((bsz, seq_len, width), BF16)
    return pl.pallas_call(
        functools.partial(_ret_scan_kernel, rope=rope, scale=float(LANES) ** -0.5),
        grid=(bsz, nc),
        in_specs=specs(fwd) + specs(bwd) + [fixed(dec), fixed(fs), fixed(te), fixed(gc), s_spec],
        out_specs=[pl.BlockSpec((None, RET_CHUNK, width), lambda b_, c: (b_, c, 0)),
                   pl.BlockSpec((None, RET_CHUNK, width), lambda b_, c: (b_, nc - 1 - c, 0)),
                   s_spec],
        out_shape=[y_shape, y_shape, jax.ShapeDtypeStruct(s0.shape, F32)],
        scratch_shapes=[pltpu.VMEM(s0.shape[1:], F32)],
        compiler_params=_cparams("parallel", "arbitrary"),
        name="ret_scan",
    )(p, p, p, cos, sin, p, p, p, cos, sin, dec, fs, te, gc, s0)


def _ret_tables(n_heads):
    n = RET_CHUNK
    lg = np.log(1.0 - 2.0 ** (-5.0 - np.arange(n_heads, dtype=np.float64)))
    pos = np.arange(n, dtype=np.float64)
    diff = pos[:, None] - pos[None, :]
    masks = (diff >= 0, diff < 0)
    dists = (diff, -diff)
    dec = np.stack([np.where(m, np.exp(np.where(m, dd, 0.0)[None] * lg[:, None, None]), 0.0)
                    for m, dd in zip(masks, dists)])
    done = np.stack([pos, n - 1.0 - pos])
    lanes = np.repeat(lg, LANES)[None, None, :]
    fs = np.exp((done[:, :, None] + 1.0) * lanes)
    te = np.exp((n - 1.0 - done[:, :, None]) * lanes)
    gc = np.broadcast_to(np.exp(n * lg)[:, None, None], (n_heads, 1, LANES))
    return tuple(jnp.asarray(a, F32) for a in (dec, fs, te, gc))


def _ret_finish_kernel(yf_ref, yb_ref, g_ref, o_ref):
    g = g_ref[...].astype(F32)
    gate = g * jax.nn.sigmoid(g)
    for h in range(o_ref.shape[1] // LANES):
        hs = slice(h * LANES, (h + 1) * LANES)
        y = yf_ref[:, hs].astype(F32) + yb_ref[:, hs].astype(F32)
        y = y * lax.rsqrt(jnp.mean(y * y, axis=-1, keepdims=True) + EPS)
        o_ref[:, hs] = (y * gate[:, hs]).astype(o_ref.dtype)


def ret_finish(yf, yb, p2, tl):
    m, width = yf.shape
    row = pl.BlockSpec((tl, width), lambda i: (i, 0))
    return pl.pallas_call(
        _ret_finish_kernel,
        grid=(m // tl,),
        in_specs=[row, row, pl.BlockSpec((tl, width), lambda i: (i, 3))],
        out_specs=row,
        out_shape=jax.ShapeDtypeStruct((m, width), BF16),
        compiler_params=_cparams("parallel"),
        name="ret_finish",
    )(yf, yb, p2)


def _rope_tables(seq_len):
    n = LANES // 4
    t = np.arange(seq_len)
    inv = ROPE_BASE ** (-np.arange(n, dtype=np.float64) / n)
    ang_row = (t // GRID_W)[:, None] * inv
    ang_col = (t % GRID_W)[:, None] * inv
    cr, sr, cc, sc = np.cos(ang_row), np.sin(ang_row), np.cos(ang_col), np.sin(ang_col)
    cos = np.concatenate([cr, cr, cc, cc], axis=-1)
    sin = np.concatenate([-sr, sr, -sc, sc], axis=-1)
    return jnp.asarray(cos, F32), jnp.asarray(sin, F32)


def _pad_rows(w, rows, at):
    return jnp.zeros((rows, w.shape[1]), w.dtype).at[at:at + w.shape[0]].set(w)


def kernel(x, c, ctx, c_ctx, w_ada, b_ada, norm_pre_mix, norm_post_mix, norm_pre_ffn, norm_post_ffn, w_in, rw_conv, rw_w0, rw_w_up, rw_a0, rw_a_up, rw_g_up, rw_k_k, rw_k_a, rw_r_k, rw_lnx_w, rw_lnx_b, w_out, ffn_w_gate, ffn_w_up, ffn_conv, ffn_conv_b, ffn_w_down):
    bsz, seq_len, d = x.shape
    ctx_len = ctx.shape[1]
    n_layers = w_ada.shape[0]
    assert n_layers == 1, "context-stream outputs are only needed between layers"
    rw_w = rw_k_k.shape[1]
    ret_w = w_out.shape[1] - rw_w
    rw_cols = 3 * rw_w + 2 * DECAY_LORA + 2 * AAA_LORA + GATE_LORA
    rw_pad = -rw_cols % (2 * LANES)
    d_ff = ffn_w_gate.shape[2]
    m = bsz * seq_len
    flat = lambda a: a.reshape(-1, a.shape[-1])
    l = 0

    cond = jnp.zeros((8, d), F32).at[:bsz].set(c).at[bsz].set(c_ctx)
    mod = ada_modulation(cond, w_ada[l], b_ada[l][None])
    sh1, sc1, g1, sh2, sc2, g2 = [mm[:bsz, None, :] for mm in jnp.split(mod, 6, axis=-1)]
    sh_c = jnp.broadcast_to(mod[bsz, :d], (bsz, 1, d))
    sc_c = jnp.broadcast_to(mod[bsz, d:2 * d], (bsz, 1, d))

    w_in_t = w_in[l].T
    w_rw_t = w_in_t[:rw_cols + rw_pad].astype(BF16)
    w_ret_t = w_in_t[rw_cols:].astype(BF16)
    g_pre = norm_pre_mix[l][None]

    def in_proj(a, sc, sh, rows, tm):
        p_a, xm = norm_mod_matmul(flat(a), g_pre, sc, sh, w_rw_t, rows, tm, TN_IN_RW)
        p_b = matmul_nt(xm, w_ret_t, tm, TN_IN_RET, BF16)
        return p_a.reshape(bsz, rows, -1), p_b.reshape(bsz, rows, -1)

    p_rw, p_ret = in_proj(x, sc1, sh1, seq_len, min(TM_IN, seq_len))
    pc_rw, pc_ret = in_proj(ctx, sc_c, sh_c, ctx_len, ctx_len)

    cw = jnp.pad(rw_conv[l], ((0, 0), (0, rw_pad)))
    lora_rows = 2 * DECAY_LORA + 2 * AAA_LORA
    split = lambda w: jnp.stack(_bf16_terms(w, 2))
    wup = split(jnp.stack([_pad_rows(rw_w_up[l][dd], lora_rows, dd * DECAY_LORA) for dd in range(2)]))
    aup = split(jnp.stack([_pad_rows(rw_a_up[l][dd], lora_rows, 2 * DECAY_LORA + dd * AAA_LORA) for dd in range(2)]))
    gup = split(_pad_rows(rw_g_up[l], 256, 0))
    vec = jnp.concatenate([rw_w0[l], rw_a0[l], rw_k_k[l][None], rw_k_a[l][None], rw_r_k[l].reshape(1, rw_w)], axis=0)
    vec = jnp.pad(vec, ((0, 1), (0, 0)))
    head_of_lane = jnp.arange(rw_w) // RW_HEAD_DIM
    e = (head_of_lane[:, None] == jnp.arange(LANES)[None, :]).astype(BF16)
    et = e.T
    prep = functools.partial(rw_prep, cw=cw, wup=wup, aup=aup, gup=gup, vec=vec, e=e, et=et, width=rw_w)
    scan_args = lambda o: (o[0:4] + [o[10], o[4]], o[5:9] + [o[10], o[9]])
    s0 = jnp.zeros((bsz, 2, rw_w // LANES, RW_CAT, RW_CAT), F32)
    outs = list(prep(pc_rw, seq_len=ctx_len, tl=min(TL_PREP, ctx_len)))
    _, _, s_rw = rw_scan(*scan_args(outs), s0)
    outs = list(prep(p_rw, seq_len=seq_len, tl=TL_PREP))
    yf, yb, _ = rw_scan(*scan_args(outs), s_rw)
    o_rw = rw_finish(flat(yf), flat(yb), flat(outs[11]), flat(outs[12]), rw_lnx_w[l][None], rw_lnx_b[l][None],
                     e, et, TL_FINISH)

    n_ret_heads = ret_w // LANES
    tables = _ret_tables(n_ret_heads)
    cos, sin = _rope_tables(seq_len)
    s0 = jnp.zeros((bsz, 2, n_ret_heads, LANES, LANES), F32)
    _, _, s_ret = ret_scan(pc_ret, cos[:ctx_len], sin[:ctx_len], tables, s0, rope=False)
    yf, yb, _ = ret_scan(p_ret, cos, sin, tables, s_ret, rope=True)
    o_ret = ret_finish(flat(yf), flat(yb), flat(p_ret), TL_FINISH)

    w_o = w_out[l].astype(BF16)
    x1, h = out_proj_residual(o_rw, o_ret, w_o[:rw_w], w_o[rw_w:], flat(x), g1, norm_post_mix[l][None],
                              norm_pre_ffn[l][None], sc2, sh2, seq_len, min(TM_OUT, seq_len))
    t = ffn_up(h, ffn_w_gate[l], ffn_w_up[l], ffn_conv[l].reshape(9, d_ff),
               ffn_conv_b[l][None], seq_len, min(TM_FFN, seq_len), TN_FFN)
    out = down_proj_residual(t, ffn_w_down[l].astype(BF16), x1, g2, norm_post_ffn[l][None], seq_len,
                             min(TM_DOWN, seq_len), TK_DOWN)
    return out.reshape(bsz, seq_len, d)
```

```python
import functools

import jax
import jax.numpy as jnp
import numpy as np
from jax import lax
from jax.experimental import pallas as pl
from jax.experimental.pallas import tpu as pltpu

F32 = jnp.float32
BF16 = jnp.bfloat16

LANES = 128
EPS = 1e-6
GRID_W = 64
RW_HEAD_DIM = 64
DECAY_LORA = 64
AAA_LORA = 64
GATE_LORA = 160
LNX_EPS = 64e-5
RET_CHUNK = 128
ROPE_BASE = 10000.0
LOG2_E = 1.4426950408889634
RW_CHUNK = 64
RW_SUB = 16
RW_PROBLEMS = 2
RW_CAT = RW_PROBLEMS * RW_CHUNK
RW_STEP_CHUNKS = 4
VMEM_LIMIT = 56 * 1024 * 1024

TM_IN, TN_IN_RW, TN_IN_RET, TM_OUT, TM_FFN, TN_FFN, TM_DOWN, TK_DOWN = 1024, 512, 1024, 512, 1024, 512, 1024, 1408
TL_PREP, TL_FINISH = 256, 512


def _cparams(*sem):
    return pltpu.CompilerParams(dimension_semantics=sem, vmem_limit_bytes=VMEM_LIMIT)


def _dot(a, b, prec=None):
    return jnp.dot(a, b, precision=prec, preferred_element_type=F32)


def _dot_nt(a, b, prec=None):
    return lax.dot_general(a, b, (((1,), (1,)), ((), ())), precision=prec, preferred_element_type=F32)


def _dot_tn(a, b, prec=None):
    return lax.dot_general(a, b, (((0,), (0,)), ((), ())), precision=prec, preferred_element_type=F32)


def _bf16_terms(a, n):
    terms = []
    for _ in range(n):
        t = a.astype(BF16)
        terms.append(t)
        a = a - t.astype(F32)
    return terms


def _dot_sel(sel, x, left=True):
    hi, lo = _bf16_terms(x, 2)
    return (_dot(sel, hi) + _dot(sel, lo)) if left else (_dot(hi, sel) + _dot(lo, sel))


def _dot_x3(a, b_hi, b_lo):
    ah, al = _bf16_terms(a, 2)
    return _dot(ah, b_hi) + (_dot(ah, b_lo) + _dot(al, b_hi))


def _rms(x, g):
    return x * lax.rsqrt(jnp.mean(x * x, axis=-1, keepdims=True) + EPS) * g


def _each(f, *lists):
    return [f(*xs) for xs in zip(*lists)]


def _ada_kernel(s_ref, w_ref, b_ref, o_ref):
    s = s_ref[...]
    s = s * jax.nn.sigmoid(s)
    w_hi, w_lo = _bf16_terms(w_ref[...], 2)
    o_ref[...] = _dot_x3(s, w_hi, w_lo) + b_ref[...]


def ada_modulation(s, w, b, tn=1024):
    m, d = s.shape
    n = w.shape[1]
    return pl.pallas_call(
        _ada_kernel,
        grid=(n // tn,),
        in_specs=[pl.BlockSpec((m, d), lambda j: (0, 0)),
                  pl.BlockSpec((d, tn), lambda j: (0, j)),
                  pl.BlockSpec((1, tn), lambda j: (0, j))],
        out_specs=pl.BlockSpec((m, tn), lambda j: (0, j)),
        out_shape=jax.ShapeDtypeStruct((m, n), F32),
        compiler_params=_cparams("arbitrary"),
        name="ada_modulation",
    )(s, w, b)


def _norm_mm_kernel(x_ref, g_ref, sc_ref, sh_ref, w_ref, o_ref, xm_ref):
    j = pl.program_id(1)
    w = w_ref[...]

    @pl.when(j == 0)
    def _():
        half = x_ref.shape[0] // 2
        for rows in (slice(0, half), slice(half, 2 * half)):
            y = _rms(x_ref[rows], g_ref[...])
            xm = (y * (1.0 + sc_ref[0]) + sh_ref[0]).astype(BF16)
            xm_ref[rows] = xm
            o_ref[rows] = _dot_nt(xm, w)

    @pl.when(j > 0)
    def _():
        o_ref[...] = _dot_nt(xm_ref[...], w)


def norm_mod_matmul(x, g, sc, sh, wt, n, rows_per_batch, tm, tn):
    m, d = x.shape
    bpb = rows_per_batch // tm
    return pl.pallas_call(
        _norm_mm_kernel,
        grid=(m // tm, n // tn),
        in_specs=[pl.BlockSpec((tm, d), lambda i, j: (i, 0)),
                  pl.BlockSpec((1, d), lambda i, j: (0, 0)),
                  pl.BlockSpec((1, 1, d), lambda i, j: (i // bpb, 0, 0)),
                  pl.BlockSpec((1, 1, d), lambda i, j: (i // bpb, 0, 0)),
                  pl.BlockSpec((tn, d), lambda i, j: (j, 0))],
        out_specs=[pl.BlockSpec((tm, tn), lambda i, j: (i, j)), pl.BlockSpec((tm, d), lambda i, j: (i, 0))],
        out_shape=[jax.ShapeDtypeStruct((m, n), F32), jax.ShapeDtypeStruct((m, d), BF16)],
        compiler_params=_cparams("parallel", "arbitrary"),
        name="norm_mod_matmul",
    )(x, g, sc, sh, wt)


def _mm_kernel(a_ref, w_ref, o_ref):
    o_ref[...] = _dot_nt(a_ref[...], w_ref[...]).astype(o_ref.dtype)


def matmul_nt(a, wt, row0, n, tm, tn, out_dtype):
    m, k = a.shape
    assert row0 % 16 == 0 and row0 + n <= wt.shape[0]
    return pl.pallas_call(
        _mm_kernel,
        grid=(m // tm, n // tn),
        in_specs=[pl.BlockSpec((tm, k), lambda i, j: (i, 0)),
                  pl.BlockSpec((pl.Element(tn), pl.Element(k)),
                               lambda i, j: (pl.multiple_of(row0 + j * tn, 16), 0))],
        out_specs=pl.BlockSpec((tm, tn), lambda i, j: (i, j)),
        out_shape=jax.ShapeDtypeStruct((m, n), out_dtype),
        compiler_params=_cparams("parallel", "arbitrary"),
        name="matmul_nt",
    )(a, wt)


def _out_proj_kernel(oa_ref, ob_ref, wa_ref, wb_ref, x_ref, g1_ref, gpost_ref, gpre_ref, sc2_ref, sh2_ref,
                     x1_ref, h_ref):
    out = _dot(oa_ref[...], wa_ref[...]) + _dot(ob_ref[...], wb_ref[...])
    x1 = x_ref[...] + g1_ref[0] * _rms(out, gpost_ref[...])
    x1_ref[...] = x1
    h_ref[...] = (_rms(x1, gpre_ref[...]) * (1.0 + sc2_ref[0]) + sh2_ref[0]).astype(BF16)


def out_proj_residual(oa, ob, wa, wb, x, g1, gpost, gpre, sc2, sh2, rows_per_batch, tm):
    m, ka = oa.shape
    kb = ob.shape[1]
    d = x.shape[1]
    bpb = rows_per_batch // tm
    row = lambda i: (i, 0)
    fixed = lambda i: (0, 0)
    per_b = lambda i: (i // bpb, 0, 0)
    return pl.pallas_call(
        _out_proj_kernel,
        grid=(m // tm,),
        in_specs=[pl.BlockSpec((tm, ka), row), pl.BlockSpec((tm, kb), row),
                  pl.BlockSpec((ka, d), fixed), pl.BlockSpec((kb, d), fixed),
                  pl.BlockSpec((tm, d), row),
                  pl.BlockSpec((1, 1, d), per_b),
                  pl.BlockSpec((1, d), fixed), pl.BlockSpec((1, d), fixed),
                  pl.BlockSpec((1, 1, d), per_b), pl.BlockSpec((1, 1, d), per_b)],
        out_specs=[pl.BlockSpec((tm, d), row), pl.BlockSpec((tm, d), row)],
        out_shape=[jax.ShapeDtypeStruct((m, d), F32), jax.ShapeDtypeStruct((m, d), BF16)],
        compiler_params=_cparams("parallel"),
        name="out_proj_residual",
    )(oa, ob, wa, wb, x, g1, gpost, gpre, sc2, sh2)


def _ffn_up_kernel(top_ref, mid_ref, bot_ref, wg_ref, wu_ref, cw_ref, cb_ref, o_ref, hext_ref, *, n_blocks):
    i = pl.program_id(1)
    tm = mid_ref.shape[0]

    @pl.when(pl.program_id(2) == 0)
    def _():
        hext_ref[0:GRID_W] = top_ref[...]
        hext_ref[GRID_W:GRID_W + tm] = mid_ref[...]
        hext_ref[GRID_W + tm:] = bot_ref[...]

    gate = _dot(hext_ref[...], wg_ref[...].astype(BF16))
    up = _dot(mid_ref[...], wu_ref[...].astype(BF16))
    tn = gate.shape[1]
    mid = gate[GRID_W:GRID_W + tm]
    above = jnp.where(i > 0, gate[:GRID_W], 0.0)
    below = jnp.where(i < n_blocks - 1, gate[GRID_W + tm:], 0.0)
    above = jnp.concatenate([above, mid[:tm - GRID_W]], axis=0)
    below = jnp.concatenate([mid[GRID_W:], below], axis=0)
    w = cw_ref[...]
    col = lax.broadcasted_iota(jnp.int32, (tm, tn), 0) % GRID_W

    def column_sum(dx):
        return above * w[dx:dx + 1] + mid * w[3 + dx:4 + dx] + below * w[6 + dx:7 + dx]

    left = jnp.where(col > 0, pltpu.roll(column_sum(0), 1, axis=0), 0.0)
    right = jnp.where(col < GRID_W - 1, pltpu.roll(column_sum(2), tm - 1, axis=0), 0.0)
    gt = column_sum(1) + left + right + cb_ref[...]
    o_ref[...] = (gt * jax.nn.sigmoid(gt) * up).astype(o_ref.dtype)


def ffn_up(h, wg, wu, w9, b, seq_len, tm, tn):
    m, d = h.shape
    f = wg.shape[1]
    nb = seq_len // tm
    bsz = m // seq_len
    hpb = tm // GRID_W
    n_halo = seq_len // GRID_W
    main = lambda b_, i, j: (b_ * nb + i, 0)
    top = lambda b_, i, j: (b_ * n_halo + jnp.maximum(i * hpb - 1, 0), 0)
    bot = lambda b_, i, j: (b_ * n_halo + jnp.minimum((i + 1) * hpb, n_halo - 1), 0)
    col = lambda b_, i, j: (0, j)
    return pl.pallas_call(
        functools.partial(_ffn_up_kernel, n_blocks=nb),
        grid=(bsz, nb, f // tn),
        in_specs=[pl.BlockSpec((GRID_W, d), top), pl.BlockSpec((tm, d), main), pl.BlockSpec((GRID_W, d), bot),
                  pl.BlockSpec((d, tn), col), pl.BlockSpec((d, tn), col),
                  pl.BlockSpec((9, tn), col), pl.BlockSpec((1, tn), col)],
        out_specs=pl.BlockSpec((tm, tn), lambda b_, i, j: (b_ * nb + i, j)),
        out_shape=jax.ShapeDtypeStruct((m, f), BF16),
        scratch_shapes=[pltpu.VMEM((tm + 2 * GRID_W, d), BF16)],
        compiler_params=_cparams("parallel", "parallel", "arbitrary"),
        name="ffn_up",
    )(h, h, h, wg, wu, w9, b)


def _down_proj_kernel(t_ref, w_ref, x1_ref, g2_ref, gpost_ref, o_ref):
    k = pl.program_id(1)
    last = pl.num_programs(1) - 1

    @pl.when(k == 0)
    def _():
        o_ref[...] = _dot(t_ref[...], w_ref[...])

    @pl.when((k > 0) & (k < last))
    def _():
        o_ref[...] += _dot(t_ref[...], w_ref[...])

    @pl.when(k == last)
    def _():
        half = o_ref.shape[0] // 2
        for rows in (slice(0, half), slice(half, 2 * half)):
            acc = o_ref[rows] + _dot(t_ref[rows], w_ref[...])
            o_ref[rows] = x1_ref[rows] + g2_ref[0] * _rms(acc, gpost_ref[...])


def down_proj_residual(t, w, x1, g2, gpost, rows_per_batch, tm, tk):
    m, kk = t.shape
    d = w.shape[1]
    bpb = rows_per_batch // tm
    return pl.pallas_call(
        _down_proj_kernel,
        grid=(m // tm, kk // tk),
        in_specs=[pl.BlockSpec((tm, tk), lambda i, k: (i, k)),
                  pl.BlockSpec((tk, d), lambda i, k: (k, 0)),
                  pl.BlockSpec((tm, d), lambda i, k: (i, 0)),
                  pl.BlockSpec((1, 1, d), lambda i, k: (i // bpb, 0, 0)),
                  pl.BlockSpec((1, d), lambda i, k: (0, 0))],
        out_specs=pl.BlockSpec((tm, d), lambda i, k: (i, 0)),
        out_shape=jax.ShapeDtypeStruct((m, d), F32),
        compiler_params=_cparams("parallel", "arbitrary"),
        name="down_proj_residual",
    )(t, w, x1, g2, gpost)


def _head_sum(x, e, et):
    return _dot_sel(et, _dot_sel(e, x, left=False), left=False)


def _rw_prep_kernel(prev_ref, cur_ref, next_ref, cw_ref, wup_ref, aup_ref, gup_ref, vec_ref, e_ref, et_ref,
                    af_ref, btf_ref, ktf_ref, rf_ref, wtf_ref, ab_ref, btb_ref, ktb_ref, rb_ref, wtb_ref,
                    v_ref, bon_ref, g_ref, *, n_blocks, width):
    i = pl.program_id(1)
    tl = cur_ref.shape[0]
    cur = cur_ref[...]
    rows = lax.broadcasted_iota(jnp.int32, cur.shape, 0)
    prev_row = jnp.where(i > 0, prev_ref[7:8, :], 0.0)
    next_row = jnp.where(i < n_blocks - 1, next_ref[0:1, :], 0.0)
    before = jnp.where(rows == 0, prev_row, pltpu.roll(cur, 1, axis=0))
    after = jnp.where(rows == tl - 1, next_row, pltpu.roll(cur, tl - 1, axis=0))
    cw = cw_ref[...]
    c = before * cw[0:1] + cur * cw[1:2] + after * cw[2:3]

    w_ = width
    k = c[:, :w_]
    v = c[:, w_:2 * w_]
    lo = c[:, 2 * w_:2 * w_ + 256]
    r = c[:, 2 * w_ + 256:3 * w_ + 256]
    gl = c[:, 3 * w_ + 256:3 * w_ + 512]
    vec = vec_ref[...]
    e = e_ref[...]
    et = et_ref[...]
    k_k, k_a, r_k = vec[4:5], vec[5:6], vec[6:7]

    kk = k * k_k
    kk = kk * lax.rsqrt(_head_sum(kk * kk, e, et) + 1e-12)
    neg_kk = -kk
    one_minus_k_a = 1.0 - k_a
    v_ref[...] = v.astype(BF16)
    tlo = jnp.tanh(lo)
    kd_sum = jnp.zeros_like(k)
    ti = lax.broadcasted_iota(jnp.int32, (tl, tl), 0)
    tj = lax.broadcasted_iota(jnp.int32, (tl, tl), 1)
    same_chunk = ti // RW_CHUNK == tj // RW_CHUNK
    wi = lax.broadcasted_iota(jnp.int32, (tl // 8, tl), 0)
    wj = lax.broadcasted_iota(jnp.int32, (tl // 8, tl), 1)
    chunk_rows = (wi * 8) // RW_CHUNK == wj // RW_CHUNK
    dirs = ((af_ref, btf_ref, ktf_ref, rf_ref, wtf_ref, tj <= ti), (ab_ref, btb_ref, ktb_ref, rb_ref, wtb_ref, tj >= ti))
    for d, (a_ref, bt_ref, kt_ref, r_ref, wt_ref, done) in enumerate(dirs):
        z = vec[d:d + 1] + _dot_x3(tlo, wup_ref[0, d], wup_ref[1, d])
        nz = -z
        softplus = jnp.maximum(nz, 0.0) + jnp.log1p(jnp.exp(-jnp.abs(nz)))
        w_log = -softplus - 0.5
        lw = jnp.exp(w_log) * -LOG2_E
        a = jax.nn.sigmoid(vec[2 + d:3 + d] + _dot_x3(lo, aup_ref[0, d], aup_ref[1, d]))
        kd = k * (a * k_a + one_minus_k_a)
        kd_sum = kd_sum + kd
        sel = jnp.concatenate([(same_chunk & done).astype(F32), chunk_rows.astype(F32)], axis=0).astype(BF16)
        sums = _dot_sel(sel, lw)
        cum = sums[:tl]
        w_inv = jnp.exp2(-cum)
        a_ref[...] = (neg_kk * jnp.exp2(cum - lw)).astype(BF16)
        bt_ref[...] = (kk * a * w_inv).astype(BF16)
        kt_ref[...] = (kd * w_inv).astype(BF16)
        r_ref[...] = (r * jnp.exp2(cum)).astype(BF16)
        wt_ref[...] = jnp.exp2(sums[tl:])
    bon_ref[...] = _head_sum(r * kd_sum * r_k, e, et) * v
    g_ref[...] = _dot_x3(jax.nn.sigmoid(gl), gup_ref[0], gup_ref[1])


def rw_prep(p, cw, wup, aup, gup, vec, e, et, seq_len, tl, width):
    bsz, _, rw_cols = p.shape
    nb = seq_len // tl
    hb = tl // 8
    n_halo = seq_len // 8
    fixed = lambda a: pl.BlockSpec(a.shape, lambda b_, i: (0,) * a.ndim)
    row_spec = pl.BlockSpec((None, tl, width), lambda b_, i: (b_, i, 0))
    wt_spec = pl.BlockSpec((None, tl // 8, width), lambda b_, i: (b_, i, 0))
    act = lambda dt: jax.ShapeDtypeStruct((bsz, seq_len, width), dt)
    wt = jax.ShapeDtypeStruct((bsz, seq_len // 8, width), F32)
    per_dir_specs = [row_spec] * 4 + [wt_spec]
    per_dir_shapes = [act(BF16)] * 4 + [wt]
    return pl.pallas_call(
        functools.partial(_rw_prep_kernel, n_blocks=nb, width=width),
        grid=(bsz, nb),
        in_specs=[pl.BlockSpec((None, 8, rw_cols), lambda b_, i: (b_, jnp.maximum(i * hb - 1, 0), 0)),
                  pl.BlockSpec((None, tl, rw_cols), lambda b_, i: (b_, i, 0)),
                  pl.BlockSpec((None, 8, rw_cols), lambda b_, i: (b_, jnp.minimum((i + 1) * hb, n_halo - 1), 0)),
                  fixed(cw), fixed(wup), fixed(aup), fixed(gup), fixed(vec), fixed(e), fixed(et)],
        out_specs=per_dir_specs * 2 + [row_spec] * 3,
        out_shape=per_dir_shapes * 2 + [act(BF16), act(F32), act(F32)],
        compiler_params=_cparams("parallel", "parallel"),
        name="rw_prep",
    )(p, p, p, cw, wup, aup, gup, vec, e, et)


def _block_diag(y, diag):
    yb = y.astype(BF16)
    tiled = jnp.concatenate([yb] * RW_PROBLEMS, axis=0)
    return jnp.where(diag, tiled, jnp.zeros_like(tiled))


def _unit_tri_inverse(a, eye, blk, off1, off2, diag):
    n = RW_CHUNK
    mm = lambda xs, ys: _each(lambda x, y: _dot(x.astype(BF16), _block_diag(y, diag)), xs, ys)
    add = lambda xs, ys: _each(jnp.add, xs, ys)
    stack = lambda xs, ys: _each(lambda x, y: jnp.concatenate([x, y], axis=0), xs, ys)
    d = [ai * blk for ai in a]
    x = [eye + di for di in d]
    d2 = mm(d, d)
    both = mm(stack(x, d2), d2)
    x = _each(lambda xi, p: xi + p[:n], x, both)
    d4 = [p[n:] for p in both]
    both = mm(stack(x, d4), d4)
    x = _each(lambda xi, p: xi + p[:n], x, both)
    d8 = [p[n:] for p in both]
    x = add(x, mm(x, d8))
    x = add(x, mm(mm(x, [ai * off1 for ai in a]), x))
    x = add(x, mm(mm(x, [ai * off2 for ai in a]), x))
    return x


def _rw_chunk_local(a, bt, kt, r, v, masks):
    strict, incl, eye, blk, off1, off2, diag = masks
    n = RW_CHUNK
    bf = lambda xs: [x.astype(BF16) for x in xs]
    bdiag = lambda xs: [_block_diag(x, diag) for x in xs]
    stack = lambda xs, ys: _each(lambda x, y: jnp.concatenate([x, y], axis=0), xs, ys)
    ar = stack(a, r)
    sc = _each(_dot_nt, ar, stack(bdiag(bt), bdiag(kt)))
    a_ab = _each(lambda x, m: jnp.where(m, x[:n, :RW_CAT], 0.0), sc, strict)
    a_ak = _each(lambda x, m: jnp.where(m, x[:n, RW_CAT:], 0.0), sc, strict)
    r_rb = _each(lambda x, m: jnp.where(m, x[n:, :RW_CAT], 0.0), sc, incl)
    r_rk = _each(lambda x, m: jnp.where(m, x[n:, RW_CAT:], 0.0), sc, incl)
    inv = _unit_tri_inverse(a_ab, eye, blk, off1, off2, diag)
    kv = _each(_dot, bf(stack(a_ak, r_rk)), bdiag(v))
    return ar, bf(inv), bf(r_rb), kv


def _rw_chunk_state(local, bt, kt, v, wtot, s, diag):
    ar, inv, r_rb, kv = local
    n = RW_CHUNK
    bdiag = lambda xs: [_block_diag(x, diag) for x in xs]
    sc_s = _each(lambda x, si: _dot_nt(x, si.astype(BF16)), ar, s)
    rhs = _each(lambda x, y: x[:n] + y[:n], sc_s, kv)
    u = _each(_dot, inv, bdiag(rhs))
    y = _each(lambda x, p, q: x[n:] + p + q[n:], sc_s, _each(_dot, r_rb, bdiag(u)), kv)
    to_end = lambda x, w: (x.astype(F32) * w).astype(BF16)
    uv = _each(lambda x, z: jnp.concatenate([x.astype(BF16), z], axis=0), u, v)
    bk = _each(lambda x, z, w: jnp.concatenate([to_end(x, w), to_end(z, w)], axis=0), bt, kt, wtot)
    upd = _each(_dot_tn, uv, bk)
    return y, _each(lambda si, w, x: si * w + jnp.where(diag, x, 0.0), s, wtot, upd)


def _rw_scan_kernel(*refs):
    fwd, bwd = refs[0:6], refs[6:12]
    s0_ref, yf_ref, yb_ref, sout_ref, s_ref = refs[12:]
    c = pl.program_id(1)

    @pl.when(c == 0)
    def _():
        s_ref[...] = s0_ref[...]

    n = RW_CHUNK
    t = lax.broadcasted_iota(jnp.int32, (n, RW_CAT), 0)
    lane = lax.broadcasted_iota(jnp.int32, (n, RW_CAT), 1)
    step = lane % n
    eye = (t == step).astype(F32)
    blk = (t // RW_SUB == step // RW_SUB).astype(F32)
    off1 = ((t // (2 * RW_SUB) == step // (2 * RW_SUB)) & (t // RW_SUB != step // RW_SUB)).astype(F32)
    off2 = (t // (2 * RW_SUB) != step // (2 * RW_SUB)).astype(F32)
    di = lax.broadcasted_iota(jnp.int32, (RW_CAT, RW_CAT), 0) // n
    dj = lax.broadcasted_iota(jnp.int32, (RW_CAT, RW_CAT), 1) // n
    diag = di == dj
    pairs = s_ref.shape[1]
    lanes = [slice(g * LANES, (g + 1) * LANES) for g in range(pairs)]
    order = [(q, RW_STEP_CHUNKS - 1 - q) for q in range(RW_STEP_CHUNKS)]
    strict = ([t > step] * pairs + [t < step] * pairs) * RW_STEP_CHUNKS
    incl = ([t >= step] * pairs + [t <= step] * pairs) * RW_STEP_CHUNKS
    masks = (strict, incl, eye, blk, off1, off2, diag)
    rows = lambda q: slice(q * n, (q + 1) * n)
    ops = [[ref[rows(q), ln] for qs in order for ref, q in zip((f, b), qs) for ln in lanes]
           for f, b in zip(fwd[:5], bwd[:5])]
    wtot = [ref[8 * q:8 * q + 1, ln] for qs in order for ref, q in zip((fwd[5], bwd[5]), qs) for ln in lanes]
    local = _rw_chunk_local(*ops, masks)
    states = [s_ref[d, g] for d in range(2) for g in range(pairs)]
    per = 2 * pairs
    for i, qs in enumerate(order):
        part = slice(i * per, (i + 1) * per)
        ys, states = _rw_chunk_state([x[part] for x in local], ops[1][part], ops[2][part], ops[4][part], wtot[part],
                                     states, diag)
        for d, (y_ref, q) in enumerate(zip((yf_ref, yb_ref), qs)):
            for g in range(pairs):
                y_ref[rows(q), lanes[g]] = ys[d * pairs + g].astype(y_ref.dtype)
    for d in range(2):
        for g in range(pairs):
            s_ref[d, g] = states[d * pairs + g]

    @pl.when(c == pl.num_programs(1) - 1)
    def _():
        sout_ref[...] = s_ref[...]


def rw_scan(fwd, bwd, s0):
    bsz, seq_len, width = fwd[0].shape
    rows = RW_STEP_CHUNKS * RW_CHUNK
    nc = seq_len // rows
    pairs = width // LANES
    f_spec = pl.BlockSpec((None, rows, width), lambda b_, c: (b_, c, 0))
    b_spec = pl.BlockSpec((None, rows, width), lambda b_, c: (b_, nc - 1 - c, 0))
    fw_spec = pl.BlockSpec((None, rows // 8, width), lambda b_, c: (b_, c, 0))
    bw_spec = pl.BlockSpec((None, rows // 8, width), lambda b_, c: (b_, nc - 1 - c, 0))
    s_spec = pl.BlockSpec((None, 2, pairs, RW_CAT, RW_CAT), lambda b_, c: (b_, 0, 0, 0, 0))
    y_shape = jax.ShapeDtypeStruct((bsz, seq_len, width), BF16)
    return pl.pallas_call(
        _rw_scan_kernel,
        grid=(bsz, nc),
        in_specs=[f_spec] * 5 + [fw_spec] + [b_spec] * 5 + [bw_spec] + [s_spec],
        out_specs=[f_spec, b_spec, s_spec],
        out_shape=[y_shape, y_shape, jax.ShapeDtypeStruct(s0.shape, F32)],
        scratch_shapes=[pltpu.VMEM((2, pairs, RW_CAT, RW_CAT), F32)],
        compiler_params=_cparams("parallel", "arbitrary"),
        name="rw_scan",
    )(*fwd, *bwd, s0)


def _rw_finish_kernel(yf_ref, yb_ref, bon_ref, g_ref, lnw_ref, lnb_ref, e_ref, et_ref, o_ref):
    y = yf_ref[...].astype(F32) + yb_ref[...].astype(F32)
    e = e_ref[...]
    et = et_ref[...]
    mu = _head_sum(y, e, et) * (1.0 / RW_HEAD_DIM)
    yc = y - mu
    var = _head_sum(yc * yc, e, et) * (1.0 / RW_HEAD_DIM)
    yn = yc * lax.rsqrt(var + LNX_EPS) * lnw_ref[...] + lnb_ref[...]
    o_ref[...] = ((yn + bon_ref[...]) * g_ref[...]).astype(o_ref.dtype)


def rw_finish(yf, yb, bon, g, lnw, lnb, e, et, tl):
    m, width = yf.shape
    row = pl.BlockSpec((tl, width), lambda i: (i, 0))
    fixed = lambda a: pl.BlockSpec(a.shape, lambda i: (0, 0))
    return pl.pallas_call(
        _rw_finish_kernel,
        grid=(m // tl,),
        in_specs=[row, row, row, row, fixed(lnw), fixed(lnb), fixed(e), fixed(et)],
        out_specs=row,
        out_shape=jax.ShapeDtypeStruct((m, width), BF16),
        compiler_params=_cparams("parallel"),
        name="rw_finish",
    )(yf, yb, bon, g, lnw, lnb, e, et)


def _rope(x, cos, sin):
    quarter = LANES // 4
    width = x.shape[1]
    lane = lax.broadcasted_iota(jnp.int32, x.shape, 1)
    first = (lane // quarter) % 2 == 0
    partner = jnp.where(first, pltpu.roll(x, width - quarter, axis=1), pltpu.roll(x, quarter, axis=1))
    return x * cos + partner * sin


def _ret_scan_kernel(kf_ref, vf_ref, qf_ref, cf_ref, sf_ref, kb_ref, vb_ref, qb_ref, cb_ref, sb_ref,
                     dec_ref, fs_ref, te_ref, gc_ref, s0_ref, yf_ref, yb_ref, sout_ref, s_ref, *, rope, scale):
    c = pl.program_id(1)

    @pl.when(c == 0)
    def _():
        s_ref[...] = s0_ref[...]

    n_heads = s_ref.shape[1]
    heads = [slice(h * LANES, (h + 1) * LANES) for h in range(n_heads)]
    qs, ks, vs, decay, ss = [], [], [], [], []
    for d, (k_ref, v_ref, q_ref, cos_ref, sin_ref) in enumerate(
            ((kf_ref, vf_ref, qf_ref, cf_ref, sf_ref), (kb_ref, vb_ref, qb_ref, cb_ref, sb_ref))):
        q = q_ref[...].astype(F32)
        k = k_ref[...].astype(F32)
        if rope:
            cos = jnp.concatenate([cos_ref[...]] * n_heads, axis=1)
            sin = jnp.concatenate([sin_ref[...]] * n_heads, axis=1)
            q = _rope(q, cos, sin)
            k = _rope(k, cos, sin)
        k = k * scale
        v = v_ref[...].astype(BF16)
        q_cross = (q * fs_ref[d]).astype(BF16)
        k_end = (k * te_ref[d]).astype(BF16)
        q = q.astype(BF16)
        k = k.astype(BF16)
        for h, hs in enumerate(heads):
            qs.append((q[:, hs], q_cross[:, hs]))
            ks.append((k[:, hs], k_end[:, hs]))
            vs.append(v[:, hs])
            decay.append(dec_ref[d, h])
            ss.append(s_ref[d, h])
    scores = _each(lambda q, k, dm: (_dot_nt(q[0], k[0]) * dm).astype(BF16), qs, ks, decay)
    cross = _each(lambda q, s: _dot(q[1], s.astype(BF16)), qs, ss)
    intra = _each(_dot, scores, vs)
    upd = _each(lambda k, v: _dot_tn(k[1], v), ks, vs)
    for d, y_ref in enumerate((yf_ref, yb_ref)):
        for h, hs in enumerate(heads):
            i = d * n_heads + h
            y_ref[:, hs] = (intra[i] + cross[i]).astype(y_ref.dtype)
            s_ref[d, h] = gc_ref[h] * ss[i] + upd[i]

    @pl.when(c == pl.num_programs(1) - 1)
    def _():
        sout_ref[...] = s_ref[...]


def ret_scan(p, cos, sin, tables, s0, rope):
    bsz, seq_len, cols = p.shape
    width = cols // 4
    nc = seq_len // RET_CHUNK
    dec, fs, te, gc = tables

    def specs(chunk):
        cols_ = [pl.BlockSpec((None, RET_CHUNK, width), lambda b_, c, o=o: (b_, chunk(c), o)) for o in range(3)]
        return cols_ + [pl.BlockSpec((RET_CHUNK, LANES), lambda b_, c: (chunk(c), 0))] * 2

    fwd = lambda c: c
    bwd = lambda c: nc - 1 - c
    fixed = lambda a: pl.BlockSpec(a.shape, lambda b_, c: (0,) * a.ndim)
    s_spec = pl.BlockSpec((None,) + s0.shape[1:], lambda b_, c: (b_, 0, 0, 0, 0))
    y_shape = jax.ShapeDtypeStruct((bsz, seq_len, width), BF16)
    return pl.pallas_call(
        functools.partial(_ret_scan_kernel, rope=rope, scale=float(LANES) ** -0.5),
        grid=(bsz, nc),
        in_specs=specs(fwd) + specs(bwd) + [fixed(dec), fixed(fs), fixed(te), fixed(gc), s_spec],
        out_specs=[pl.BlockSpec((None, RET_CHUNK, width), lambda b_, c: (b_, c, 0)),
                   pl.BlockSpec((None, RET_CHUNK, width), lambda b_, c: (b_, nc - 1 - c, 0)),
                   s_spec],
        out_shape=[y_shape, y_shape, jax.ShapeDtypeStruct(s0.shape, F32)],
        scratch_shapes=[pltpu.VMEM(s0.shape[1:], F32)],
        compiler_params=_cparams("parallel", "arbitrary"),
        name="ret_scan",
    )(p, p, p, cos, sin, p, p, p, cos, sin, dec, fs, te, gc, s0)


def _ret_tables(n_heads):
    n = RET_CHUNK
    lg = np.log(1.0 - 2.0 ** (-5.0 - np.arange(n_heads, dtype=np.float64)))
    pos = np.arange(n, dtype=np.float64)
    diff = pos[:, None] - pos[None, :]
    masks = (diff >= 0, diff < 0)
    dists = (diff, -diff)
    dec = np.stack([np.where(m, np.exp(np.where(m, dd, 0.0)[None] * lg[:, None, None]), 0.0)
                    for m, dd in zip(masks, dists)])
    done = np.stack([pos, n - 1.0 - pos])
    lanes = np.repeat(lg, LANES)[None, None, :]
    fs = np.exp((done[:, :, None] + 1.0) * lanes)
    te = np.exp((n - 1.0 - done[:, :, None]) * lanes)
    gc = np.broadcast_to(np.exp(n * lg)[:, None, None], (n_heads, 1, LANES))
    return tuple(jnp.asarray(a, F32) for a in (dec, fs, te, gc))


def _ret_finish_kernel(yf_ref, yb_ref, g_ref, o_ref):
    g = g_ref[...].astype(F32)
    gate = g * jax.nn.sigmoid(g)
    for h in range(o_ref.shape[1] // LANES):
        hs = slice(h * LANES, (h + 1) * LANES)
        y = yf_ref[:, hs].astype(F32) + yb_ref[:, hs].astype(F32)
        y = y * lax.rsqrt(jnp.mean(y * y, axis=-1, keepdims=True) + EPS)
        o_ref[:, hs] = (y * gate[:, hs]).astype(o_ref.dtype)


def ret_finish(yf, yb, p2, tl):
    m, width = yf.shape
    row = pl.BlockSpec((tl, width), lambda i: (i, 0))
    return pl.pallas_call(
        _ret_finish_kernel,
        grid=(m // tl,),
        in_specs=[row, row, pl.BlockSpec((tl, width), lambda i: (i, 3))],
        out_specs=row,
        out_shape=jax.ShapeDtypeStruct((m, width), BF16),
        compiler_params=_cparams("parallel"),
        name="ret_finish",
    )(yf, yb, p2)


def _rope_tables(seq_len):
    n = LANES // 4
    t = np.arange(seq_len)
    inv = ROPE_BASE ** (-np.arange(n, dtype=np.float64) / n)
    ang_row = (t // GRID_W)[:, None] * inv
    ang_col = (t % GRID_W)[:, None] * inv
    cr, sr, cc, sc = np.cos(ang_row), np.sin(ang_row), np.cos(ang_col), np.sin(ang_col)
    cos = np.concatenate([cr, cr, cc, cc], axis=-1)
    sin = np.concatenate([-sr, sr, -sc, sc], axis=-1)
    return jnp.asarray(cos, F32), jnp.asarray(sin, F32)


def _pad_rows(w, rows, at):
    return jnp.zeros((rows, w.shape[1]), w.dtype).at[at:at + w.shape[0]].set(w)


def kernel(x, c, ctx, c_ctx, w_ada, b_ada, norm_pre_mix, norm_post_mix, norm_pre_ffn, norm_post_ffn, w_in, rw_conv, rw_w0, rw_w_up, rw_a0, rw_a_up, rw_g_up, rw_k_k, rw_k_a, rw_r_k, rw_lnx_w, rw_lnx_b, w_out, ffn_w_gate, ffn_w_up, ffn_conv, ffn_conv_b, ffn_w_down):
    bsz, seq_len, d = x.shape
    ctx_len = ctx.shape[1]
    n_layers = w_ada.shape[0]
    assert n_layers == 1, "context-stream outputs are only needed between layers"
    rw_w = rw_k_k.shape[1]
    ret_w = w_out.shape[1] - rw_w
    rw_cols = 3 * rw_w + 2 * DECAY_LORA + 2 * AAA_LORA + GATE_LORA
    rw_pad = -rw_cols % (2 * LANES)
    d_ff = ffn_w_gate.shape[2]
    m = bsz * seq_len
    flat = lambda a: a.reshape(-1, a.shape[-1])
    l = 0

    cond = jnp.zeros((8, d), F32).at[:bsz].set(c).at[bsz].set(c_ctx)
    mod = ada_modulation(cond, w_ada[l], b_ada[l][None])
    sh1, sc1, g1, sh2, sc2, g2 = [mm[:bsz, None, :] for mm in jnp.split(mod, 6, axis=-1)]
    sh_c = jnp.broadcast_to(mod[bsz, :d], (bsz, 1, d))
    sc_c = jnp.broadcast_to(mod[bsz, d:2 * d], (bsz, 1, d))

    w_in_t = w_in[l].T.astype(BF16)
    g_pre = norm_pre_mix[l][None]

    def in_proj(a, sc, sh, rows, tm):
        p_a, xm = norm_mod_matmul(flat(a), g_pre, sc, sh, w_in_t, rw_cols + rw_pad, rows, tm, TN_IN_RW)
        p_b = matmul_nt(xm, w_in_t, rw_cols, 4 * ret_w, tm, TN_IN_RET, BF16)
        return p_a.reshape(bsz, rows, -1), p_b.reshape(bsz, rows, -1)

    p_rw, p_ret = in_proj(x, sc1, sh1, seq_len, min(TM_IN, seq_len))
    pc_rw, pc_ret = in_proj(ctx, sc_c, sh_c, ctx_len, ctx_len)

    cw = jnp.pad(rw_conv[l], ((0, 0), (0, rw_pad)))
    lora_rows = 2 * DECAY_LORA + 2 * AAA_LORA
    split = lambda w: jnp.stack(_bf16_terms(w, 2))
    wup = split(jnp.stack([_pad_rows(rw_w_up[l][dd], lora_rows, dd * DECAY_LORA) for dd in range(2)]))
    aup = split(jnp.stack([_pad_rows(rw_a_up[l][dd], lora_rows, 2 * DECAY_LORA + dd * AAA_LORA) for dd in range(2)]))
    gup = split(_pad_rows(rw_g_up[l], 256, 0))
    vec = jnp.concatenate([rw_w0[l], rw_a0[l], rw_k_k[l][None], rw_k_a[l][None], rw_r_k[l].reshape(1, rw_w)], axis=0)
    vec = jnp.pad(vec, ((0, 1), (0, 0)))
    head_of_lane = jnp.arange(rw_w) // RW_HEAD_DIM
    e = (head_of_lane[:, None] == jnp.arange(LANES)[None, :]).astype(BF16)
    et = e.T
    prep = functools.partial(rw_prep, cw=cw, wup=wup, aup=aup, gup=gup, vec=vec, e=e, et=et, width=rw_w)
    scan_args = lambda o: (o[0:4] + [o[10], o[4]], o[5:9] + [o[10], o[9]])
    s0 = jnp.zeros((bsz, 2, rw_w // LANES, RW_CAT, RW_CAT), F32)
    outs = list(prep(pc_rw, seq_len=ctx_len, tl=min(TL_PREP, ctx_len)))
    _, _, s_rw = rw_scan(*scan_args(outs), s0)
    outs = list(prep(p_rw, seq_len=seq_len, tl=TL_PREP))
    yf, yb, _ = rw_scan(*scan_args(outs), s_rw)
    o_rw = rw_finish(flat(yf), flat(yb), flat(outs[11]), flat(outs[12]), rw_lnx_w[l][None], rw_lnx_b[l][None],
                     e, et, TL_FINISH)

    n_ret_heads = ret_w // LANES
    tables = _ret_tables(n_ret_heads)
    cos, sin = _rope_tables(seq_len)
    s0 = jnp.zeros((bsz, 2, n_ret_heads, LANES, LANES), F32)
    _, _, s_ret = ret_scan(pc_ret, cos[:ctx_len], sin[:ctx_len], tables, s0, rope=False)
    yf, yb, _ = ret_scan(p_ret, cos, sin, tables, s_ret, rope=True)
    o_ret = ret_finish(flat(yf), flat(yb), flat(p_ret), TL_FINISH)

    w_o = w_out[l].astype(BF16)
    x1, h = out_proj_residual(o_rw, o_ret, w_o[:rw_w], w_o[rw_w:], flat(x), g1, norm_post_mix[l][None],
                              norm_pre_ffn[l][None], sc2, sh2, seq_len, min(TM_OUT, seq_len))
    t = ffn_up(h, ffn_w_gate[l], ffn_w_up[l], ffn_conv[l].reshape(9, d_ff),
               ffn_conv_b[l][None], seq_len, min(TM_FFN, seq_len), TN_FFN)
    out = down_proj_residual(t, ffn_w_down[l].astype(BF16), x1, g2, norm_post_ffn[l][None], seq_len,
                             min(TM_DOWN, seq_len), TK_DOWN)
    return out.reshape(bsz, seq_len, d)
```

```python
import functools

import jax
import jax.numpy as jnp
import numpy as np
from jax import lax
from jax.experimental import pallas as pl
from jax.experimental.pallas import tpu as pltpu

F32 = jnp.float32
BF16 = jnp.bfloat16

LANES = 128
EPS = 1e-6
GRID_W = 64
RW_HEAD_DIM = 64
DECAY_LORA = 64
AAA_LORA = 64
GATE_LORA = 160
LNX_EPS = 64e-5
RET_CHUNK = 128
ROPE_BASE = 10000.0
LOG2_E = 1.4426950408889634
RW_CHUNK = 64
RW_SUB = 16
RW_PROBLEMS = 2
RW_CAT = RW_PROBLEMS * RW_CHUNK
RW_STEP_CHUNKS = 4
VMEM_LIMIT = 56 * 1024 * 1024

TM_IN, TN_IN_RW, TM_OUT, TM_FFN, TN_FFN, TM_DOWN, TK_DOWN = 1024, 512, 512, 1024, 512, 1024, 1408
TL_PREP, TL_FINISH = 256, 512


def _cparams(*sem):
    return pltpu.CompilerParams(dimension_semantics=sem, vmem_limit_bytes=VMEM_LIMIT)


def _dot(a, b, prec=None):
    return jnp.dot(a, b, precision=prec, preferred_element_type=F32)


def _dot_nt(a, b, prec=None):
    return lax.dot_general(a, b, (((1,), (1,)), ((), ())), precision=prec, preferred_element_type=F32)


def _dot_tn(a, b, prec=None):
    return lax.dot_general(a, b, (((0,), (0,)), ((), ())), precision=prec, preferred_element_type=F32)


def _bf16_terms(a, n):
    terms = []
    for _ in range(n):
        t = a.astype(BF16)
        terms.append(t)
        a = a - t.astype(F32)
    return terms


def _dot_sel(sel, x, left=True):
    hi, lo = _bf16_terms(x, 2)
    return (_dot(sel, hi) + _dot(sel, lo)) if left else (_dot(hi, sel) + _dot(lo, sel))


def _dot_x3(a, b_hi, b_lo):
    ah, al = _bf16_terms(a, 2)
    return _dot(ah, b_hi) + (_dot(ah, b_lo) + _dot(al, b_hi))


def _rms(x, g):
    return x * lax.rsqrt(jnp.mean(x * x, axis=-1, keepdims=True) + EPS) * g


def _each(f, *lists):
    return [f(*xs) for xs in zip(*lists)]


def _ada_kernel(s_ref, w_ref, b_ref, o_ref):
    s = s_ref[...]
    s = s * jax.nn.sigmoid(s)
    w_hi, w_lo = _bf16_terms(w_ref[...], 2)
    o_ref[...] = _dot_x3(s, w_hi, w_lo) + b_ref[...]


def ada_modulation(s, w, b, tn=1024):
    m, d = s.shape
    n = w.shape[1]
    return pl.pallas_call(
        _ada_kernel,
        grid=(n // tn,),
        in_specs=[pl.BlockSpec((m, d), lambda j: (0, 0)),
                  pl.BlockSpec((d, tn), lambda j: (0, j)),
                  pl.BlockSpec((1, tn), lambda j: (0, j))],
        out_specs=pl.BlockSpec((m, tn), lambda j: (0, j)),
        out_shape=jax.ShapeDtypeStruct((m, n), F32),
        compiler_params=_cparams("arbitrary"),
        name="ada_modulation",
    )(s, w, b)


def _norm_mm_kernel(x_ref, g_ref, sc_ref, sh_ref, w_ref, o_ref, xm_ref):
    j = pl.program_id(1)
    w = w_ref[...]

    @pl.when(j == 0)
    def _():
        half = x_ref.shape[0] // 2
        for rows in (slice(0, half), slice(half, 2 * half)):
            y = _rms(x_ref[rows], g_ref[...])
            xm = (y * (1.0 + sc_ref[0]) + sh_ref[0]).astype(BF16)
            xm_ref[rows] = xm
            o_ref[rows] = _dot_nt(xm, w)

    @pl.when(j > 0)
    def _():
        o_ref[...] = _dot_nt(xm_ref[...], w)


def norm_mod_matmul(x, g, sc, sh, wt, n, rows_per_batch, tm, tn):
    m, d = x.shape
    bpb = rows_per_batch // tm
    return pl.pallas_call(
        _norm_mm_kernel,
        grid=(m // tm, n // tn),
        in_specs=[pl.BlockSpec((tm, d), lambda i, j: (i, 0)),
                  pl.BlockSpec((1, d), lambda i, j: (0, 0)),
                  pl.BlockSpec((1, 1, d), lambda i, j: (i // bpb, 0, 0)),
                  pl.BlockSpec((1, 1, d), lambda i, j: (i // bpb, 0, 0)),
                  pl.BlockSpec((tn, d), lambda i, j: (j, 0))],
        out_specs=[pl.BlockSpec((tm, tn), lambda i, j: (i, j)), pl.BlockSpec((tm, d), lambda i, j: (i, 0))],
        out_shape=[jax.ShapeDtypeStruct((m, n), F32), jax.ShapeDtypeStruct((m, d), BF16)],
        compiler_params=_cparams("parallel", "arbitrary"),
        name="norm_mod_matmul",
    )(x, g, sc, sh, wt)


def _ret_proj_kernel(a_ref, w_ref, cos_ref, sin_ref, o_ref, *, rope, k_scale):
    j = pl.program_id(1)
    acc = _dot_nt(a_ref[...], w_ref[...])

    def rotary(x):
        if not rope:
            return x
        n_heads = x.shape[1] // LANES
        cos = jnp.concatenate([cos_ref[...]] * n_heads, axis=1)
        sin = jnp.concatenate([sin_ref[...]] * n_heads, axis=1)
        return _rope(x, cos, sin)

    @pl.when(j == 0)
    def _():
        o_ref[...] = (rotary(acc) * k_scale).astype(o_ref.dtype)

    @pl.when(j == 2)
    def _():
        o_ref[...] = rotary(acc).astype(o_ref.dtype)

    @pl.when((j == 1) | (j == 3))
    def _():
        o_ref[...] = acc.astype(o_ref.dtype)


def ret_proj(a, wt, row0, width, cos, sin, rows_per_batch, tm, rope):
    m, k = a.shape
    assert row0 % 16 == 0 and row0 + 4 * width <= wt.shape[0]
    bpb = rows_per_batch // tm
    tab = pl.BlockSpec((tm, LANES), lambda i, j: (i % bpb, 0))
    return pl.pallas_call(
        functools.partial(_ret_proj_kernel, rope=rope, k_scale=float(LANES) ** -0.5),
        grid=(m // tm, 4),
        in_specs=[pl.BlockSpec((tm, k), lambda i, j: (i, 0)),
                  pl.BlockSpec((pl.Element(width), pl.Element(k)),
                               lambda i, j: (pl.multiple_of(row0 + j * width, 16), 0)),
                  tab, tab],
        out_specs=pl.BlockSpec((tm, width), lambda i, j: (i, j)),
        out_shape=jax.ShapeDtypeStruct((m, 4 * width), BF16),
        compiler_params=_cparams("parallel", "arbitrary"),
        name="ret_proj",
    )(a, wt, cos, sin)


def _out_proj_kernel(oa_ref, ob_ref, wa_ref, wb_ref, x_ref, g1_ref, gpost_ref, gpre_ref, sc2_ref, sh2_ref,
                     x1_ref, h_ref):
    out = _dot(oa_ref[...], wa_ref[...]) + _dot(ob_ref[...], wb_ref[...])
    x1 = x_ref[...] + g1_ref[0] * _rms(out, gpost_ref[...])
    x1_ref[...] = x1
    h_ref[...] = (_rms(x1, gpre_ref[...]) * (1.0 + sc2_ref[0]) + sh2_ref[0]).astype(BF16)


def out_proj_residual(oa, ob, wa, wb, x, g1, gpost, gpre, sc2, sh2, rows_per_batch, tm):
    m, ka = oa.shape
    kb = ob.shape[1]
    d = x.shape[1]
    bpb = rows_per_batch // tm
    row = lambda i: (i, 0)
    fixed = lambda i: (0, 0)
    per_b = lambda i: (i // bpb, 0, 0)
    return pl.pallas_call(
        _out_proj_kernel,
        grid=(m // tm,),
        in_specs=[pl.BlockSpec((tm, ka), row), pl.BlockSpec((tm, kb), row),
                  pl.BlockSpec((ka, d), fixed), pl.BlockSpec((kb, d), fixed),
                  pl.BlockSpec((tm, d), row),
                  pl.BlockSpec((1, 1, d), per_b),
                  pl.BlockSpec((1, d), fixed), pl.BlockSpec((1, d), fixed),
                  pl.BlockSpec((1, 1, d), per_b), pl.BlockSpec((1, 1, d), per_b)],
        out_specs=[pl.BlockSpec((tm, d), row), pl.BlockSpec((tm, d), row)],
        out_shape=[jax.ShapeDtypeStruct((m, d), F32), jax.ShapeDtypeStruct((m, d), BF16)],
        compiler_params=_cparams("parallel"),
        name="out_proj_residual",
    )(oa, ob, wa, wb, x, g1, gpost, gpre, sc2, sh2)


def _ffn_up_kernel(top_ref, mid_ref, bot_ref, wg_ref, wu_ref, cw_ref, cb_ref, o_ref, hext_ref, *, n_blocks):
    i = pl.program_id(1)
    tm = mid_ref.shape[0]

    @pl.when(pl.program_id(2) == 0)
    def _():
        hext_ref[0:GRID_W] = top_ref[...]
        hext_ref[GRID_W:GRID_W + tm] = mid_ref[...]
        hext_ref[GRID_W + tm:] = bot_ref[...]

    gate = _dot(hext_ref[...], wg_ref[...].astype(BF16))
    up = _dot(mid_ref[...], wu_ref[...].astype(BF16))
    tn = gate.shape[1]
    mid = gate[GRID_W:GRID_W + tm]
    above = jnp.where(i > 0, gate[:GRID_W], 0.0)
    below = jnp.where(i < n_blocks - 1, gate[GRID_W + tm:], 0.0)
    above = jnp.concatenate([above, mid[:tm - GRID_W]], axis=0)
    below = jnp.concatenate([mid[GRID_W:], below], axis=0)
    w = cw_ref[...]
    col = lax.broadcasted_iota(jnp.int32, (tm, tn), 0) % GRID_W

    def column_sum(dx):
        return above * w[dx:dx + 1] + mid * w[3 + dx:4 + dx] + below * w[6 + dx:7 + dx]

    left = jnp.where(col > 0, pltpu.roll(column_sum(0), 1, axis=0), 0.0)
    right = jnp.where(col < GRID_W - 1, pltpu.roll(column_sum(2), tm - 1, axis=0), 0.0)
    gt = column_sum(1) + left + right + cb_ref[...]
    o_ref[...] = (gt * jax.nn.sigmoid(gt) * up).astype(o_ref.dtype)


def ffn_up(h, wg, wu, w9, b, seq_len, tm, tn):
    m, d = h.shape
    f = wg.shape[1]
    nb = seq_len // tm
    bsz = m // seq_len
    hpb = tm // GRID_W
    n_halo = seq_len // GRID_W
    main = lambda b_, i, j: (b_ * nb + i, 0)
    top = lambda b_, i, j: (b_ * n_halo + jnp.maximum(i * hpb - 1, 0), 0)
    bot = lambda b_, i, j: (b_ * n_halo + jnp.minimum((i + 1) * hpb, n_halo - 1), 0)
    col = lambda b_, i, j: (0, j)
    return pl.pallas_call(
        functools.partial(_ffn_up_kernel, n_blocks=nb),
        grid=(bsz, nb, f // tn),
        in_specs=[pl.BlockSpec((GRID_W, d), top), pl.BlockSpec((tm, d), main), pl.BlockSpec((GRID_W, d), bot),
                  pl.BlockSpec((d, tn), col), pl.BlockSpec((d, tn), col),
                  pl.BlockSpec((9, tn), col), pl.BlockSpec((1, tn), col)],
        out_specs=pl.BlockSpec((tm, tn), lambda b_, i, j: (b_ * nb + i, j)),
        out_shape=jax.ShapeDtypeStruct((m, f), BF16),
        scratch_shapes=[pltpu.VMEM((tm + 2 * GRID_W, d), BF16)],
        compiler_params=_cparams("parallel", "parallel", "arbitrary"),
        name="ffn_up",
    )(h, h, h, wg, wu, w9, b)


def _down_proj_kernel(t_ref, w_ref, x1_ref, g2_ref, gpost_ref, o_ref):
    k = pl.program_id(1)
    last = pl.num_programs(1) - 1

    @pl.when(k == 0)
    def _():
        o_ref[...] = _dot(t_ref[...], w_ref[...])

    @pl.when((k > 0) & (k < last))
    def _():
        o_ref[...] += _dot(t_ref[...], w_ref[...])

    @pl.when(k == last)
    def _():
        half = o_ref.shape[0] // 2
        for rows in (slice(0, half), slice(half, 2 * half)):
            acc = o_ref[rows] + _dot(t_ref[rows], w_ref[...])
            o_ref[rows] = x1_ref[rows] + g2_ref[0] * _rms(acc, gpost_ref[...])


def down_proj_residual(t, w, x1, g2, gpost, rows_per_batch, tm, tk):
    m, kk = t.shape
    d = w.shape[1]
    bpb = rows_per_batch // tm
    return pl.pallas_call(
        _down_proj_kernel,
        grid=(m // tm, kk // tk),
        in_specs=[pl.BlockSpec((tm, tk), lambda i, k: (i, k)),
                  pl.BlockSpec((tk, d), lambda i, k: (k, 0)),
                  pl.BlockSpec((tm, d), lambda i, k: (i, 0)),
                  pl.BlockSpec((1, 1, d), lambda i, k: (i // bpb, 0, 0)),
                  pl.BlockSpec((1, d), lambda i, k: (0, 0))],
        out_specs=pl.BlockSpec((tm, d), lambda i, k: (i, 0)),
        out_shape=jax.ShapeDtypeStruct((m, d), F32),
        compiler_params=_cparams("parallel", "arbitrary"),
        name="down_proj_residual",
    )(t, w, x1, g2, gpost)


def _head_sum(x, e, et):
    return _dot_sel(et, _dot_sel(e, x, left=False), left=False)


def _rw_prep_kernel(prev_ref, cur_ref, next_ref, cw_ref, wup_ref, aup_ref, gup_ref, vec_ref, e_ref, et_ref,
                    af_ref, btf_ref, ktf_ref, rf_ref, wtf_ref, ab_ref, btb_ref, ktb_ref, rb_ref, wtb_ref,
                    v_ref, bon_ref, g_ref, *, n_blocks, width):
    i = pl.program_id(1)
    tl = cur_ref.shape[0]
    cur = cur_ref[...]
    rows = lax.broadcasted_iota(jnp.int32, cur.shape, 0)
    prev_row = jnp.where(i > 0, prev_ref[7:8, :], 0.0)
    next_row = jnp.where(i < n_blocks - 1, next_ref[0:1, :], 0.0)
    before = jnp.where(rows == 0, prev_row, pltpu.roll(cur, 1, axis=0))
    after = jnp.where(rows == tl - 1, next_row, pltpu.roll(cur, tl - 1, axis=0))
    cw = cw_ref[...]
    c = before * cw[0:1] + cur * cw[1:2] + after * cw[2:3]

    w_ = width
    k = c[:, :w_]
    v = c[:, w_:2 * w_]
    lo = c[:, 2 * w_:2 * w_ + 256]
    r = c[:, 2 * w_ + 256:3 * w_ + 256]
    gl = c[:, 3 * w_ + 256:3 * w_ + 512]
    vec = vec_ref[...]
    e = e_ref[...]
    et = et_ref[...]
    k_k, k_a, r_k = vec[4:5], vec[5:6], vec[6:7]

    kk = k * k_k
    kk = kk * lax.rsqrt(_head_sum(kk * kk, e, et) + 1e-12)
    neg_kk = -kk
    one_minus_k_a = 1.0 - k_a
    v_ref[...] = v.astype(BF16)
    tlo = jnp.tanh(lo)
    kd_sum = jnp.zeros_like(k)
    ti = lax.broadcasted_iota(jnp.int32, (tl, tl), 0)
    tj = lax.broadcasted_iota(jnp.int32, (tl, tl), 1)
    same_chunk = ti // RW_CHUNK == tj // RW_CHUNK
    wi = lax.broadcasted_iota(jnp.int32, (tl // 8, tl), 0)
    wj = lax.broadcasted_iota(jnp.int32, (tl // 8, tl), 1)
    chunk_rows = (wi * 8) // RW_CHUNK == wj // RW_CHUNK
    dirs = ((af_ref, btf_ref, ktf_ref, rf_ref, wtf_ref, tj <= ti), (ab_ref, btb_ref, ktb_ref, rb_ref, wtb_ref, tj >= ti))
    for d, (a_ref, bt_ref, kt_ref, r_ref, wt_ref, done) in enumerate(dirs):
        z = vec[d:d + 1] + _dot_x3(tlo, wup_ref[0, d], wup_ref[1, d])
        nz = -z
        softplus = jnp.maximum(nz, 0.0) + jnp.log1p(jnp.exp(-jnp.abs(nz)))
        w_log = -softplus - 0.5
        lw = jnp.exp(w_log) * -LOG2_E
        a = jax.nn.sigmoid(vec[2 + d:3 + d] + _dot_x3(lo, aup_ref[0, d], aup_ref[1, d]))
        kd = k * (a * k_a + one_minus_k_a)
        kd_sum = kd_sum + kd
        sel = jnp.concatenate([(same_chunk & done).astype(F32), chunk_rows.astype(F32)], axis=0).astype(BF16)
        sums = _dot_sel(sel, lw)
        cum = sums[:tl]
        w_inv = jnp.exp2(-cum)
        a_ref[...] = (neg_kk * jnp.exp2(cum - lw)).astype(BF16)
        bt_ref[...] = (kk * a * w_inv).astype(BF16)
        kt_ref[...] = (kd * w_inv).astype(BF16)
        r_ref[...] = (r * jnp.exp2(cum)).astype(BF16)
        wt_ref[...] = jnp.exp2(sums[tl:])
    bon_ref[...] = _head_sum(r * kd_sum * r_k, e, et) * v
    g_ref[...] = _dot_x3(jax.nn.sigmoid(gl), gup_ref[0], gup_ref[1])


def rw_prep(p, cw, wup, aup, gup, vec, e, et, seq_len, tl, width):
    bsz, _, rw_cols = p.shape
    nb = seq_len // tl
    hb = tl // 8
    n_halo = seq_len // 8
    fixed = lambda a: pl.BlockSpec(a.shape, lambda b_, i: (0,) * a.ndim)
    row_spec = pl.BlockSpec((None, tl, width), lambda b_, i: (b_, i, 0))
    wt_spec = pl.BlockSpec((None, tl // 8, width), lambda b_, i: (b_, i, 0))
    act = lambda dt: jax.ShapeDtypeStruct((bsz, seq_len, width), dt)
    wt = jax.ShapeDtypeStruct((bsz, seq_len // 8, width), F32)
    per_dir_specs = [row_spec] * 4 + [wt_spec]
    per_dir_shapes = [act(BF16)] * 4 + [wt]
    return pl.pallas_call(
        functools.partial(_rw_prep_kernel, n_blocks=nb, width=width),
        grid=(bsz, nb),
        in_specs=[pl.BlockSpec((None, 8, rw_cols), lambda b_, i: (b_, jnp.maximum(i * hb - 1, 0), 0)),
                  pl.BlockSpec((None, tl, rw_cols), lambda b_, i: (b_, i, 0)),
                  pl.BlockSpec((None, 8, rw_cols), lambda b_, i: (b_, jnp.minimum((i + 1) * hb, n_halo - 1), 0)),
                  fixed(cw), fixed(wup), fixed(aup), fixed(gup), fixed(vec), fixed(e), fixed(et)],
        out_specs=per_dir_specs * 2 + [row_spec] * 3,
        out_shape=per_dir_shapes * 2 + [act(BF16), act(F32), act(F32)],
        compiler_params=_cparams("parallel", "parallel"),
        name="rw_prep",
    )(p, p, p, cw, wup, aup, gup, vec, e, et)


def _block_diag(y, diag):
    yb = y.astype(BF16)
    tiled = jnp.concatenate([yb] * RW_PROBLEMS, axis=0)
    return jnp.where(diag, tiled, jnp.zeros_like(tiled))


def _unit_tri_inverse(a, eye, blk, off1, off2, diag):
    n = RW_CHUNK
    mm = lambda xs, ys: _each(lambda x, y: _dot(x.astype(BF16), _block_diag(y, diag)), xs, ys)
    add = lambda xs, ys: _each(jnp.add, xs, ys)
    stack = lambda xs, ys: _each(lambda x, y: jnp.concatenate([x, y], axis=0), xs, ys)
    d = [ai * blk for ai in a]
    x = [eye + di for di in d]
    d2 = mm(d, d)
    both = mm(stack(x, d2), d2)
    x = _each(lambda xi, p: xi + p[:n], x, both)
    d4 = [p[n:] for p in both]
    both = mm(stack(x, d4), d4)
    x = _each(lambda xi, p: xi + p[:n], x, both)
    d8 = [p[n:] for p in both]
    x = add(x, mm(x, d8))
    x = add(x, mm(mm(x, [ai * off1 for ai in a]), x))
    x = add(x, mm(mm(x, [ai * off2 for ai in a]), x))
    return x


def _rw_chunk_local(a, bt, kt, r, v, masks):
    strict, incl, eye, blk, off1, off2, diag = masks
    n = RW_CHUNK
    bf = lambda xs: [x.astype(BF16) for x in xs]
    bdiag = lambda xs: [_block_diag(x, diag) for x in xs]
    stack = lambda xs, ys: _each(lambda x, y: jnp.concatenate([x, y], axis=0), xs, ys)
    ar = stack(a, r)
    sc = _each(_dot_nt, ar, stack(bdiag(bt), bdiag(kt)))
    a_ab = _each(lambda x, m: jnp.where(m, x[:n, :RW_CAT], 0.0), sc, strict)
    a_ak = _each(lambda x, m: jnp.where(m, x[:n, RW_CAT:], 0.0), sc, strict)
    r_rb = _each(lambda x, m: jnp.where(m, x[n:, :RW_CAT], 0.0), sc, incl)
    r_rk = _each(lambda x, m: jnp.where(m, x[n:, RW_CAT:], 0.0), sc, incl)
    inv = _unit_tri_inverse(a_ab, eye, blk, off1, off2, diag)
    kv = _each(_dot, bf(stack(a_ak, r_rk)), bdiag(v))
    return ar, bf(inv), bf(r_rb), kv


def _rw_chunk_state(local, bt, kt, v, wtot, s, diag):
    ar, inv, r_rb, kv = local
    n = RW_CHUNK
    bdiag = lambda xs: [_block_diag(x, diag) for x in xs]
    sc_s = _each(lambda x, si: _dot_nt(x, si.astype(BF16)), ar, s)
    rhs = _each(lambda x, y: x[:n] + y[:n], sc_s, kv)
    u = _each(_dot, inv, bdiag(rhs))
    y = _each(lambda x, p, q: x[n:] + p + q[n:], sc_s, _each(_dot, r_rb, bdiag(u)), kv)
    to_end = lambda x, w: (x.astype(F32) * w).astype(BF16)
    uv = _each(lambda x, z: jnp.concatenate([x.astype(BF16), z], axis=0), u, v)
    bk = _each(lambda x, z, w: jnp.concatenate([to_end(x, w), to_end(z, w)], axis=0), bt, kt, wtot)
    upd = _each(_dot_tn, uv, bk)
    return y, _each(lambda si, w, x: si * w + jnp.where(diag, x, 0.0), s, wtot, upd)


def _rw_scan_kernel(*refs):
    fwd, bwd = refs[0:6], refs[6:12]
    s0_ref, yf_ref, yb_ref, sout_ref, s_ref = refs[12:]
    c = pl.program_id(1)

    @pl.when(c == 0)
    def _():
        s_ref[...] = s0_ref[...]

    n = RW_CHUNK
    t = lax.broadcasted_iota(jnp.int32, (n, RW_CAT), 0)
    lane = lax.broadcasted_iota(jnp.int32, (n, RW_CAT), 1)
    step = lane % n
    eye = (t == step).astype(F32)
    blk = (t // RW_SUB == step // RW_SUB).astype(F32)
    off1 = ((t // (2 * RW_SUB) == step // (2 * RW_SUB)) & (t // RW_SUB != step // RW_SUB)).astype(F32)
    off2 = (t // (2 * RW_SUB) != step // (2 * RW_SUB)).astype(F32)
    di = lax.broadcasted_iota(jnp.int32, (RW_CAT, RW_CAT), 0) // n
    dj = lax.broadcasted_iota(jnp.int32, (RW_CAT, RW_CAT), 1) // n
    diag = di == dj
    pairs = s_ref.shape[1]
    lanes = [slice(g * LANES, (g + 1) * LANES) for g in range(pairs)]
    order = [(q, RW_STEP_CHUNKS - 1 - q) for q in range(RW_STEP_CHUNKS)]
    strict = ([t > step] * pairs + [t < step] * pairs) * RW_STEP_CHUNKS
    incl = ([t >= step] * pairs + [t <= step] * pairs) * RW_STEP_CHUNKS
    masks = (strict, incl, eye, blk, off1, off2, diag)
    rows = lambda q: slice(q * n, (q + 1) * n)
    ops = [[ref[rows(q), ln] for qs in order for ref, q in zip((f, b), qs) for ln in lanes]
           for f, b in zip(fwd[:5], bwd[:5])]
    wtot = [ref[8 * q:8 * q + 1, ln] for qs in order for ref, q in zip((fwd[5], bwd[5]), qs) for ln in lanes]
    local = _rw_chunk_local(*ops, masks)
    states = [s_ref[d, g] for d in range(2) for g in range(pairs)]
    per = 2 * pairs
    for i, qs in enumerate(order):
        part = slice(i * per, (i + 1) * per)
        ys, states = _rw_chunk_state([x[part] for x in local], ops[1][part], ops[2][part], ops[4][part], wtot[part],
                                     states, diag)
        for d, (y_ref, q) in enumerate(zip((yf_ref, yb_ref), qs)):
            for g in range(pairs):
                y_ref[rows(q), lanes[g]] = ys[d * pairs + g].astype(y_ref.dtype)
    for d in range(2):
        for g in range(pairs):
            s_ref[d, g] = states[d * pairs + g]

    @pl.when(c == pl.num_programs(1) - 1)
    def _():
        sout_ref[...] = s_ref[...]


def rw_scan(fwd, bwd, s0):
    bsz, seq_len, width = fwd[0].shape
    rows = RW_STEP_CHUNKS * RW_CHUNK
    nc = seq_len // rows
    pairs = width // LANES
    f_spec = pl.BlockSpec((None, rows, width), lambda b_, c: (b_, c, 0))
    b_spec = pl.BlockSpec((None, rows, width), lambda b_, c: (b_, nc - 1 - c, 0))
    fw_spec = pl.BlockSpec((None, rows // 8, width), lambda b_, c: (b_, c, 0))
    bw_spec = pl.BlockSpec((None, rows // 8, width), lambda b_, c: (b_, nc - 1 - c, 0))
    s_spec = pl.BlockSpec((None, 2, pairs, RW_CAT, RW_CAT), lambda b_, c: (b_, 0, 0, 0, 0))
    y_shape = jax.ShapeDtypeStruct((bsz, seq_len, width), BF16)
    return pl.pallas_call(
        _rw_scan_kernel,
        grid=(bsz, nc),
        in_specs=[f_spec] * 5 + [fw_spec] + [b_spec] * 5 + [bw_spec] + [s_spec],
        out_specs=[f_spec, b_spec, s_spec],
        out_shape=[y_shape, y_shape, jax.ShapeDtypeStruct(s0.shape, F32)],
        scratch_shapes=[pltpu.VMEM((2, pairs, RW_CAT, RW_CAT), F32)],
        compiler_params=_cparams("parallel", "arbitrary"),
        name="rw_scan",
    )(*fwd, *bwd, s0)


def _rw_finish_kernel(yf_ref, yb_ref, bon_ref, g_ref, lnw_ref, lnb_ref, e_ref, et_ref, o_ref):
    y = yf_ref[...].astype(F32) + yb_ref[...].astype(F32)
    e = e_ref[...]
    et = et_ref[...]
    mu = _head_sum(y, e, et) * (1.0 / RW_HEAD_DIM)
    yc = y - mu
    var = _head_sum(yc * yc, e, et) * (1.0 / RW_HEAD_DIM)
    yn = yc * lax.rsqrt(var + LNX_EPS) * lnw_ref[...] + lnb_ref[...]
    o_ref[...] = ((yn + bon_ref[...]) * g_ref[...]).astype(o_ref.dtype)


def rw_finish(yf, yb, bon, g, lnw, lnb, e, et, tl):
    m, width = yf.shape
    row = pl.BlockSpec((tl, width), lambda i: (i, 0))
    fixed = lambda a: pl.BlockSpec(a.shape, lambda i: (0, 0))
    return pl.pallas_call(
        _rw_finish_kernel,
        grid=(m // tl,),
        in_specs=[row, row, row, row, fixed(lnw), fixed(lnb), fixed(e), fixed(et)],
        out_specs=row,
        out_shape=jax.ShapeDtypeStruct((m, width), BF16),
        compiler_params=_cparams("parallel"),
        name="rw_finish",
    )(yf, yb, bon, g, lnw, lnb, e, et)


def _rope(x, cos, sin):
    quarter = LANES // 4
    width = x.shape[1]
    lane = lax.broadcasted_iota(jnp.int32, x.shape, 1)
    first = (lane // quarter) % 2 == 0
    partner = jnp.where(first, pltpu.roll(x, width - quarter, axis=1), pltpu.roll(x, quarter, axis=1))
    return x * cos + partner * sin


def _ret_scan_kernel(kf_ref, vf_ref, qf_ref, kb_ref, vb_ref, qb_ref,
                     dec_ref, fs_ref, te_ref, gc_ref, s0_ref, yf_ref, yb_ref, sout_ref, s_ref):
    c = pl.program_id(1)

    @pl.when(c == 0)
    def _():
        s_ref[...] = s0_ref[...]

    n_heads = s_ref.shape[1]
    heads = [slice(h * LANES, (h + 1) * LANES) for h in range(n_heads)]
    qs, ks, vs, decay, ss = [], [], [], [], []
    for d, (k_ref, v_ref, q_ref) in enumerate(((kf_ref, vf_ref, qf_ref), (kb_ref, vb_ref, qb_ref))):
        q = q_ref[...]
        k = k_ref[...]
        v = v_ref[...]
        q_cross = (q.astype(F32) * fs_ref[d]).astype(BF16)
        k_end = (k.astype(F32) * te_ref[d]).astype(BF16)
        for h, hs in enumerate(heads):
            qs.append((q[:, hs], q_cross[:, hs]))
            ks.append((k[:, hs], k_end[:, hs]))
            vs.append(v[:, hs])
            decay.append(dec_ref[d, h])
            ss.append(s_ref[d, h])
    scores = _each(lambda q, k, dm: (_dot_nt(q[0], k[0]) * dm).astype(BF16), qs, ks, decay)
    cross = _each(lambda q, s: _dot(q[1], s.astype(BF16)), qs, ss)
    intra = _each(_dot, scores, vs)
    upd = _each(lambda k, v: _dot_tn(k[1], v), ks, vs)
    for d, y_ref in enumerate((yf_ref, yb_ref)):
        for h, hs in enumerate(heads):
            i = d * n_heads + h
            y_ref[:, hs] = (intra[i] + cross[i]).astype(y_ref.dtype)
            s_ref[d, h] = gc_ref[h] * ss[i] + upd[i]

    @pl.when(c == pl.num_programs(1) - 1)
    def _():
        sout_ref[...] = s_ref[...]


def ret_scan(p, tables, s0):
    bsz, seq_len, cols = p.shape
    width = cols // 4
    nc = seq_len // RET_CHUNK
    dec, fs, te, gc = tables

    def specs(chunk):
        return [pl.BlockSpec((None, RET_CHUNK, width), lambda b_, c, o=o: (b_, chunk(c), o)) for o in range(3)]

    fwd = lambda c: c
    bwd = lambda c: nc - 1 - c
    fixed = lambda a: pl.BlockSpec(a.shape, lambda b_, c: (0,) * a.ndim)
    s_spec = pl.BlockSpec((None,) + s0.shape[1:], lambda b_, c: (b_, 0, 0, 0, 0))
    y_shape = jax.ShapeDtypeStruct((bsz, seq_len, width), BF16)
    return pl.pallas_call(
        _ret_scan_kernel,
        grid=(bsz, nc),
        in_specs=specs(fwd) + specs(bwd) + [fixed(dec), fixed(fs), fixed(te), fixed(gc), s_spec],
        out_specs=[pl.BlockSpec((None, RET_CHUNK, width), lambda b_, c: (b_, c, 0)),
                   pl.BlockSpec((None, RET_CHUNK, width), lambda b_, c: (b_, nc - 1 - c, 0)),
                   s_spec],
        out_shape=[y_shape, y_shape, jax.ShapeDtypeStruct(s0.shape, F32)],
        scratch_shapes=[pltpu.VMEM(s0.shape[1:], F32)],
        compiler_params=_cparams("parallel", "arbitrary"),
        name="ret_scan",
    )(p, p, p, p, p, p, dec, fs, te, gc, s0)


def _ret_tables(n_heads):
    n = RET_CHUNK
    lg = np.log(1.0 - 2.0 ** (-5.0 - np.arange(n_heads, dtype=np.float64)))
    pos = np.arange(n, dtype=np.float64)
    diff = pos[:, None] - pos[None, :]
    masks = (diff >= 0, diff < 0)
    dists = (diff, -diff)
    dec = np.stack([np.where(m, np.exp(np.where(m, dd, 0.0)[None] * lg[:, None, None]), 0.0)
                    for m, dd in zip(masks, dists)])
    done = np.stack([pos, n - 1.0 - pos])
    lanes = np.repeat(lg, LANES)[None, None, :]
    fs = np.exp((done[:, :, None] + 1.0) * lanes)
    te = np.exp((n - 1.0 - done[:, :, None]) * lanes)
    gc = np.broadcast_to(np.exp(n * lg)[:, None, None], (n_heads, 1, LANES))
    return tuple(jnp.asarray(a, F32) for a in (dec, fs, te, gc))


def _ret_finish_kernel(yf_ref, yb_ref, g_ref, o_ref):
    g = g_ref[...].astype(F32)
    gate = g * jax.nn.sigmoid(g)
    for h in range(o_ref.shape[1] // LANES):
        hs = slice(h * LANES, (h + 1) * LANES)
        y = yf_ref[:, hs].astype(F32) + yb_ref[:, hs].astype(F32)
        y = y * lax.rsqrt(jnp.mean(y * y, axis=-1, keepdims=True) + EPS)
        o_ref[:, hs] = (y * gate[:, hs]).astype(o_ref.dtype)


def ret_finish(yf, yb, p2, tl):
    m, width = yf.shape
    row = pl.BlockSpec((tl, width), lambda i: (i, 0))
    return pl.pallas_call(
        _ret_finish_kernel,
        grid=(m // tl,),
        in_specs=[row, row, pl.BlockSpec((tl, width), lambda i: (i, 3))],
        out_specs=row,
        out_shape=jax.ShapeDtypeStruct((m, width), BF16),
        compiler_params=_cparams("parallel"),
        name="ret_finish",
    )(yf, yb, p2)


def _rope_tables(seq_len):
    n = LANES // 4
    t = np.arange(seq_len)
    inv = ROPE_BASE ** (-np.arange(n, dtype=np.float64) / n)
    ang_row = (t // GRID_W)[:, None] * inv
    ang_col = (t % GRID_W)[:, None] * inv
    cr, sr, cc, sc = np.cos(ang_row), np.sin(ang_row), np.cos(ang_col), np.sin(ang_col)
    cos = np.concatenate([cr, cr, cc, cc], axis=-1)
    sin = np.concatenate([-sr, sr, -sc, sc], axis=-1)
    return jnp.asarray(cos, F32), jnp.asarray(sin, F32)


def _pad_rows(w, rows, at):
    return jnp.zeros((rows, w.shape[1]), w.dtype).at[at:at + w.shape[0]].set(w)


def kernel(x, c, ctx, c_ctx, w_ada, b_ada, norm_pre_mix, norm_post_mix, norm_pre_ffn, norm_post_ffn, w_in, rw_conv, rw_w0, rw_w_up, rw_a0, rw_a_up, rw_g_up, rw_k_k, rw_k_a, rw_r_k, rw_lnx_w, rw_lnx_b, w_out, ffn_w_gate, ffn_w_up, ffn_conv, ffn_conv_b, ffn_w_down):
    bsz, seq_len, d = x.shape
    ctx_len = ctx.shape[1]
    n_layers = w_ada.shape[0]
    assert n_layers == 1, "context-stream outputs are only needed between layers"
    rw_w = rw_k_k.shape[1]
    ret_w = w_out.shape[1] - rw_w
    rw_cols = 3 * rw_w + 2 * DECAY_LORA + 2 * AAA_LORA + GATE_LORA
    rw_pad = -rw_cols % (2 * LANES)
    d_ff = ffn_w_gate.shape[2]
    m = bsz * seq_len
    flat = lambda a: a.reshape(-1, a.shape[-1])
    l = 0

    cond = jnp.zeros((8, d), F32).at[:bsz].set(c).at[bsz].set(c_ctx)
    mod = ada_modulation(cond, w_ada[l], b_ada[l][None])
    sh1, sc1, g1, sh2, sc2, g2 = [mm[:bsz, None, :] for mm in jnp.split(mod, 6, axis=-1)]
    sh_c = jnp.broadcast_to(mod[bsz, :d], (bsz, 1, d))
    sc_c = jnp.broadcast_to(mod[bsz, d:2 * d], (bsz, 1, d))

    w_in_t = w_in[l].T.astype(BF16)
    g_pre = norm_pre_mix[l][None]

    def in_proj(a, sc, sh, rows, tm, rope):
        p_a, xm = norm_mod_matmul(flat(a), g_pre, sc, sh, w_in_t, rw_cols + rw_pad, rows, tm, TN_IN_RW)
        p_b = ret_proj(xm, w_in_t, rw_cols, ret_w, cos[:rows], sin[:rows], rows, tm, rope)
        return p_a.reshape(bsz, rows, -1), p_b.reshape(bsz, rows, -1)

    cos, sin = _rope_tables(seq_len)
    p_rw, p_ret = in_proj(x, sc1, sh1, seq_len, min(TM_IN, seq_len), True)
    pc_rw, pc_ret = in_proj(ctx, sc_c, sh_c, ctx_len, ctx_len, False)

    cw = jnp.pad(rw_conv[l], ((0, 0), (0, rw_pad)))
    lora_rows = 2 * DECAY_LORA + 2 * AAA_LORA
    split = lambda w: jnp.stack(_bf16_terms(w, 2))
    wup = split(jnp.stack([_pad_rows(rw_w_up[l][dd], lora_rows, dd * DECAY_LORA) for dd in range(2)]))
    aup = split(jnp.stack([_pad_rows(rw_a_up[l][dd], lora_rows, 2 * DECAY_LORA + dd * AAA_LORA) for dd in range(2)]))
    gup = split(_pad_rows(rw_g_up[l], 256, 0))
    vec = jnp.concatenate([rw_w0[l], rw_a0[l], rw_k_k[l][None], rw_k_a[l][None], rw_r_k[l].reshape(1, rw_w)], axis=0)
    vec = jnp.pad(vec, ((0, 1), (0, 0)))
    head_of_lane = jnp.arange(rw_w) // RW_HEAD_DIM
    e = (head_of_lane[:, None] == jnp.arange(LANES)[None, :]).astype(BF16)
    et = e.T
    prep = functools.partial(rw_prep, cw=cw, wup=wup, aup=aup, gup=gup, vec=vec, e=e, et=et, width=rw_w)
    scan_args = lambda o: (o[0:4] + [o[10], o[4]], o[5:9] + [o[10], o[9]])
    s0 = jnp.zeros((bsz, 2, rw_w // LANES, RW_CAT, RW_CAT), F32)
    outs = list(prep(pc_rw, seq_len=ctx_len, tl=min(TL_PREP, ctx_len)))
    _, _, s_rw = rw_scan(*scan_args(outs), s0)
    outs = list(prep(p_rw, seq_len=seq_len, tl=TL_PREP))
    yf, yb, _ = rw_scan(*scan_args(outs), s_rw)
    o_rw = rw_finish(flat(yf), flat(yb), flat(outs[11]), flat(outs[12]), rw_lnx_w[l][None], rw_lnx_b[l][None],
                     e, et, TL_FINISH)

    n_ret_heads = ret_w // LANES
    tables = _ret_tables(n_ret_heads)
    s0 = jnp.zeros((bsz, 2, n_ret_heads, LANES, LANES), F32)
    _, _, s_ret = ret_scan(pc_ret, tables, s0)
    yf, yb, _ = ret_scan(p_ret, tables, s_ret)
    o_ret = ret_finish(flat(yf), flat(yb), flat(p_ret), TL_FINISH)

    w_o = w_out[l].astype(BF16)
    x1, h = out_proj_residual(o_rw, o_ret, w_o[:rw_w], w_o[rw_w:], flat(x), g1, norm_post_mix[l][None],
                              norm_pre_ffn[l][None], sc2, sh2, seq_len, min(TM_OUT, seq_len))
    t = ffn_up(h, ffn_w_gate[l], ffn_w_up[l], ffn_conv[l].reshape(9, d_ff),
               ffn_conv_b[l][None], seq_len, min(TM_FFN, seq_len), TN_FFN)
    out = down_proj_residual(t, ffn_w_down[l].astype(BF16), x1, g2, norm_post_ffn[l][None], seq_len,
                             min(TM_DOWN, seq_len), TK_DOWN)
    return out.reshape(bsz, seq_len, d)
```

```python
import functools

import jax
import jax.numpy as jnp
import numpy as np
from jax import lax
from jax.experimental import pallas as pl
from jax.experimental.pallas import tpu as pltpu

F32 = jnp.float32
BF16 = jnp.bfloat16

LANES = 128
EPS = 1e-6
GRID_W = 64
RW_HEAD_DIM = 64
DECAY_LORA = 64
AAA_LORA = 64
GATE_LORA = 160
LNX_EPS = 64e-5
RET_CHUNK = 128
RET_STEP_CHUNKS = 2
ROPE_BASE = 10000.0
LOG2_E = 1.4426950408889634
RW_CHUNK = 64
RW_SUB = 16
RW_PROBLEMS = 2
RW_CAT = RW_PROBLEMS * RW_CHUNK
RW_STEP_CHUNKS = 4
VMEM_LIMIT = 56 * 1024 * 1024

TM_IN, TN_IN_RW, TN_IN_RET, TM_OUT, TM_FFN, TN_FFN, TM_DOWN, TK_DOWN = 1024, 512, 1024, 512, 1024, 512, 1024, 1408
TL_PREP, TL_FINISH = 256, 512


def _cparams(*sem):
    return pltpu.CompilerParams(dimension_semantics=sem, vmem_limit_bytes=VMEM_LIMIT)


def _dot(a, b, prec=None):
    return jnp.dot(a, b, precision=prec, preferred_element_type=F32)


def _dot_nt(a, b, prec=None):
    return lax.dot_general(a, b, (((1,), (1,)), ((), ())), precision=prec, preferred_element_type=F32)


def _dot_tn(a, b, prec=None):
    return lax.dot_general(a, b, (((0,), (0,)), ((), ())), precision=prec, preferred_element_type=F32)


def _bf16_terms(a, n):
    terms = []
    for _ in range(n):
        t = a.astype(BF16)
        terms.append(t)
        a = a - t.astype(F32)
    return terms


def _dot_sel(sel, x, left=True):
    hi, lo = _bf16_terms(x, 2)
    return (_dot(sel, hi) + _dot(sel, lo)) if left else (_dot(hi, sel) + _dot(lo, sel))


def _dot_x3(a, b_hi, b_lo):
    ah, al = _bf16_terms(a, 2)
    return _dot(ah, b_hi) + (_dot(ah, b_lo) + _dot(al, b_hi))


def _rms(x, g):
    return x * lax.rsqrt(jnp.mean(x * x, axis=-1, keepdims=True) + EPS) * g


def _each(f, *lists):
    return [f(*xs) for xs in zip(*lists)]


def _ada_kernel(s_ref, w_ref, b_ref, o_ref):
    s = s_ref[...]
    s = s * jax.nn.sigmoid(s)
    w_hi, w_lo = _bf16_terms(w_ref[...], 2)
    o_ref[...] = _dot_x3(s, w_hi, w_lo) + b_ref[...]


def ada_modulation(s, w, b, tn=1024):
    m, d = s.shape
    n = w.shape[1]
    return pl.pallas_call(
        _ada_kernel,
        grid=(n // tn,),
        in_specs=[pl.BlockSpec((m, d), lambda j: (0, 0)),
                  pl.BlockSpec((d, tn), lambda j: (0, j)),
                  pl.BlockSpec((1, tn), lambda j: (0, j))],
        out_specs=pl.BlockSpec((m, tn), lambda j: (0, j)),
        out_shape=jax.ShapeDtypeStruct((m, n), F32),
        compiler_params=_cparams("arbitrary"),
        name="ada_modulation",
    )(s, w, b)


def _norm_mm_kernel(x_ref, g_ref, sc_ref, sh_ref, w_ref, o_ref, xm_ref):
    j = pl.program_id(1)
    w = w_ref[...]

    @pl.when(j == 0)
    def _():
        half = x_ref.shape[0] // 2
        for rows in (slice(0, half), slice(half, 2 * half)):
            y = _rms(x_ref[rows], g_ref[...])
            xm = (y * (1.0 + sc_ref[0]) + sh_ref[0]).astype(BF16)
            xm_ref[rows] = xm
            o_ref[rows] = _dot_nt(xm, w)

    @pl.when(j > 0)
    def _():
        o_ref[...] = _dot_nt(xm_ref[...], w)


def norm_mod_matmul(x, g, sc, sh, wt, n, rows_per_batch, tm, tn):
    m, d = x.shape
    bpb = rows_per_batch // tm
    return pl.pallas_call(
        _norm_mm_kernel,
        grid=(m // tm, n // tn),
        in_specs=[pl.BlockSpec((tm, d), lambda i, j: (i, 0)),
                  pl.BlockSpec((1, d), lambda i, j: (0, 0)),
                  pl.BlockSpec((1, 1, d), lambda i, j: (i // bpb, 0, 0)),
                  pl.BlockSpec((1, 1, d), lambda i, j: (i // bpb, 0, 0)),
                  pl.BlockSpec((tn, d), lambda i, j: (j, 0))],
        out_specs=[pl.BlockSpec((tm, tn), lambda i, j: (i, j)), pl.BlockSpec((tm, d), lambda i, j: (i, 0))],
        out_shape=[jax.ShapeDtypeStruct((m, n), F32), jax.ShapeDtypeStruct((m, d), BF16)],
        compiler_params=_cparams("parallel", "arbitrary"),
        name="norm_mod_matmul",
    )(x, g, sc, sh, wt)


def _mm_kernel(a_ref, w_ref, o_ref):
    o_ref[...] = _dot_nt(a_ref[...], w_ref[...]).astype(o_ref.dtype)


def matmul_nt(a, wt, row0, n, tm, tn, out_dtype):
    m, k = a.shape
    assert row0 % 16 == 0 and row0 + n <= wt.shape[0]
    return pl.pallas_call(
        _mm_kernel,
        grid=(m // tm, n // tn),
        in_specs=[pl.BlockSpec((tm, k), lambda i, j: (i, 0)),
                  pl.BlockSpec((pl.Element(tn), pl.Element(k)),
                               lambda i, j: (pl.multiple_of(row0 + j * tn, 16), 0))],
        out_specs=pl.BlockSpec((tm, tn), lambda i, j: (i, j)),
        out_shape=jax.ShapeDtypeStruct((m, n), out_dtype),
        compiler_params=_cparams("parallel", "arbitrary"),
        name="matmul_nt",
    )(a, wt)


def _out_proj_kernel(oa_ref, ob_ref, wa_ref, wb_ref, x_ref, g1_ref, gpost_ref, gpre_ref, sc2_ref, sh2_ref,
                     x1_ref, h_ref):
    out = _dot(oa_ref[...], wa_ref[...]) + _dot(ob_ref[...], wb_ref[...])
    x1 = x_ref[...] + g1_ref[0] * _rms(out, gpost_ref[...])
    x1_ref[...] = x1
    h_ref[...] = (_rms(x1, gpre_ref[...]) * (1.0 + sc2_ref[0]) + sh2_ref[0]).astype(BF16)


def out_proj_residual(oa, ob, wa, wb, x, g1, gpost, gpre, sc2, sh2, rows_per_batch, tm):
    m, ka = oa.shape
    kb = ob.shape[1]
    d = x.shape[1]
    bpb = rows_per_batch // tm
    row = lambda i: (i, 0)
    fixed = lambda i: (0, 0)
    per_b = lambda i: (i // bpb, 0, 0)
    return pl.pallas_call(
        _out_proj_kernel,
        grid=(m // tm,),
        in_specs=[pl.BlockSpec((tm, ka), row), pl.BlockSpec((tm, kb), row),
                  pl.BlockSpec((ka, d), fixed), pl.BlockSpec((kb, d), fixed),
                  pl.BlockSpec((tm, d), row),
                  pl.BlockSpec((1, 1, d), per_b),
                  pl.BlockSpec((1, d), fixed), pl.BlockSpec((1, d), fixed),
                  pl.BlockSpec((1, 1, d), per_b), pl.BlockSpec((1, 1, d), per_b)],
        out_specs=[pl.BlockSpec((tm, d), row), pl.BlockSpec((tm, d), row)],
        out_shape=[jax.ShapeDtypeStruct((m, d), F32), jax.ShapeDtypeStruct((m, d), BF16)],
        compiler_params=_cparams("parallel"),
        name="out_proj_residual",
    )(oa, ob, wa, wb, x, g1, gpost, gpre, sc2, sh2)


def _ffn_up_kernel(top_ref, mid_ref, bot_ref, wg_ref, wu_ref, cw_ref, cb_ref, o_ref, hext_ref, *, n_blocks):
    i = pl.program_id(1)
    tm = mid_ref.shape[0]

    @pl.when(pl.program_id(2) == 0)
    def _():
        hext_ref[0:GRID_W] = top_ref[...]
        hext_ref[GRID_W:GRID_W + tm] = mid_ref[...]
        hext_ref[GRID_W + tm:] = bot_ref[...]

    gate = _dot(hext_ref[...], wg_ref[...].astype(BF16))
    up = _dot(mid_ref[...], wu_ref[...].astype(BF16))
    tn = gate.shape[1]
    mid = gate[GRID_W:GRID_W + tm]
    above = jnp.where(i > 0, gate[:GRID_W], 0.0)
    below = jnp.where(i < n_blocks - 1, gate[GRID_W + tm:], 0.0)
    above = jnp.concatenate([above, mid[:tm - GRID_W]], axis=0)
    below = jnp.concatenate([mid[GRID_W:], below], axis=0)
    w = cw_ref[...]
    col = lax.broadcasted_iota(jnp.int32, (tm, tn), 0) % GRID_W

    def column_sum(dx):
        return above * w[dx:dx + 1] + mid * w[3 + dx:4 + dx] + below * w[6 + dx:7 + dx]

    left = jnp.where(col > 0, pltpu.roll(column_sum(0), 1, axis=0), 0.0)
    right = jnp.where(col < GRID_W - 1, pltpu.roll(column_sum(2), tm - 1, axis=0), 0.0)
    gt = column_sum(1) + left + right + cb_ref[...]
    o_ref[...] = (gt * jax.nn.sigmoid(gt) * up).astype(o_ref.dtype)


def ffn_up(h, wg, wu, w9, b, seq_len, tm, tn):
    m, d = h.shape
    f = wg.shape[1]
    nb = seq_len // tm
    bsz = m // seq_len
    hpb = tm // GRID_W
    n_halo = seq_len // GRID_W
    main = lambda b_, i, j: (b_ * nb + i, 0)
    top = lambda b_, i, j: (b_ * n_halo + jnp.maximum(i * hpb - 1, 0), 0)
    bot = lambda b_, i, j: (b_ * n_halo + jnp.minimum((i + 1) * hpb, n_halo - 1), 0)
    col = lambda b_, i, j: (0, j)
    return pl.pallas_call(
        functools.partial(_ffn_up_kernel, n_blocks=nb),
        grid=(bsz, nb, f // tn),
        in_specs=[pl.BlockSpec((GRID_W, d), top), pl.BlockSpec((tm, d), main), pl.BlockSpec((GRID_W, d), bot),
                  pl.BlockSpec((d, tn), col), pl.BlockSpec((d, tn), col),
                  pl.BlockSpec((9, tn), col), pl.BlockSpec((1, tn), col)],
        out_specs=pl.BlockSpec((tm, tn), lambda b_, i, j: (b_ * nb + i, j)),
        out_shape=jax.ShapeDtypeStruct((m, f), BF16),
        scratch_shapes=[pltpu.VMEM((tm + 2 * GRID_W, d), BF16)],
        compiler_params=_cparams("parallel", "parallel", "arbitrary"),
        name="ffn_up",
    )(h, h, h, wg, wu, w9, b)


def _down_proj_kernel(t_ref, w_ref, x1_ref, g2_ref, gpost_ref, o_ref):
    k = pl.program_id(1)
    last = pl.num_programs(1) - 1

    @pl.when(k == 0)
    def _():
        o_ref[...] = _dot(t_ref[...], w_ref[...])

    @pl.when((k > 0) & (k < last))
    def _():
        o_ref[...] += _dot(t_ref[...], w_ref[...])

    @pl.when(k == last)
    def _():
        half = o_ref.shape[0] // 2
        for rows in (slice(0, half), slice(half, 2 * half)):
            acc = o_ref[rows] + _dot(t_ref[rows], w_ref[...])
            o_ref[rows] = x1_ref[rows] + g2_ref[0] * _rms(acc, gpost_ref[...])


def down_proj_residual(t, w, x1, g2, gpost, rows_per_batch, tm, tk):
    m, kk = t.shape
    d = w.shape[1]
    bpb = rows_per_batch // tm
    return pl.pallas_call(
        _down_proj_kernel,
        grid=(m // tm, kk // tk),
        in_specs=[pl.BlockSpec((tm, tk), lambda i, k: (i, k)),
                  pl.BlockSpec((tk, d), lambda i, k: (k, 0)),
                  pl.BlockSpec((tm, d), lambda i, k: (i, 0)),
                  pl.BlockSpec((1, 1, d), lambda i, k: (i // bpb, 0, 0)),
                  pl.BlockSpec((1, d), lambda i, k: (0, 0))],
        out_specs=pl.BlockSpec((tm, d), lambda i, k: (i, 0)),
        out_shape=jax.ShapeDtypeStruct((m, d), F32),
        compiler_params=_cparams("parallel", "arbitrary"),
        name="down_proj_residual",
    )(t, w, x1, g2, gpost)


def _head_sum(x, e, et):
    return _dot_sel(et, _dot_sel(e, x, left=False), left=False)


def _rw_prep_kernel(prev_ref, cur_ref, next_ref, cw_ref, wup_ref, aup_ref, gup_ref, vec_ref, e_ref, et_ref,
                    af_ref, btf_ref, ktf_ref, rf_ref, wtf_ref, ab_ref, btb_ref, ktb_ref, rb_ref, wtb_ref,
                    v_ref, bon_ref, g_ref, *, n_blocks, width):
    i = pl.program_id(1)
    tl = cur_ref.shape[0]
    cur = cur_ref[...]
    rows = lax.broadcasted_iota(jnp.int32, cur.shape, 0)
    prev_row = jnp.where(i > 0, prev_ref[7:8, :], 0.0)
    next_row = jnp.where(i < n_blocks - 1, next_ref[0:1, :], 0.0)
    before = jnp.where(rows == 0, prev_row, pltpu.roll(cur, 1, axis=0))
    after = jnp.where(rows == tl - 1, next_row, pltpu.roll(cur, tl - 1, axis=0))
    cw = cw_ref[...]
    c = before * cw[0:1] + cur * cw[1:2] + after * cw[2:3]

    w_ = width
    k = c[:, :w_]
    v = c[:, w_:2 * w_]
    lo = c[:, 2 * w_:2 * w_ + 256]
    r = c[:, 2 * w_ + 256:3 * w_ + 256]
    gl = c[:, 3 * w_ + 256:3 * w_ + 512]
    vec = vec_ref[...]
    e = e_ref[...]
    et = et_ref[...]
    k_k, k_a, r_k = vec[4:5], vec[5:6], vec[6:7]

    kk = k * k_k
    kk = kk * lax.rsqrt(_head_sum(kk * kk, e, et) + 1e-12)
    neg_kk = -kk
    one_minus_k_a = 1.0 - k_a
    v_ref[...] = v.astype(BF16)
    tlo = jnp.tanh(lo)
    kd_sum = jnp.zeros_like(k)
    ti = lax.broadcasted_iota(jnp.int32, (tl, tl), 0)
    tj = lax.broadcasted_iota(jnp.int32, (tl, tl), 1)
    same_chunk = ti // RW_CHUNK == tj // RW_CHUNK
    wi = lax.broadcasted_iota(jnp.int32, (tl // 8, tl), 0)
    wj = lax.broadcasted_iota(jnp.int32, (tl // 8, tl), 1)
    chunk_rows = (wi * 8) // RW_CHUNK == wj // RW_CHUNK
    dirs = ((af_ref, btf_ref, ktf_ref, rf_ref, wtf_ref, tj <= ti), (ab_ref, btb_ref, ktb_ref, rb_ref, wtb_ref, tj >= ti))
    for d, (a_ref, bt_ref, kt_ref, r_ref, wt_ref, done) in enumerate(dirs):
        z = vec[d:d + 1] + _dot_x3(tlo, wup_ref[0, d], wup_ref[1, d])
        nz = -z
        softplus = jnp.maximum(nz, 0.0) + jnp.log1p(jnp.exp(-jnp.abs(nz)))
        w_log = -softplus - 0.5
        lw = jnp.exp(w_log) * -LOG2_E
        a = jax.nn.sigmoid(vec[2 + d:3 + d] + _dot_x3(lo, aup_ref[0, d], aup_ref[1, d]))
        kd = k * (a * k_a + one_minus_k_a)
        kd_sum = kd_sum + kd
        sel = jnp.concatenate([(same_chunk & done).astype(F32), chunk_rows.astype(F32)], axis=0).astype(BF16)
        sums = _dot_sel(sel, lw)
        cum = sums[:tl]
        w_inv = jnp.exp2(-cum)
        a_ref[...] = (neg_kk * jnp.exp2(cum - lw)).astype(BF16)
        bt_ref[...] = (kk * a * w_inv).astype(BF16)
        kt_ref[...] = (kd * w_inv).astype(BF16)
        r_ref[...] = (r * jnp.exp2(cum)).astype(BF16)
        wt_ref[...] = jnp.exp2(sums[tl:])
    bon_ref[...] = _head_sum(r * kd_sum * r_k, e, et) * v
    g_ref[...] = _dot_x3(jax.nn.sigmoid(gl), gup_ref[0], gup_ref[1])


def rw_prep(p, cw, wup, aup, gup, vec, e, et, seq_len, tl, width):
    bsz, _, rw_cols = p.shape
    nb = seq_len // tl
    hb = tl // 8
    n_halo = seq_len // 8
    fixed = lambda a: pl.BlockSpec(a.shape, lambda b_, i: (0,) * a.ndim)
    row_spec = pl.BlockSpec((None, tl, width), lambda b_, i: (b_, i, 0))
    wt_spec = pl.BlockSpec((None, tl // 8, width), lambda b_, i: (b_, i, 0))
    act = lambda dt: jax.ShapeDtypeStruct((bsz, seq_len, width), dt)
    wt = jax.ShapeDtypeStruct((bsz, seq_len // 8, width), F32)
    per_dir_specs = [row_spec] * 4 + [wt_spec]
    per_dir_shapes = [act(BF16)] * 4 + [wt]
    return pl.pallas_call(
        functools.partial(_rw_prep_kernel, n_blocks=nb, width=width),
        grid=(bsz, nb),
        in_specs=[pl.BlockSpec((None, 8, rw_cols), lambda b_, i: (b_, jnp.maximum(i * hb - 1, 0), 0)),
                  pl.BlockSpec((None, tl, rw_cols), lambda b_, i: (b_, i, 0)),
                  pl.BlockSpec((None, 8, rw_cols), lambda b_, i: (b_, jnp.minimum((i + 1) * hb, n_halo - 1), 0)),
                  fixed(cw), fixed(wup), fixed(aup), fixed(gup), fixed(vec), fixed(e), fixed(et)],
        out_specs=per_dir_specs * 2 + [row_spec] * 3,
        out_shape=per_dir_shapes * 2 + [act(BF16), act(F32), act(F32)],
        compiler_params=_cparams("parallel", "parallel"),
        name="rw_prep",
    )(p, p, p, cw, wup, aup, gup, vec, e, et)


def _block_diag(y, diag):
    yb = y.astype(BF16)
    tiled = jnp.concatenate([yb] * RW_PROBLEMS, axis=0)
    return jnp.where(diag, tiled, jnp.zeros_like(tiled))


def _unit_tri_inverse(a, eye, blk, off1, off2, diag):
    n = RW_CHUNK
    mm = lambda xs, ys: _each(lambda x, y: _dot(x.astype(BF16), _block_diag(y, diag)), xs, ys)
    add = lambda xs, ys: _each(jnp.add, xs, ys)
    stack = lambda xs, ys: _each(lambda x, y: jnp.concatenate([x, y], axis=0), xs, ys)
    d = [ai * blk for ai in a]
    x = [eye + di for di in d]
    d2 = mm(d, d)
    both = mm(stack(x, d2), d2)
    x = _each(lambda xi, p: xi + p[:n], x, both)
    d4 = [p[n:] for p in both]
    both = mm(stack(x, d4), d4)
    x = _each(lambda xi, p: xi + p[:n], x, both)
    d8 = [p[n:] for p in both]
    x = add(x, mm(x, d8))
    x = add(x, mm(mm(x, [ai * off1 for ai in a]), x))
    x = add(x, mm(mm(x, [ai * off2 for ai in a]), x))
    return x


def _rw_chunk_local(a, bt, kt, r, v, masks):
    strict, incl, eye, blk, off1, off2, diag = masks
    n = RW_CHUNK
    bf = lambda xs: [x.astype(BF16) for x in xs]
    bdiag = lambda xs: [_block_diag(x, diag) for x in xs]
    stack = lambda xs, ys: _each(lambda x, y: jnp.concatenate([x, y], axis=0), xs, ys)
    ar = stack(a, r)
    sc = _each(_dot_nt, ar, stack(bdiag(bt), bdiag(kt)))
    a_ab = _each(lambda x, m: jnp.where(m, x[:n, :RW_CAT], 0.0), sc, strict)
    a_ak = _each(lambda x, m: jnp.where(m, x[:n, RW_CAT:], 0.0), sc, strict)
    r_rb = _each(lambda x, m: jnp.where(m, x[n:, :RW_CAT], 0.0), sc, incl)
    r_rk = _each(lambda x, m: jnp.where(m, x[n:, RW_CAT:], 0.0), sc, incl)
    inv = _unit_tri_inverse(a_ab, eye, blk, off1, off2, diag)
    kv = _each(_dot, bf(stack(a_ak, r_rk)), bdiag(v))
    return ar, bf(inv), bf(r_rb), kv


def _rw_chunk_state(local, bt, kt, v, wtot, s, diag):
    ar, inv, r_rb, kv = local
    n = RW_CHUNK
    bdiag = lambda xs: [_block_diag(x, diag) for x in xs]
    sc_s = _each(lambda x, si: _dot_nt(x, si.astype(BF16)), ar, s)
    rhs = _each(lambda x, y: x[:n] + y[:n], sc_s, kv)
    u = _each(_dot, inv, bdiag(rhs))
    y = _each(lambda x, p, q: x[n:] + p + q[n:], sc_s, _each(_dot, r_rb, bdiag(u)), kv)
    to_end = lambda x, w: (x.astype(F32) * w).astype(BF16)
    uv = _each(lambda x, z: jnp.concatenate([x.astype(BF16), z], axis=0), u, v)
    bk = _each(lambda x, z, w: jnp.concatenate([to_end(x, w), to_end(z, w)], axis=0), bt, kt, wtot)
    upd = _each(_dot_tn, uv, bk)
    return y, _each(lambda si, w, x: si * w + jnp.where(diag, x, 0.0), s, wtot, upd)


def _rw_scan_kernel(*refs):
    fwd, bwd = refs[0:6], refs[6:12]
    s0_ref, yf_ref, yb_ref, sout_ref, s_ref = refs[12:]
    c = pl.program_id(1)

    @pl.when(c == 0)
    def _():
        s_ref[...] = s0_ref[...]

    n = RW_CHUNK
    t = lax.broadcasted_iota(jnp.int32, (n, RW_CAT), 0)
    lane = lax.broadcasted_iota(jnp.int32, (n, RW_CAT), 1)
    step = lane % n
    eye = (t == step).astype(F32)
    blk = (t // RW_SUB == step // RW_SUB).astype(F32)
    off1 = ((t // (2 * RW_SUB) == step // (2 * RW_SUB)) & (t // RW_SUB != step // RW_SUB)).astype(F32)
    off2 = (t // (2 * RW_SUB) != step // (2 * RW_SUB)).astype(F32)
    di = lax.broadcasted_iota(jnp.int32, (RW_CAT, RW_CAT), 0) // n
    dj = lax.broadcasted_iota(jnp.int32, (RW_CAT, RW_CAT), 1) // n
    diag = di == dj
    pairs = s_ref.shape[1]
    lanes = [slice(g * LANES, (g + 1) * LANES) for g in range(pairs)]
    order = [(q, RW_STEP_CHUNKS - 1 - q) for q in range(RW_STEP_CHUNKS)]
    strict = ([t > step] * pairs + [t < step] * pairs) * RW_STEP_CHUNKS
    incl = ([t >= step] * pairs + [t <= step] * pairs) * RW_STEP_CHUNKS
    masks = (strict, incl, eye, blk, off1, off2, diag)
    rows = lambda q: slice(q * n, (q + 1) * n)
    ops = [[ref[rows(q), ln] for qs in order for ref, q in zip((f, b), qs) for ln in lanes]
           for f, b in zip(fwd[:5], bwd[:5])]
    wtot = [ref[8 * q:8 * q + 1, ln] for qs in order for ref, q in zip((fwd[5], bwd[5]), qs) for ln in lanes]
    local = _rw_chunk_local(*ops, masks)
    states = [s_ref[d, g] for d in range(2) for g in range(pairs)]
    per = 2 * pairs
    for i, qs in enumerate(order):
        part = slice(i * per, (i + 1) * per)
        ys, states = _rw_chunk_state([x[part] for x in local], ops[1][part], ops[2][part], ops[4][part], wtot[part],
                                     states, diag)
        for d, (y_ref, q) in enumerate(zip((yf_ref, yb_ref), qs)):
            for g in range(pairs):
                y_ref[rows(q), lanes[g]] = ys[d * pairs + g].astype(y_ref.dtype)
    for d in range(2):
        for g in range(pairs):
            s_ref[d, g] = states[d * pairs + g]

    @pl.when(c == pl.num_programs(1) - 1)
    def _():
        sout_ref[...] = s_ref[...]


def rw_scan(fwd, bwd, s0):
    bsz, seq_len, width = fwd[0].shape
    rows = RW_STEP_CHUNKS * RW_CHUNK
    nc = seq_len // rows
    pairs = width // LANES
    f_spec = pl.BlockSpec((None, rows, width), lambda b_, c: (b_, c, 0))
    b_spec = pl.BlockSpec((None, rows, width), lambda b_, c: (b_, nc - 1 - c, 0))
    fw_spec = pl.BlockSpec((None, rows // 8, width), lambda b_, c: (b_, c, 0))
    bw_spec = pl.BlockSpec((None, rows // 8, width), lambda b_, c: (b_, nc - 1 - c, 0))
    s_spec = pl.BlockSpec((None, 2, pairs, RW_CAT, RW_CAT), lambda b_, c: (b_, 0, 0, 0, 0))
    y_shape = jax.ShapeDtypeStruct((bsz, seq_len, width), BF16)
    return pl.pallas_call(
        _rw_scan_kernel,
        grid=(bsz, nc),
        in_specs=[f_spec] * 5 + [fw_spec] + [b_spec] * 5 + [bw_spec] + [s_spec],
        out_specs=[f_spec, b_spec, s_spec],
        out_shape=[y_shape, y_shape, jax.ShapeDtypeStruct(s0.shape, F32)],
        scratch_shapes=[pltpu.VMEM((2, pairs, RW_CAT, RW_CAT), F32)],
        compiler_params=_cparams("parallel", "arbitrary"),
        name="rw_scan",
    )(*fwd, *bwd, s0)


def _rw_finish_kernel(yf_ref, yb_ref, bon_ref, g_ref, lnw_ref, lnb_ref, e_ref, et_ref, o_ref):
    y = yf_ref[...].astype(F32) + yb_ref[...].astype(F32)
    e = e_ref[...]
    et = et_ref[...]
    mu = _head_sum(y, e, et) * (1.0 / RW_HEAD_DIM)
    yc = y - mu
    var = _head_sum(yc * yc, e, et) * (1.0 / RW_HEAD_DIM)
    yn = yc * lax.rsqrt(var + LNX_EPS) * lnw_ref[...] + lnb_ref[...]
    o_ref[...] = ((yn + bon_ref[...]) * g_ref[...]).astype(o_ref.dtype)


def rw_finish(yf, yb, bon, g, lnw, lnb, e, et, tl):
    m, width = yf.shape
    row = pl.BlockSpec((tl, width), lambda i: (i, 0))
    fixed = lambda a: pl.BlockSpec(a.shape, lambda i: (0, 0))
    return pl.pallas_call(
        _rw_finish_kernel,
        grid=(m // tl,),
        in_specs=[row, row, row, row, fixed(lnw), fixed(lnb), fixed(e), fixed(et)],
        out_specs=row,
        out_shape=jax.ShapeDtypeStruct((m, width), BF16),
        compiler_params=_cparams("parallel"),
        name="rw_finish",
    )(yf, yb, bon, g, lnw, lnb, e, et)


def _rope(x, cos, sin):
    quarter = LANES // 4
    width = x.shape[1]
    lane = lax.broadcasted_iota(jnp.int32, x.shape, 1)
    first = (lane // quarter) % 2 == 0
    partner = jnp.where(first, pltpu.roll(x, width - quarter, axis=1), pltpu.roll(x, quarter, axis=1))
    return x * cos + partner * sin


def _ret_scan_kernel(kf_ref, vf_ref, qf_ref, cf_ref, sf_ref, kb_ref, vb_ref, qb_ref, cb_ref, sb_ref,
                     dec_ref, fs_ref, te_ref, gc_ref, s0_ref, yf_ref, yb_ref, sout_ref, s_ref, *, rope, scale):
    c = pl.program_id(1)

    @pl.when(c == 0)
    def _():
        s_ref[...] = s0_ref[...]

    n_heads = s_ref.shape[1]
    n = RET_CHUNK
    heads = [slice(h * LANES, (h + 1) * LANES) for h in range(n_heads)]
    rows = lambda q: slice(q * n, (q + 1) * n)
    ss = [s_ref[d, h] for d in range(2) for h in range(n_heads)]
    for qf, qb in [(q, RET_STEP_CHUNKS - 1 - q) for q in range(RET_STEP_CHUNKS)]:
        qs, ks, vs, decay = [], [], [], []
        for d, (k_ref, v_ref, q_ref, cos_ref, sin_ref, r) in enumerate(
                ((kf_ref, vf_ref, qf_ref, cf_ref, sf_ref, rows(qf)), (kb_ref, vb_ref, qb_ref, cb_ref, sb_ref, rows(qb)))):
            q = q_ref[r, :].astype(F32)
            k = k_ref[r, :].astype(F32)
            if rope:
                cos = jnp.concatenate([cos_ref[r, :]] * n_heads, axis=1)
                sin = jnp.concatenate([sin_ref[r, :]] * n_heads, axis=1)
                q = _rope(q, cos, sin)
                k = _rope(k, cos, sin)
            k = k * scale
            v = v_ref[r, :].astype(BF16)
            q_cross = (q * fs_ref[d]).astype(BF16)
            k_end = (k * te_ref[d]).astype(BF16)
            q = q.astype(BF16)
            k = k.astype(BF16)
            for h, hs in enumerate(heads):
                qs.append((q[:, hs], q_cross[:, hs]))
                ks.append((k[:, hs], k_end[:, hs]))
                vs.append(v[:, hs])
                decay.append(dec_ref[d, h])
        scores = _each(lambda q, k, dm: (_dot_nt(q[0], k[0]) * dm).astype(BF16), qs, ks, decay)
        cross = _each(lambda q, s: _dot(q[1], s.astype(BF16)), qs, ss)
        intra = _each(_dot, scores, vs)
        upd = _each(lambda k, v: _dot_tn(k[1], v), ks, vs)
        for d, (y_ref, r) in enumerate(((yf_ref, rows(qf)), (yb_ref, rows(qb)))):
            for h, hs in enumerate(heads):
                i = d * n_heads + h
                y_ref[r, hs] = (intra[i] + cross[i]).astype(y_ref.dtype)
        ss = [gc_ref[i % n_heads] * ss[i] + upd[i] for i in range(2 * n_heads)]
    for d in range(2):
        for h in range(n_heads):
            s_ref[d, h] = ss[d * n_heads + h]

    @pl.when(c == pl.num_programs(1) - 1)
    def _():
        sout_ref[...] = s_ref[...]


def ret_scan(p, cos, sin, tables, s0, rope):
    bsz, seq_len, cols = p.shape
    width = cols // 4
    rows = RET_STEP_CHUNKS * RET_CHUNK
    nc = seq_len // rows
    dec, fs, te, gc = tables

    def specs(chunk):
        cols_ = [pl.BlockSpec((None, rows, width), lambda b_, c, o=o: (b_, chunk(c), o)) for o in range(3)]
        return cols_ + [pl.BlockSpec((rows, LANES), lambda b_, c: (chunk(c), 0))] * 2

    fwd = lambda c: c
    bwd = lambda c: nc - 1 - c
    fixed = lambda a: pl.BlockSpec(a.shape, lambda b_, c: (0,) * a.ndim)
    s_spec = pl.BlockSpec((None,) + s0.shape[1:], lambda b_, c: (b_, 0, 0, 0, 0))
    y_shape = jax.ShapeDtypeStruct((bsz, seq_len, width), BF16)
    return pl.pallas_call(
        functools.partial(_ret_scan_kernel, rope=rope, scale=float(LANES) ** -0.5),
        grid=(bsz, nc),
        in_specs=specs(fwd) + specs(bwd) + [fixed(dec), fixed(fs), fixed(te), fixed(gc), s_spec],
        out_specs=[pl.BlockSpec((None, rows, width), lambda b_, c: (b_, c, 0)),
                   pl.BlockSpec((None, rows, width), lambda b_, c: (b_, nc - 1 - c, 0)),
                   s_spec],
        out_shape=[y_shape, y_shape, jax.ShapeDtypeStruct(s0.shape, F32)],
        scratch_shapes=[pltpu.VMEM(s0.shape[1:], F32)],
        compiler_params=_cparams("parallel", "arbitrary"),
        name="ret_scan",
    )(p, p, p, cos, sin, p, p, p, cos, sin, dec, fs, te, gc, s0)


def _ret_tables(n_heads):
    n = RET_CHUNK
    lg = np.log(1.0 - 2.0 ** (-5.0 - np.arange(n_heads, dtype=np.float64)))
    pos = np.arange(n, dtype=np.float64)
    diff = pos[:, None] - pos[None, :]
    masks = (diff >= 0, diff < 0)
    dists = (diff, -diff)
    dec = np.stack([np.where(m, np.exp(np.where(m, dd, 0.0)[None] * lg[:, None, None]), 0.0)
                    for m, dd in zip(masks, dists)])
    done = np.stack([pos, n - 1.0 - pos])
    lanes = np.repeat(lg, LANES)[None, None, :]
    fs = np.exp((done[:, :, None] + 1.0) * lanes)
    te = np.exp((n - 1.0 - done[:, :, None]) * lanes)
    gc = np.broadcast_to(np.exp(n * lg)[:, None, None], (n_heads, 1, LANES))
    return tuple(jnp.asarray(a, F32) for a in (dec, fs, te, gc))


def _ret_finish_kernel(yf_ref, yb_ref, g_ref, o_ref):
    g = g_ref[...].astype(F32)
    gate = g * jax.nn.sigmoid(g)
    for h in range(o_ref.shape[1] // LANES):
        hs = slice(h * LANES, (h + 1) * LANES)
        y = yf_ref[:, hs].astype(F32) + yb_ref[:, hs].astype(F32)
        y = y * lax.rsqrt(jnp.mean(y * y, axis=-1, keepdims=True) + EPS)
        o_ref[:, hs] = (y * gate[:, hs]).astype(o_ref.dtype)


def ret_finish(yf, yb, p2, tl):
    m, width = yf.shape
    row = pl.BlockSpec((tl, width), lambda i: (i, 0))
    return pl.pallas_call(
        _ret_finish_kernel,
        grid=(m // tl,),
        in_specs=[row, row, pl.BlockSpec((tl, width), lambda i: (i, 3))],
        out_specs=row,
        out_shape=jax.ShapeDtypeStruct((m, width), BF16),
        compiler_params=_cparams("parallel"),
        name="ret_finish",
    )(yf, yb, p2)


def _rope_tables(seq_len):
    n = LANES // 4
    t = np.arange(seq_len)
    inv = ROPE_BASE ** (-np.arange(n, dtype=np.float64) / n)
    ang_row = (t // GRID_W)[:, None] * inv
    ang_col = (t % GRID_W)[:, None] * inv
    cr, sr, cc, sc = np.cos(ang_row), np.sin(ang_row), np.cos(ang_col), np.sin(ang_col)
    cos = np.concatenate([cr, cr, cc, cc], axis=-1)
    sin = np.concatenate([-sr, sr, -sc, sc], axis=-1)
    return jnp.asarray(cos, F32), jnp.asarray(sin, F32)


def _pad_rows(w, rows, at):
    return jnp.zeros((rows, w.shape[1]), w.dtype).at[at:at + w.shape[0]].set(w)


def kernel(x, c, ctx, c_ctx, w_ada, b_ada, norm_pre_mix, norm_post_mix, norm_pre_ffn, norm_post_ffn, w_in, rw_conv, rw_w0, rw_w_up, rw_a0, rw_a_up, rw_g_up, rw_k_k, rw_k_a, rw_r_k, rw_lnx_w, rw_lnx_b, w_out, ffn_w_gate, ffn_w_up, ffn_conv, ffn_conv_b, ffn_w_down):
    bsz, seq_len, d = x.shape
    ctx_len = ctx.shape[1]
    n_layers = w_ada.shape[0]
    assert n_layers == 1, "context-stream outputs are only needed between layers"
    rw_w = rw_k_k.shape[1]
    ret_w = w_out.shape[1] - rw_w
    rw_cols = 3 * rw_w + 2 * DECAY_LORA + 2 * AAA_LORA + GATE_LORA
    rw_pad = -rw_cols % (2 * LANES)
    d_ff = ffn_w_gate.shape[2]
    m = bsz * seq_len
    flat = lambda a: a.reshape(-1, a.shape[-1])
    l = 0

    cond = jnp.zeros((8, d), F32).at[:bsz].set(c).at[bsz].set(c_ctx)
    mod = ada_modulation(cond, w_ada[l], b_ada[l][None])
    sh1, sc1, g1, sh2, sc2, g2 = [mm[:bsz, None, :] for mm in jnp.split(mod, 6, axis=-1)]
    sh_c = jnp.broadcast_to(mod[bsz, :d], (bsz, 1, d))
    sc_c = jnp.broadcast_to(mod[bsz, d:2 * d], (bsz, 1, d))

    w_in_t = w_in[l].T.astype(BF16)
    g_pre = norm_pre_mix[l][None]

    def in_proj(a, sc, sh, rows, tm):
        p_a, xm = norm_mod_matmul(flat(a), g_pre, sc, sh, w_in_t, rw_cols + rw_pad, rows, tm, TN_IN_RW)
        p_b = matmul_nt(xm, w_in_t, rw_cols, 4 * ret_w, tm, TN_IN_RET, BF16)
        return p_a.reshape(bsz, rows, -1), p_b.reshape(bsz, rows, -1)

    p_rw, p_ret = in_proj(x, sc1, sh1, seq_len, min(TM_IN, seq_len))
    pc_rw, pc_ret = in_proj(ctx, sc_c, sh_c, ctx_len, ctx_len)

    cw = jnp.pad(rw_conv[l], ((0, 0), (0, rw_pad)))
    lora_rows = 2 * DECAY_LORA + 2 * AAA_LORA
    split = lambda w: jnp.stack(_bf16_terms(w, 2))
    wup = split(jnp.stack([_pad_rows(rw_w_up[l][dd], lora_rows, dd * DECAY_LORA) for dd in range(2)]))
    aup = split(jnp.stack([_pad_rows(rw_a_up[l][dd], lora_rows, 2 * DECAY_LORA + dd * AAA_LORA) for dd in range(2)]))
    gup = split(_pad_rows(rw_g_up[l], 256, 0))
    vec = jnp.concatenate([rw_w0[l], rw_a0[l], rw_k_k[l][None], rw_k_a[l][None], rw_r_k[l].reshape(1, rw_w)], axis=0)
    vec = jnp.pad(vec, ((0, 1), (0, 0)))
    head_of_lane = jnp.arange(rw_w) // RW_HEAD_DIM
    e = (head_of_lane[:, None] == jnp.arange(LANES)[None, :]).astype(BF16)
    et = e.T
    prep = functools.partial(rw_prep, cw=cw, wup=wup, aup=aup, gup=gup, vec=vec, e=e, et=et, width=rw_w)
    scan_args = lambda o: (o[0:4] + [o[10], o[4]], o[5:9] + [o[10], o[9]])
    s0 = jnp.zeros((bsz, 2, rw_w // LANES, RW_CAT, RW_CAT), F32)
    outs = list(prep(pc_rw, seq_len=ctx_len, tl=min(TL_PREP, ctx_len)))
    _, _, s_rw = rw_scan(*scan_args(outs), s0)
    outs = list(prep(p_rw, seq_len=seq_len, tl=TL_PREP))
    yf, yb, _ = rw_scan(*scan_args(outs), s_rw)
    o_rw = rw_finish(flat(yf), flat(yb), flat(outs[11]), flat(outs[12]), rw_lnx_w[l][None], rw_lnx_b[l][None],
                     e, et, TL_FINISH)

    n_ret_heads = ret_w // LANES
    tables = _ret_tables(n_ret_heads)
    cos, sin = _rope_tables(seq_len)
    s0 = jnp.zeros((bsz, 2, n_ret_heads, LANES, LANES), F32)
    _, _, s_ret = ret_scan(pc_ret, cos[:ctx_len], sin[:ctx_len], tables, s0, rope=False)
    yf, yb, _ = ret_scan(p_ret, cos, sin, tables, s_ret, rope=True)
    o_ret = ret_finish(flat(yf), flat(yb), flat(p_ret), TL_FINISH)

    w_o = w_out[l].astype(BF16)
    x1, h = out_proj_residual(o_rw, o_ret, w_o[:rw_w], w_o[rw_w:], flat(x), g1, norm_post_mix[l][None],
                              norm_pre_ffn[l][None], sc2, sh2, seq_len, min(TM_OUT, seq_len))
    t = ffn_up(h, ffn_w_gate[l], ffn_w_up[l], ffn_conv[l].reshape(9, d_ff),
               ffn_conv_b[l][None], seq_len, min(TM_FFN, seq_len), TN_FFN)
    out = down_proj_residual(t, ffn_w_down[l].astype(BF16), x1, g2, norm_post_ffn[l][None], seq_len,
                             min(TM_DOWN, seq_len), TK_DOWN)
    return out.reshape(bsz, seq_len, d)
```

```python
import functools

import jax
import jax.numpy as jnp
import numpy as np
from jax import lax
from jax.experimental import pallas as pl
from jax.experimental.pallas import tpu as pltpu

F32 = jnp.float32
BF16 = jnp.bfloat16

LANES = 128
EPS = 1e-6
GRID_W = 64
RW_HEAD_DIM = 64
DECAY_LORA = 64
AAA_LORA = 64
GATE_LORA = 160
LNX_EPS = 64e-5
RET_CHUNK = 128
RET_STEP_CHUNKS = 2
ROPE_BASE = 10000.0
LOG2_E = 1.4426950408889634
RW_CHUNK = 64
RW_SUB = 16
RW_PROBLEMS = 2
RW_CAT = RW_PROBLEMS * RW_CHUNK
RW_STEP_CHUNKS = 4
VMEM_LIMIT = 56 * 1024 * 1024

TM_IN, TN_IN_RW, TN_IN_RET, TM_OUT, TM_FFN, TN_FFN, TM_DOWN, TK_DOWN = 1024, 512, 1024, 512, 1024, 512, 1024, 1408
TL_PREP, TL_FINISH = 256, 512


def _cparams(*sem):
    return pltpu.CompilerParams(dimension_semantics=sem, vmem_limit_bytes=VMEM_LIMIT)


def _dot(a, b, prec=None):
    return jnp.dot(a, b, precision=prec, preferred_element_type=F32)


def _dot_nt(a, b, prec=None):
    return lax.dot_general(a, b, (((1,), (1,)), ((), ())), precision=prec, preferred_element_type=F32)


def _dot_tn(a, b, prec=None):
    return lax.dot_general(a, b, (((0,), (0,)), ((), ())), precision=prec, preferred_element_type=F32)


def _bf16_terms(a, n):
    terms = []
    for _ in range(n):
        t = a.astype(BF16)
        terms.append(t)
        a = a - t.astype(F32)
    return terms


def _dot_sel(sel, x, left=True):
    hi, lo = _bf16_terms(x, 2)
    return (_dot(sel, hi) + _dot(sel, lo)) if left else (_dot(hi, sel) + _dot(lo, sel))


def _dot_x3(a, b_hi, b_lo):
    ah, al = _bf16_terms(a, 2)
    return _dot(ah, b_hi) + (_dot(ah, b_lo) + _dot(al, b_hi))


def _rms(x, g):
    return x * lax.rsqrt(jnp.mean(x * x, axis=-1, keepdims=True) + EPS) * g


def _each(f, *lists):
    return [f(*xs) for xs in zip(*lists)]


def _ada_kernel(s_ref, w_ref, b_ref, o_ref):
    s = s_ref[...]
    s = s * jax.nn.sigmoid(s)
    w_hi, w_lo = _bf16_terms(w_ref[...], 2)
    o_ref[...] = _dot_x3(s, w_hi, w_lo) + b_ref[...]


def ada_modulation(s, w, b, tn=1024):
    m, d = s.shape
    n = w.shape[1]
    return pl.pallas_call(
        _ada_kernel,
        grid=(n // tn,),
        in_specs=[pl.BlockSpec((m, d), lambda j: (0, 0)),
                  pl.BlockSpec((d, tn), lambda j: (0, j)),
                  pl.BlockSpec((1, tn), lambda j: (0, j))],
        out_specs=pl.BlockSpec((m, tn), lambda j: (0, j)),
        out_shape=jax.ShapeDtypeStruct((m, n), F32),
        compiler_params=_cparams("arbitrary"),
        name="ada_modulation",
    )(s, w, b)


def _norm_mm_kernel(x_ref, g_ref, sc_ref, sh_ref, w_ref, o_ref, xm_ref):
    j = pl.program_id(1)
    w = w_ref[...]

    @pl.when(j == 0)
    def _():
        half = x_ref.shape[0] // 2
        for rows in (slice(0, half), slice(half, 2 * half)):
            y = _rms(x_ref[rows], g_ref[...])
            xm = (y * (1.0 + sc_ref[0]) + sh_ref[0]).astype(BF16)
            xm_ref[rows] = xm
            o_ref[rows] = _dot_nt(xm, w)

    @pl.when(j > 0)
    def _():
        o_ref[...] = _dot_nt(xm_ref[...], w)


def norm_mod_matmul(x, g, sc, sh, wt, n, rows_per_batch, tm, tn):
    m, d = x.shape
    bpb = rows_per_batch // tm
    return pl.pallas_call(
        _norm_mm_kernel,
        grid=(m // tm, n // tn),
        in_specs=[pl.BlockSpec((tm, d), lambda i, j: (i, 0)),
                  pl.BlockSpec((1, d), lambda i, j: (0, 0)),
                  pl.BlockSpec((1, 1, d), lambda i, j: (i // bpb, 0, 0)),
                  pl.BlockSpec((1, 1, d), lambda i, j: (i // bpb, 0, 0)),
                  pl.BlockSpec((tn, d), lambda i, j: (j, 0))],
        out_specs=[pl.BlockSpec((tm, tn), lambda i, j: (i, j)), pl.BlockSpec((tm, d), lambda i, j: (i, 0))],
        out_shape=[jax.ShapeDtypeStruct((m, n), F32), jax.ShapeDtypeStruct((m, d), BF16)],
        compiler_params=_cparams("parallel", "arbitrary"),
        name="norm_mod_matmul",
    )(x, g, sc, sh, wt)


def _mm_kernel(a_ref, w_ref, o_ref):
    o_ref[...] = _dot_nt(a_ref[...], w_ref[...]).astype(o_ref.dtype)


def matmul_nt(a, wt, row0, n, tm, tn, out_dtype):
    m, k = a.shape
    assert row0 % 16 == 0 and row0 + n <= wt.shape[0]
    return pl.pallas_call(
        _mm_kernel,
        grid=(m // tm, n // tn),
        in_specs=[pl.BlockSpec((tm, k), lambda i, j: (i, 0)),
                  pl.BlockSpec((pl.Element(tn), pl.Element(k)),
                               lambda i, j: (pl.multiple_of(row0 + j * tn, 16), 0))],
        out_specs=pl.BlockSpec((tm, tn), lambda i, j: (i, j)),
        out_shape=jax.ShapeDtypeStruct((m, n), out_dtype),
        compiler_params=_cparams("parallel", "arbitrary"),
        name="matmul_nt",
    )(a, wt)


def _out_proj_kernel(oa_ref, ob_ref, wa_ref, wb_ref, x_ref, g1_ref, gpost_ref, gpre_ref, sc2_ref, sh2_ref,
                     x1_ref, h_ref):
    half = x_ref.shape[0] // 2
    for rows in (slice(0, half), slice(half, 2 * half)):
        out = _dot(oa_ref[rows], wa_ref[...]) + _dot(ob_ref[rows], wb_ref[...])
        x1 = x_ref[rows] + g1_ref[0] * _rms(out, gpost_ref[...])
        x1_ref[rows] = x1
        h_ref[rows] = (_rms(x1, gpre_ref[...]) * (1.0 + sc2_ref[0]) + sh2_ref[0]).astype(BF16)


def out_proj_residual(oa, ob, wa, wb, x, g1, gpost, gpre, sc2, sh2, rows_per_batch, tm):
    m, ka = oa.shape
    kb = ob.shape[1]
    d = x.shape[1]
    bpb = rows_per_batch // tm
    row = lambda i: (i, 0)
    fixed = lambda i: (0, 0)
    per_b = lambda i: (i // bpb, 0, 0)
    return pl.pallas_call(
        _out_proj_kernel,
        grid=(m // tm,),
        in_specs=[pl.BlockSpec((tm, ka), row), pl.BlockSpec((tm, kb), row),
                  pl.BlockSpec((ka, d), fixed), pl.BlockSpec((kb, d), fixed),
                  pl.BlockSpec((tm, d), row),
                  pl.BlockSpec((1, 1, d), per_b),
                  pl.BlockSpec((1, d), fixed), pl.BlockSpec((1, d), fixed),
                  pl.BlockSpec((1, 1, d), per_b), pl.BlockSpec((1, 1, d), per_b)],
        out_specs=[pl.BlockSpec((tm, d), row), pl.BlockSpec((tm, d), row)],
        out_shape=[jax.ShapeDtypeStruct((m, d), F32), jax.ShapeDtypeStruct((m, d), BF16)],
        compiler_params=_cparams("parallel"),
        name="out_proj_residual",
    )(oa, ob, wa, wb, x, g1, gpost, gpre, sc2, sh2)


def _ffn_up_kernel(top_ref, mid_ref, bot_ref, wg_ref, wu_ref, cw_ref, cb_ref, o_ref, hext_ref, *, n_blocks):
    i = pl.program_id(1)
    tm = mid_ref.shape[0]

    @pl.when(pl.program_id(2) == 0)
    def _():
        hext_ref[0:GRID_W] = top_ref[...]
        hext_ref[GRID_W:GRID_W + tm] = mid_ref[...]
        hext_ref[GRID_W + tm:] = bot_ref[...]

    gate = _dot(hext_ref[...], wg_ref[...].astype(BF16))
    up = _dot(mid_ref[...], wu_ref[...].astype(BF16))
    tn = gate.shape[1]
    mid = gate[GRID_W:GRID_W + tm]
    above = jnp.where(i > 0, gate[:GRID_W], 0.0)
    below = jnp.where(i < n_blocks - 1, gate[GRID_W + tm:], 0.0)
    above = jnp.concatenate([above, mid[:tm - GRID_W]], axis=0)
    below = jnp.concatenate([mid[GRID_W:], below], axis=0)
    w = cw_ref[...]
    col = lax.broadcasted_iota(jnp.int32, (tm, tn), 0) % GRID_W

    def column_sum(dx):
        return above * w[dx:dx + 1] + mid * w[3 + dx:4 + dx] + below * w[6 + dx:7 + dx]

    left = jnp.where(col > 0, pltpu.roll(column_sum(0), 1, axis=0), 0.0)
    right = jnp.where(col < GRID_W - 1, pltpu.roll(column_sum(2), tm - 1, axis=0), 0.0)
    gt = column_sum(1) + left + right + cb_ref[...]
    o_ref[...] = (gt * jax.nn.sigmoid(gt) * up).astype(o_ref.dtype)


def ffn_up(h, wg, wu, w9, b, seq_len, tm, tn):
    m, d = h.shape
    f = wg.shape[1]
    nb = seq_len // tm
    bsz = m // seq_len
    hpb = tm // GRID_W
    n_halo = seq_len // GRID_W
    main = lambda b_, i, j: (b_ * nb + i, 0)
    top = lambda b_, i, j: (b_ * n_halo + jnp.maximum(i * hpb - 1, 0), 0)
    bot = lambda b_, i, j: (b_ * n_halo + jnp.minimum((i + 1) * hpb, n_halo - 1), 0)
    col = lambda b_, i, j: (0, j)
    return pl.pallas_call(
        functools.partial(_ffn_up_kernel, n_blocks=nb),
        grid=(bsz, nb, f // tn),
        in_specs=[pl.BlockSpec((GRID_W, d), top), pl.BlockSpec((tm, d), main), pl.BlockSpec((GRID_W, d), bot),
                  pl.BlockSpec((d, tn), col), pl.BlockSpec((d, tn), col),
                  pl.BlockSpec((9, tn), col), pl.BlockSpec((1, tn), col)],
        out_specs=pl.BlockSpec((tm, tn), lambda b_, i, j: (b_ * nb + i, j)),
        out_shape=jax.ShapeDtypeStruct((m, f), BF16),
        scratch_shapes=[pltpu.VMEM((tm + 2 * GRID_W, d), BF16)],
        compiler_params=_cparams("parallel", "parallel", "arbitrary"),
        name="ffn_up",
    )(h, h, h, wg, wu, w9, b)


def _down_proj_kernel(t_ref, w_ref, x1_ref, g2_ref, gpost_ref, o_ref):
    k = pl.program_id(1)
    last = pl.num_programs(1) - 1

    @pl.when(k == 0)
    def _():
        o_ref[...] = _dot(t_ref[...], w_ref[...])

    @pl.when((k > 0) & (k < last))
    def _():
        o_ref[...] += _dot(t_ref[...], w_ref[...])

    @pl.when(k == last)
    def _():
        half = o_ref.shape[0] // 2
        for rows in (slice(0, half), slice(half, 2 * half)):
            acc = o_ref[rows] + _dot(t_ref[rows], w_ref[...])
            o_ref[rows] = x1_ref[rows] + g2_ref[0] * _rms(acc, gpost_ref[...])


def down_proj_residual(t, w, x1, g2, gpost, rows_per_batch, tm, tk):
    m, kk = t.shape
    d = w.shape[1]
    bpb = rows_per_batch // tm
    return pl.pallas_call(
        _down_proj_kernel,
        grid=(m // tm, kk // tk),
        in_specs=[pl.BlockSpec((tm, tk), lambda i, k: (i, k)),
                  pl.BlockSpec((tk, d), lambda i, k: (k, 0)),
                  pl.BlockSpec((tm, d), lambda i, k: (i, 0)),
                  pl.BlockSpec((1, 1, d), lambda i, k: (i // bpb, 0, 0)),
                  pl.BlockSpec((1, d), lambda i, k: (0, 0))],
        out_specs=pl.BlockSpec((tm, d), lambda i, k: (i, 0)),
        out_shape=jax.ShapeDtypeStruct((m, d), F32),
        compiler_params=_cparams("parallel", "arbitrary"),
        name="down_proj_residual",
    )(t, w, x1, g2, gpost)


def _head_sum(x, e, et):
    return _dot_sel(et, _dot_sel(e, x, left=False), left=False)


def _rw_prep_kernel(prev_ref, cur_ref, next_ref, cw_ref, wup_ref, aup_ref, gup_ref, vec_ref, e_ref, et_ref,
                    af_ref, btf_ref, ktf_ref, rf_ref, wtf_ref, ab_ref, btb_ref, ktb_ref, rb_ref, wtb_ref,
                    v_ref, bon_ref, g_ref, *, n_blocks, width):
    i = pl.program_id(1)
    tl = cur_ref.shape[0]
    cur = cur_ref[...]
    rows = lax.broadcasted_iota(jnp.int32, cur.shape, 0)
    prev_row = jnp.where(i > 0, prev_ref[7:8, :], 0.0)
    next_row = jnp.where(i < n_blocks - 1, next_ref[0:1, :], 0.0)
    before = jnp.where(rows == 0, prev_row, pltpu.roll(cur, 1, axis=0))
    after = jnp.where(rows == tl - 1, next_row, pltpu.roll(cur, tl - 1, axis=0))
    cw = cw_ref[...]
    c = before * cw[0:1] + cur * cw[1:2] + after * cw[2:3]

    w_ = width
    k = c[:, :w_]
    v = c[:, w_:2 * w_]
    lo = c[:, 2 * w_:2 * w_ + 256]
    r = c[:, 2 * w_ + 256:3 * w_ + 256]
    gl = c[:, 3 * w_ + 256:3 * w_ + 512]
    vec = vec_ref[...]
    e = e_ref[...]
    et = et_ref[...]
    k_k, k_a, r_k = vec[4:5], vec[5:6], vec[6:7]

    kk = k * k_k
    kk = kk * lax.rsqrt(_head_sum(kk * kk, e, et) + 1e-12)
    neg_kk = -kk
    one_minus_k_a = 1.0 - k_a
    v_ref[...] = v.astype(BF16)
    tlo = jnp.tanh(lo)
    kd_sum = jnp.zeros_like(k)
    ti = lax.broadcasted_iota(jnp.int32, (tl, tl), 0)
    tj = lax.broadcasted_iota(jnp.int32, (tl, tl), 1)
    same_chunk = ti // RW_CHUNK == tj // RW_CHUNK
    wi = lax.broadcasted_iota(jnp.int32, (tl // 8, tl), 0)
    wj = lax.broadcasted_iota(jnp.int32, (tl // 8, tl), 1)
    chunk_rows = (wi * 8) // RW_CHUNK == wj // RW_CHUNK
    dirs = ((af_ref, btf_ref, ktf_ref, rf_ref, wtf_ref, tj <= ti), (ab_ref, btb_ref, ktb_ref, rb_ref, wtb_ref, tj >= ti))
    for d, (a_ref, bt_ref, kt_ref, r_ref, wt_ref, done) in enumerate(dirs):
        z = vec[d:d + 1] + _dot_x3(tlo, wup_ref[0, d], wup_ref[1, d])
        nz = -z
        softplus = jnp.maximum(nz, 0.0) + jnp.log1p(jnp.exp(-jnp.abs(nz)))
        w_log = -softplus - 0.5
        lw = jnp.exp(w_log) * -LOG2_E
        a = jax.nn.sigmoid(vec[2 + d:3 + d] + _dot_x3(lo, aup_ref[0, d], aup_ref[1, d]))
        kd = k * (a * k_a + one_minus_k_a)
        kd_sum = kd_sum + kd
        sel = jnp.concatenate([(same_chunk & done).astype(F32), chunk_rows.astype(F32)], axis=0).astype(BF16)
        sums = _dot_sel(sel, lw)
        cum = sums[:tl]
        w_inv = jnp.exp2(-cum)
        a_ref[...] = (neg_kk * jnp.exp2(cum - lw)).astype(BF16)
        bt_ref[...] = (kk * a * w_inv).astype(BF16)
        kt_ref[...] = (kd * w_inv).astype(BF16)
        r_ref[...] = (r * jnp.exp2(cum)).astype(BF16)
        wt_ref[...] = jnp.exp2(sums[tl:])
    bon_ref[...] = _head_sum(r * kd_sum * r_k, e, et) * v
    g_ref[...] = _dot_x3(jax.nn.sigmoid(gl), gup_ref[0], gup_ref[1])


def rw_prep(p, cw, wup, aup, gup, vec, e, et, seq_len, tl, width):
    bsz, _, rw_cols = p.shape
    nb = seq_len // tl
    hb = tl // 8
    n_halo = seq_len // 8
    fixed = lambda a: pl.BlockSpec(a.shape, lambda b_, i: (0,) * a.ndim)
    row_spec = pl.BlockSpec((None, tl, width), lambda b_, i: (b_, i, 0))
    wt_spec = pl.BlockSpec((None, tl // 8, width), lambda b_, i: (b_, i, 0))
    act = lambda dt: jax.ShapeDtypeStruct((bsz, seq_len, width), dt)
    wt = jax.ShapeDtypeStruct((bsz, seq_len // 8, width), F32)
    per_dir_specs = [row_spec] * 4 + [wt_spec]
    per_dir_shapes = [act(BF16)] * 4 + [wt]
    return pl.pallas_call(
        functools.partial(_rw_prep_kernel, n_blocks=nb, width=width),
        grid=(bsz, nb),
        in_specs=[pl.BlockSpec((None, 8, rw_cols), lambda b_, i: (b_, jnp.maximum(i * hb - 1, 0), 0)),
                  pl.BlockSpec((None, tl, rw_cols), lambda b_, i: (b_, i, 0)),
                  pl.BlockSpec((None, 8, rw_cols), lambda b_, i: (b_, jnp.minimum((i + 1) * hb, n_halo - 1), 0)),
                  fixed(cw), fixed(wup), fixed(aup), fixed(gup), fixed(vec), fixed(e), fixed(et)],
        out_specs=per_dir_specs * 2 + [row_spec] * 3,
        out_shape=per_dir_shapes * 2 + [act(BF16), act(F32), act(F32)],
        compiler_params=_cparams("parallel", "parallel"),
        name="rw_prep",
    )(p, p, p, cw, wup, aup, gup, vec, e, et)


def _block_diag(y, diag):
    yb = y.astype(BF16)
    tiled = jnp.concatenate([yb] * RW_PROBLEMS, axis=0)
    return jnp.where(diag, tiled, jnp.zeros_like(tiled))


def _unit_tri_inverse(a, eye, blk, off1, off2, diag):
    n = RW_CHUNK
    mm = lambda xs, ys: _each(lambda x, y: _dot(x.astype(BF16), _block_diag(y, diag)), xs, ys)
    add = lambda xs, ys: _each(jnp.add, xs, ys)
    stack = lambda xs, ys: _each(lambda x, y: jnp.concatenate([x, y], axis=0), xs, ys)
    d = [ai * blk for ai in a]
    x = [eye + di for di in d]
    d2 = mm(d, d)
    both = mm(stack(x, d2), d2)
    x = _each(lambda xi, p: xi + p[:n], x, both)
    d4 = [p[n:] for p in both]
    both = mm(stack(x, d4), d4)
    x = _each(lambda xi, p: xi + p[:n], x, both)
    d8 = [p[n:] for p in both]
    x = add(x, mm(x, d8))
    x = add(x, mm(mm(x, [ai * off1 for ai in a]), x))
    x = add(x, mm(mm(x, [ai * off2 for ai in a]), x))
    return x


def _rw_chunk_local(a, bt, kt, r, v, masks):
    strict, incl, eye, blk, off1, off2, diag = masks
    n = RW_CHUNK
    bf = lambda xs: [x.astype(BF16) for x in xs]
    bdiag = lambda xs: [_block_diag(x, diag) for x in xs]
    stack = lambda xs, ys: _each(lambda x, y: jnp.concatenate([x, y], axis=0), xs, ys)
    ar = stack(a, r)
    sc = _each(_dot_nt, ar, stack(bdiag(bt), bdiag(kt)))
    a_ab = _each(lambda x, m: jnp.where(m, x[:n, :RW_CAT], 0.0), sc, strict)
    a_ak = _each(lambda x, m: jnp.where(m, x[:n, RW_CAT:], 0.0), sc, strict)
    r_rb = _each(lambda x, m: jnp.where(m, x[n:, :RW_CAT], 0.0), sc, incl)
    r_rk = _each(lambda x, m: jnp.where(m, x[n:, RW_CAT:], 0.0), sc, incl)
    inv = _unit_tri_inverse(a_ab, eye, blk, off1, off2, diag)
    kv = _each(_dot, bf(stack(a_ak, r_rk)), bdiag(v))
    return ar, bf(inv), bf(r_rb), kv


def _rw_chunk_state(local, bt, kt, v, wtot, s, diag):
    ar, inv, r_rb, kv = local
    n = RW_CHUNK
    bdiag = lambda xs: [_block_diag(x, diag) for x in xs]
    sc_s = _each(lambda x, si: _dot_nt(x, si.astype(BF16)), ar, s)
    rhs = _each(lambda x, y: x[:n] + y[:n], sc_s, kv)
    u = _each(_dot, inv, bdiag(rhs))
    y = _each(lambda x, p, q: x[n:] + p + q[n:], sc_s, _each(_dot, r_rb, bdiag(u)), kv)
    to_end = lambda x, w: (x.astype(F32) * w).astype(BF16)
    uv = _each(lambda x, z: jnp.concatenate([x.astype(BF16), z], axis=0), u, v)
    bk = _each(lambda x, z, w: jnp.concatenate([to_end(x, w), to_end(z, w)], axis=0), bt, kt, wtot)
    upd = _each(_dot_tn, uv, bk)
    return y, _each(lambda si, w, x: si * w + jnp.where(diag, x, 0.0), s, wtot, upd)


def _rw_scan_kernel(*refs):
    fwd, bwd = refs[0:6], refs[6:12]
    s0_ref, yf_ref, yb_ref, sout_ref, s_ref = refs[12:]
    c = pl.program_id(1)

    @pl.when(c == 0)
    def _():
        s_ref[...] = s0_ref[...]

    n = RW_CHUNK
    t = lax.broadcasted_iota(jnp.int32, (n, RW_CAT), 0)
    lane = lax.broadcasted_iota(jnp.int32, (n, RW_CAT), 1)
    step = lane % n
    eye = (t == step).astype(F32)
    blk = (t // RW_SUB == step // RW_SUB).astype(F32)
    off1 = ((t // (2 * RW_SUB) == step // (2 * RW_SUB)) & (t // RW_SUB != step // RW_SUB)).astype(F32)
    off2 = (t // (2 * RW_SUB) != step // (2 * RW_SUB)).astype(F32)
    di = lax.broadcasted_iota(jnp.int32, (RW_CAT, RW_CAT), 0) // n
    dj = lax.broadcasted_iota(jnp.int32, (RW_CAT, RW_CAT), 1) // n
    diag = di == dj
    pairs = s_ref.shape[1]
    lanes = [slice(g * LANES, (g + 1) * LANES) for g in range(pairs)]
    order = [(q, RW_STEP_CHUNKS - 1 - q) for q in range(RW_STEP_CHUNKS)]
    strict = ([t > step] * pairs + [t < step] * pairs) * RW_STEP_CHUNKS
    incl = ([t >= step] * pairs + [t <= step] * pairs) * RW_STEP_CHUNKS
    masks = (strict, incl, eye, blk, off1, off2, diag)
    rows = lambda q: slice(q * n, (q + 1) * n)
    ops = [[ref[rows(q), ln] for qs in order for ref, q in zip((f, b), qs) for ln in lanes]
           for f, b in zip(fwd[:5], bwd[:5])]
    wtot = [ref[8 * q:8 * q + 1, ln] for qs in order for ref, q in zip((fwd[5], bwd[5]), qs) for ln in lanes]
    local = _rw_chunk_local(*ops, masks)
    states = [s_ref[d, g] for d in range(2) for g in range(pairs)]
    per = 2 * pairs
    for i, qs in enumerate(order):
        part = slice(i * per, (i + 1) * per)
        ys, states = _rw_chunk_state([x[part] for x in local], ops[1][part], ops[2][part], ops[4][part], wtot[part],
                                     states, diag)
        for d, (y_ref, q) in enumerate(zip((yf_ref, yb_ref), qs)):
            for g in range(pairs):
                y_ref[rows(q), lanes[g]] = ys[d * pairs + g].astype(y_ref.dtype)
    for d in range(2):
        for g in range(pairs):
            s_ref[d, g] = states[d * pairs + g]

    @pl.when(c == pl.num_programs(1) - 1)
    def _():
        sout_ref[...] = s_ref[...]


def rw_scan(fwd, bwd, s0):
    bsz, seq_len, width = fwd[0].shape
    rows = RW_STEP_CHUNKS * RW_CHUNK
    nc = seq_len // rows
    pairs = width // LANES
    f_spec = pl.BlockSpec((None, rows, width), lambda b_, c: (b_, c, 0))
    b_spec = pl.BlockSpec((None, rows, width), lambda b_, c: (b_, nc - 1 - c, 0))
    fw_spec = pl.BlockSpec((None, rows // 8, width), lambda b_, c: (b_, c, 0))
    bw_spec = pl.BlockSpec((None, rows // 8, width), lambda b_, c: (b_, nc - 1 - c, 0))
    s_spec = pl.BlockSpec((None, 2, pairs, RW_CAT, RW_CAT), lambda b_, c: (b_, 0, 0, 0, 0))
    y_shape = jax.ShapeDtypeStruct((bsz, seq_len, width), BF16)
    return pl.pallas_call(
        _rw_scan_kernel,
        grid=(bsz, nc),
        in_specs=[f_spec] * 5 + [fw_spec] + [b_spec] * 5 + [bw_spec] + [s_spec],
        out_specs=[f_spec, b_spec, s_spec],
        out_shape=[y_shape, y_shape, jax.ShapeDtypeStruct(s0.shape, F32)],
        scratch_shapes=[pltpu.VMEM((2, pairs, RW_CAT, RW_CAT), F32)],
        compiler_params=_cparams("parallel", "arbitrary"),
        name="rw_scan",
    )(*fwd, *bwd, s0)


def _rw_finish_kernel(yf_ref, yb_ref, bon_ref, g_ref, lnw_ref, lnb_ref, e_ref, et_ref, o_ref):
    y = yf_ref[...].astype(F32) + yb_ref[...].astype(F32)
    e = e_ref[...]
    et = et_ref[...]
    mu = _head_sum(y, e, et) * (1.0 / RW_HEAD_DIM)
    yc = y - mu
    var = _head_sum(yc * yc, e, et) * (1.0 / RW_HEAD_DIM)
    yn = yc * lax.rsqrt(var + LNX_EPS) * lnw_ref[...] + lnb_ref[...]
    o_ref[...] = ((yn + bon_ref[...]) * g_ref[...]).astype(o_ref.dtype)


def rw_finish(yf, yb, bon, g, lnw, lnb, e, et, tl):
    m, width = yf.shape
    row = pl.BlockSpec((tl, width), lambda i: (i, 0))
    fixed = lambda a: pl.BlockSpec(a.shape, lambda i: (0, 0))
    return pl.pallas_call(
        _rw_finish_kernel,
        grid=(m // tl,),
        in_specs=[row, row, row, row, fixed(lnw), fixed(lnb), fixed(e), fixed(et)],
        out_specs=row,
        out_shape=jax.ShapeDtypeStruct((m, width), BF16),
        compiler_params=_cparams("parallel"),
        name="rw_finish",
    )(yf, yb, bon, g, lnw, lnb, e, et)


def _rope(x, cos, sin):
    quarter = LANES // 4
    width = x.shape[1]
    lane = lax.broadcasted_iota(jnp.int32, x.shape, 1)
    first = (lane // quarter) % 2 == 0
    partner = jnp.where(first, pltpu.roll(x, width - quarter, axis=1), pltpu.roll(x, quarter, axis=1))
    return x * cos + partner * sin


def _ret_scan_kernel(kf_ref, vf_ref, qf_ref, cf_ref, sf_ref, kb_ref, vb_ref, qb_ref, cb_ref, sb_ref,
                     dec_ref, fs_ref, te_ref, gc_ref, s0_ref, yf_ref, yb_ref, sout_ref, s_ref, *, rope, scale):
    c = pl.program_id(1)

    @pl.when(c == 0)
    def _():
        s_ref[...] = s0_ref[...]

    n_heads = s_ref.shape[1]
    n = RET_CHUNK
    heads = [slice(h * LANES, (h + 1) * LANES) for h in range(n_heads)]
    rows = lambda q: slice(q * n, (q + 1) * n)
    ss = [s_ref[d, h] for d in range(2) for h in range(n_heads)]
    for qf, qb in [(q, RET_STEP_CHUNKS - 1 - q) for q in range(RET_STEP_CHUNKS)]:
        qs, ks, vs, decay = [], [], [], []
        for d, (k_ref, v_ref, q_ref, cos_ref, sin_ref, r) in enumerate(
                ((kf_ref, vf_ref, qf_ref, cf_ref, sf_ref, rows(qf)), (kb_ref, vb_ref, qb_ref, cb_ref, sb_ref, rows(qb)))):
            q = q_ref[r, :].astype(F32)
            k = k_ref[r, :].astype(F32)
            if rope:
                cos = jnp.concatenate([cos_ref[r, :]] * n_heads, axis=1)
                sin = jnp.concatenate([sin_ref[r, :]] * n_heads, axis=1)
                q = _rope(q, cos, sin)
                k = _rope(k, cos, sin)
            k = k * scale
            v = v_ref[r, :].astype(BF16)
            q_cross = (q * fs_ref[d]).astype(BF16)
            k_end = (k * te_ref[d]).astype(BF16)
            q = q.astype(BF16)
            k = k.astype(BF16)
            for h, hs in enumerate(heads):
                qs.append((q[:, hs], q_cross[:, hs]))
                ks.append((k[:, hs], k_end[:, hs]))
                vs.append(v[:, hs])
                decay.append(dec_ref[d, h])
        scores = _each(lambda q, k, dm: (_dot_nt(q[0], k[0]) * dm).astype(BF16), qs, ks, decay)
        cross = _each(lambda q, s: _dot(q[1], s.astype(BF16)), qs, ss)
        intra = _each(_dot, scores, vs)
        upd = _each(lambda k, v: _dot_tn(k[1], v), ks, vs)
        for d, (y_ref, r) in enumerate(((yf_ref, rows(qf)), (yb_ref, rows(qb)))):
            for h, hs in enumerate(heads):
                i = d * n_heads + h
                y_ref[r, hs] = (intra[i] + cross[i]).astype(y_ref.dtype)
        ss = [gc_ref[i % n_heads] * ss[i] + upd[i] for i in range(2 * n_heads)]
    for d in range(2):
        for h in range(n_heads):
            s_ref[d, h] = ss[d * n_heads + h]

    @pl.when(c == pl.num_programs(1) - 1)
    def _():
        sout_ref[...] = s_ref[...]


def ret_scan(p, cos, sin, tables, s0, rope):
    bsz, seq_len, cols = p.shape
    width = cols // 4
    rows = RET_STEP_CHUNKS * RET_CHUNK
    nc = seq_len // rows
    dec, fs, te, gc = tables

    def specs(chunk):
        cols_ = [pl.BlockSpec((None, rows, width), lambda b_, c, o=o: (b_, chunk(c), o)) for o in range(3)]
        return cols_ + [pl.BlockSpec((rows, LANES), lambda b_, c: (chunk(c), 0))] * 2

    fwd = lambda c: c
    bwd = lambda c: nc - 1 - c
    fixed = lambda a: pl.BlockSpec(a.shape, lambda b_, c: (0,) * a.ndim)
    s_spec = pl.BlockSpec((None,) + s0.shape[1:], lambda b_, c: (b_, 0, 0, 0, 0))
    y_shape = jax.ShapeDtypeStruct((bsz, seq_len, width), BF16)
    return pl.pallas_call(
        functools.partial(_ret_scan_kernel, rope=rope, scale=float(LANES) ** -0.5),
        grid=(bsz, nc),
        in_specs=specs(fwd) + specs(bwd) + [fixed(dec), fixed(fs), fixed(te), fixed(gc), s_spec],
        out_specs=[pl.BlockSpec((None, rows, width), lambda b_, c: (b_, c, 0)),
                   pl.BlockSpec((None, rows, width), lambda b_, c: (b_, nc - 1 - c, 0)),
                   s_spec],
        out_shape=[y_shape, y_shape, jax.ShapeDtypeStruct(s0.shape, F32)],
        scratch_shapes=[pltpu.VMEM(s0.shape[1:], F32)],
        compiler_params=_cparams("parallel", "arbitrary"),
        name="ret_scan",
    )(p, p, p, cos, sin, p, p, p, cos, sin, dec, fs, te, gc, s0)


def _ret_tables(n_heads):
    n = RET_CHUNK
    lg = np.log(1.0 - 2.0 ** (-5.0 - np.arange(n_heads, dtype=np.float64)))
    pos = np.arange(n, dtype=np.float64)
    diff = pos[:, None] - pos[None, :]
    masks = (diff >= 0, diff < 0)
    dists = (diff, -diff)
    dec = np.stack([np.where(m, np.exp(np.where(m, dd, 0.0)[None] * lg[:, None, None]), 0.0)
                    for m, dd in zip(masks, dists)])
    done = np.stack([pos, n - 1.0 - pos])
    lanes = np.repeat(lg, LANES)[None, None, :]
    fs = np.exp((done[:, :, None] + 1.0) * lanes)
    te = np.exp((n - 1.0 - done[:, :, None]) * lanes)
    gc = np.broadcast_to(np.exp(n * lg)[:, None, None], (n_heads, 1, LANES))
    return tuple(jnp.asarray(a, F32) for a in (dec, fs, te, gc))


def _ret_finish_kernel(yf_ref, yb_ref, g_ref, o_ref):
    g = g_ref[...].astype(F32)
    gate = g * jax.nn.sigmoid(g)
    for h in range(o_ref.shape[1] // LANES):
        hs = slice(h * LANES, (h + 1) * LANES)
        y = yf_ref[:, hs].astype(F32) + yb_ref[:, hs].astype(F32)
        y = y * lax.rsqrt(jnp.mean(y * y, axis=-1, keepdims=True) + EPS)
        o_ref[:, hs] = (y * gate[:, hs]).astype(o_ref.dtype)


def ret_finish(yf, yb, p2, tl):
    m, width = yf.shape
    row = pl.BlockSpec((tl, width), lambda i: (i, 0))
    return pl.pallas_call(
        _ret_finish_kernel,
        grid=(m // tl,),
        in_specs=[row, row, pl.BlockSpec((tl, width), lambda i: (i, 3))],
        out_specs=row,
        out_shape=jax.ShapeDtypeStruct((m, width), BF16),
        compiler_params=_cparams("parallel"),
        name="ret_finish",
    )(yf, yb, p2)


def _rope_tables(seq_len):
    n = LANES // 4
    t = np.arange(seq_len)
    inv = ROPE_BASE ** (-np.arange(n, dtype=np.float64) / n)
    ang_row = (t // GRID_W)[:, None] * inv
    ang_col = (t % GRID_W)[:, None] * inv
    cr, sr, cc, sc = np.cos(ang_row), np.sin(ang_row), np.cos(ang_col), np.sin(ang_col)
    cos = np.concatenate([cr, cr, cc, cc], axis=-1)
    sin = np.concatenate([-sr, sr, -sc, sc], axis=-1)
    return jnp.asarray(cos, F32), jnp.asarray(sin, F32)


def _pad_rows(w, rows, at):
    return jnp.zeros((rows, w.shape[1]), w.dtype).at[at:at + w.shape[0]].set(w)


def kernel(x, c, ctx, c_ctx, w_ada, b_ada, norm_pre_mix, norm_post_mix, norm_pre_ffn, norm_post_ffn, w_in, rw_conv, rw_w0, rw_w_up, rw_a0, rw_a_up, rw_g_up, rw_k_k, rw_k_a, rw_r_k, rw_lnx_w, rw_lnx_b, w_out, ffn_w_gate, ffn_w_up, ffn_conv, ffn_conv_b, ffn_w_down):
    bsz, seq_len, d = x.shape
    ctx_len = ctx.shape[1]
    n_layers = w_ada.shape[0]
    assert n_layers == 1, "context-stream outputs are only needed between layers"
    rw_w = rw_k_k.shape[1]
    ret_w = w_out.shape[1] - rw_w
    rw_cols = 3 * rw_w + 2 * DECAY_LORA + 2 * AAA_LORA + GATE_LORA
    rw_pad = -rw_cols % (2 * LANES)
    d_ff = ffn_w_gate.shape[2]
    m = bsz * seq_len
    flat = lambda a: a.reshape(-1, a.shape[-1])
    l = 0

    cond = jnp.zeros((8, d), F32).at[:bsz].set(c).at[bsz].set(c_ctx)
    mod = ada_modulation(cond, w_ada[l], b_ada[l][None])
    sh1, sc1, g1, sh2, sc2, g2 = [mm[:bsz, None, :] for mm in jnp.split(mod, 6, axis=-1)]
    sh_c = jnp.broadcast_to(mod[bsz, :d], (bsz, 1, d))
    sc_c = jnp.broadcast_to(mod[bsz, d:2 * d], (bsz, 1, d))

    w_in_t = w_in[l].T.astype(BF16)
    g_pre = norm_pre_mix[l][None]

    def in_proj(a, sc, sh, rows, tm):
        p_a, xm = norm_mod_matmul(flat(a), g_pre, sc, sh, w_in_t, rw_cols + rw_pad, rows, tm, TN_IN_RW)
        p_b = matmul_nt(xm, w_in_t, rw_cols, 4 * ret_w, tm, TN_IN_RET, BF16)
        return p_a.reshape(bsz, rows, -1), p_b.reshape(bsz, rows, -1)

    p_rw, p_ret = in_proj(x, sc1, sh1, seq_len, min(TM_IN, seq_len))
    pc_rw, pc_ret = in_proj(ctx, sc_c, sh_c, ctx_len, ctx_len)

    cw = jnp.pad(rw_conv[l], ((0, 0), (0, rw_pad)))
    lora_rows = 2 * DECAY_LORA + 2 * AAA_LORA
    split = lambda w: jnp.stack(_bf16_terms(w, 2))
    wup = split(jnp.stack([_pad_rows(rw_w_up[l][dd], lora_rows, dd * DECAY_LORA) for dd in range(2)]))
    aup = split(jnp.stack([_pad_rows(rw_a_up[l][dd], lora_rows, 2 * DECAY_LORA + dd * AAA_LORA) for dd in range(2)]))
    gup = split(_pad_rows(rw_g_up[l], 256, 0))
    vec = jnp.concatenate([rw_w0[l], rw_a0[l], rw_k_k[l][None], rw_k_a[l][None], rw_r_k[l].reshape(1, rw_w)], axis=0)
    vec = jnp.pad(vec, ((0, 1), (0, 0)))
    head_of_lane = jnp.arange(rw_w) // RW_HEAD_DIM
    e = (head_of_lane[:, None] == jnp.arange(LANES)[None, :]).astype(BF16)
    et = e.T
    prep = functools.partial(rw_prep, cw=cw, wup=wup, aup=aup, gup=gup, vec=vec, e=e, et=et, width=rw_w)
    scan_args = lambda o: (o[0:4] + [o[10], o[4]], o[5:9] + [o[10], o[9]])
    s0 = jnp.zeros((bsz, 2, rw_w // LANES, RW_CAT, RW_CAT), F32)
    outs = list(prep(pc_rw, seq_len=ctx_len, tl=min(TL_PREP, ctx_len)))
    _, _, s_rw = rw_scan(*scan_args(outs), s0)
    outs = list(prep(p_rw, seq_len=seq_len, tl=TL_PREP))
    yf, yb, _ = rw_scan(*scan_args(outs), s_rw)
    o_rw = rw_finish(flat(yf), flat(yb), flat(outs[11]), flat(outs[12]), rw_lnx_w[l][None], rw_lnx_b[l][None],
                     e, et, TL_FINISH)

    n_ret_heads = ret_w // LANES
    tables = _ret_tables(n_ret_heads)
    cos, sin = _rope_tables(seq_len)
    s0 = jnp.zeros((bsz, 2, n_ret_heads, LANES, LANES), F32)
    _, _, s_ret = ret_scan(pc_ret, cos[:ctx_len], sin[:ctx_len], tables, s0, rope=False)
    yf, yb, _ = ret_scan(p_ret, cos, sin, tables, s_ret, rope=True)
    o_ret = ret_finish(flat(yf), flat(yb), flat(p_ret), TL_FINISH)

    w_o = w_out[l].astype(BF16)
    x1, h = out_proj_residual(o_rw, o_ret, w_o[:rw_w], w_o[rw_w:], flat(x), g1, norm_post_mix[l][None],
                              norm_pre_ffn[l][None], sc2, sh2, seq_len, min(TM_OUT, seq_len))
    t = ffn_up(h, ffn_w_gate[l], ffn_w_up[l], ffn_conv[l].reshape(9, d_ff),
               ffn_conv_b[l][None], seq_len, min(TM_FFN, seq_len), TN_FFN)
    out = down_proj_residual(t, ffn_w_down[l].astype(BF16), x1, g2, norm_post_ffn[l][None], seq_len,
                             min(TM_DOWN, seq_len), TK_DOWN)
    return out.reshape(bsz, seq_len, d)
```
